```python
import math
import jax, jax.numpy as jnp
from jax import lax
import numpy as np

D_MODEL = 1024
BATCH = 1
SEQ = 16384
DEPTH = 4
DEC_BATCH = 16
DEC_SEQ = 16
PAST_LEN = 2048

CHUNK = 64
CONV_WIDTH = 4
RMS_EPS = 1e-6
MIX_WIDTH = D_MODEL
LRU_WIDTH = MIX_WIDTH // 4
LRU_BLOCKS = 4
LRU_BLOCK = LRU_WIDTH // LRU_BLOCKS
LRU_C = 8.0
SSD_INNER = MIX_WIDTH // 4
SSD_HEAD_DIM = 64
SSD_HEADS = SSD_INNER // SSD_HEAD_DIM
SSD_GROUPS = 2
SSD_STATE = 128
SSD_CONV_DIM = SSD_INNER + 2 * SSD_GROUPS * SSD_STATE
GDN_WIDTH = MIX_WIDTH // 4
GDN_HEAD_DIM = 64
GDN_HEADS = GDN_WIDTH // GDN_HEAD_DIM
GDN_CONV_DIM = 3 * GDN_WIDTH
S5_WIDTH = MIX_WIDTH - LRU_WIDTH - SSD_INNER - GDN_WIDTH
S5_GROUP_CH = 16
S5_GROUPS = S5_WIDTH // S5_GROUP_CH
S5_STATE = 64
MEM_TOKENS = 256
MEM_HEADS = 4
MEM_HEAD_DIM = D_MODEL // MEM_HEADS
FFN_HIDDEN = ((8 * D_MODEL + 3 * 256 - 1) // (3 * 256)) * 256
IN_SIZES = (LRU_WIDTH, LRU_WIDTH, SSD_INNER, SSD_CONV_DIM, SSD_HEADS,
            GDN_CONV_DIM, GDN_WIDTH, GDN_HEADS, GDN_HEADS, S5_WIDTH)
IN_COLS = sum(IN_SIZES)

kernel_name = "hybrid_streaming_encoder_step"


def rmsnorm(x, g, eps=RMS_EPS):
    xf = x.astype(jnp.float32)
    y = xf * lax.rsqrt(jnp.mean(xf * xf, axis=-1, keepdims=True) + eps)
    return (y * g.astype(jnp.float32)).astype(x.dtype)


def l2norm(x, eps=1e-6):
    return x * lax.rsqrt(jnp.sum(x * x, axis=-1, keepdims=True) + eps)


def _split_cols(u):
    outs, start = [], 0
    for size in IN_SIZES:
        outs.append(u[..., start:start + size])
        start += size
    return outs


def _pad_time(t, pad):
    if pad == 0:
        return t
    widths = [(0, 0)] * t.ndim
    widths[1] = (0, pad)
    return jnp.pad(t, widths)


def causal_conv(x, buf, w, b=None):
    T = x.shape[1]
    xp = jnp.concatenate([buf.astype(x.dtype), x], axis=1)
    y = xp[:, 0:T] * w[0]
    for j in range(1, CONV_WIDTH):
        y = y + xp[:, j:j + T] * w[j]
    if b is not None:
        y = y + b
    return y, xp[:, T:]


def _linear_combine(e1, e2):
    a1, b1 = e1
    a2, b2 = e2
    return a2 * a1, a2 * b1 + b2


def _complex_linear_combine(e1, e2):
    ar1, ai1, br1, bi1 = e1
    ar2, ai2, br2, bi2 = e2
    return (ar2 * ar1 - ai2 * ai1, ar2 * ai1 + ai2 * ar1,
            ar2 * br1 - ai2 * bi1 + br2, ar2 * bi1 + ai2 * br1 + bi2)


def rglru_mixer(gate_in, x_in, h0, buf, conv_w, conv_b, w_r, b_r, w_i, b_i, lam):
    xc, new_buf = causal_conv(x_in, buf, conv_w, conv_b)
    B_, T, _ = xc.shape
    xb = xc.reshape(B_, T, LRU_BLOCKS, LRU_BLOCK)
    r = jax.nn.sigmoid(jnp.einsum('btnc,ncd->btnd', xb, w_r).reshape(B_, T, LRU_WIDTH) + b_r)
    i = jax.nn.sigmoid(jnp.einsum('btnc,ncd->btnd', xb, w_i).reshape(B_, T, LRU_WIDTH) + b_i)
    log_a = -LRU_C * r * jax.nn.softplus(-lam)
    a = jnp.exp(log_a)
    bt = jnp.sqrt(-jnp.expm1(2.0 * log_a)) * (i * xc)
    bt = bt.at[:, 0].add(a[:, 0] * h0)
    _, hs = lax.associative_scan(_linear_combine, (a, bt), axis=1)
    y = hs * jax.nn.gelu(gate_in)
    return y, hs[:, -1], new_buf


def ssd_scan(x, dt, a, bm, cm, h0):
    B_, T, H, P = x.shape
    cs = min(CHUNK, T)
    nc = -(-T // cs)
    pad = nc * cs - T
    rep = H // SSD_GROUPS
    bh = jnp.repeat(bm, rep, axis=2)
    ch = jnp.repeat(cm, rep, axis=2)
    x, dt, bh, ch = [_pad_time(t, pad).reshape((B_, nc, cs) + t.shape[2:]) for t in (x, dt, bh, ch)]
    g = jnp.cumsum(dt * a, axis=2)
    causal = jnp.tril(jnp.ones((cs, cs), bool))
    seg = jnp.exp(jnp.where(causal[None, None, :, :, None],
                            g[:, :, :, None, :] - g[:, :, None, :, :], -jnp.inf))
    scores = jnp.einsum('bcthn,bcshn->bctsh', ch, bh) * seg
    xdt = x * dt[..., None]
    y_diag = jnp.einsum('bctsh,bcshp->bcthp', scores, xdt)
    to_end = jnp.exp(g[:, :, -1:, :] - g)
    chunk_states = jnp.einsum('bcsh,bcshp,bcshn->bchpn', to_end, xdt, bh)
    chunk_decay = jnp.exp(g[:, :, -1, :])

    def step(h, inp):
        st, dec = inp
        return dec[..., None, None] * h + st, h

    h_last, h_start = lax.scan(step, h0.astype(jnp.float32),
                               (jnp.moveaxis(chunk_states, 1, 0), jnp.moveaxis(chunk_decay, 1, 0)))
    h_start = jnp.moveaxis(h_start, 0, 1)
    y_off = jnp.einsum('bcthn,bchpn->bcthp', ch * jnp.exp(g)[..., None], h_start)
    y = (y_diag + y_off).reshape(B_, nc * cs, H, P)[:, :T]
    return y, h_last


def ssd_mixer(z, xbc, dt_raw, h0, buf, conv_w, conv_b, dt_bias, a_log, d_skip, norm_g):
    xbc, new_buf = causal_conv(xbc, buf, conv_w, conv_b)
    xbc = jax.nn.silu(xbc)
    B_, T, _ = xbc.shape
    gn = SSD_GROUPS * SSD_STATE
    xs = xbc[..., :SSD_INNER].reshape(B_, T, SSD_HEADS, SSD_HEAD_DIM)
    bm = xbc[..., SSD_INNER:SSD_INNER + gn].reshape(B_, T, SSD_GROUPS, SSD_STATE)
    cm = xbc[..., SSD_INNER + gn:].reshape(B_, T, SSD_GROUPS, SSD_STATE)
    dt = jax.nn.softplus(dt_raw + dt_bias)
    a = -jnp.exp(a_log.astype(jnp.float32))
    y, h_last = ssd_scan(xs, dt, a, bm, cm, h0)
    y = y + d_skip[:, None] * xs
    gs = SSD_INNER // SSD_GROUPS
    yg = (y.reshape(B_, T, SSD_INNER) * jax.nn.silu(z)).reshape(B_, T, SSD_GROUPS, gs)
    yg = rmsnorm(yg, norm_g.reshape(SSD_GROUPS, gs))
    return yg.reshape(B_, T, SSD_INNER), h_last, new_buf


def gated_delta_scan(q, k, v, g, beta, s0):
    B_, T, H, DK = q.shape
    DV = v.shape[-1]
    cs = min(CHUNK, T)
    nc = -(-T // cs)
    pad = nc * cs - T

    def blk(t):
        t = _pad_time(t, pad).reshape((B_, nc, cs) + t.shape[2:])
        return jnp.swapaxes(t, 2, 3)

    q, k, v, g, beta = [blk(t) for t in (q, k, v, g, beta)]
    gc = jnp.cumsum(g, axis=-1)
    diff = gc[..., :, None] - gc[..., None, :]
    incl = jnp.tril(jnp.ones((cs, cs), bool))
    strict = jnp.tril(jnp.ones((cs, cs), bool), -1)
    decay = jnp.exp(jnp.where(incl, diff, -jnp.inf))
    kk = jnp.einsum('bnhtd,bnhsd->bnhts', k, k)
    lower = jnp.where(strict, beta[..., :, None] * kk * decay, 0.0)
    m = lower + jnp.eye(cs, dtype=lower.dtype)
    rhs = jnp.concatenate([beta[..., None] * v, (beta * jnp.exp(gc))[..., None] * k], axis=-1)
    sol = lax.linalg.triangular_solve(m, rhs, left_side=True, lower=True, unit_diagonal=True)
    u, w = sol[..., :DV], sol[..., DV:]
    qk = jnp.einsum('bnhtd,bnhsd->bnhts', q, k) * decay
    q_dec = q * jnp.exp(gc)[..., None]
    k_dec = k * jnp.exp(gc[..., -1:] - gc)[..., None]
    chunk_decay = jnp.exp(gc[..., -1])

    def step(s, inp):
        qd, kd, uc, wc, qkc, dec = inp
        delta = uc - jnp.einsum('bhsk,bhkv->bhsv', wc, s)
        o = jnp.einsum('bhtk,bhkv->bhtv', qd, s) + jnp.einsum('bhts,bhsv->bhtv', qkc, delta)
        s = dec[..., None, None] * s + jnp.einsum('bhsk,bhsv->bhkv', kd, delta)
        return s, o

    xs = tuple(jnp.moveaxis(t, 1, 0) for t in (q_dec, k_dec, u, w, qk, chunk_decay))
    s_last, o = lax.scan(step, s0.astype(jnp.float32), xs)
    o = jnp.swapaxes(jnp.moveaxis(o, 0, 1), 2, 3).reshape(B_, nc * cs, H, DV)[:, :T]
    return o, s_last


def gdn_mixer(qkv, z, beta_raw, a_raw, s0, buf, conv_w, dt_bias, a_log, norm_g):
    qkv, new_buf = causal_conv(qkv, buf, conv_w)
    qkv = jax.nn.silu(qkv)
    B_, T, _ = qkv.shape
    q, k, v = [qkv[..., j * GDN_WIDTH:(j + 1) * GDN_WIDTH].reshape(B_, T, GDN_HEADS, GDN_HEAD_DIM)
               for j in range(3)]
    q = l2norm(q) * GDN_HEAD_DIM ** -0.5
    k = l2norm(k)
    beta = jax.nn.sigmoid(beta_raw)
    g = -jnp.exp(a_log) * jax.nn.softplus(a_raw + dt_bias)
    o, s_last = gated_delta_scan(q, k, v, g, beta, s0)
    o = rmsnorm(o, norm_g) * jax.nn.silu(z.reshape(B_, T, GDN_HEADS, GDN_HEAD_DIM))
    return o.reshape(B_, T, GDN_WIDTH), s_last, new_buf


def s5_mixer(u_in, s0_re, s0_im, a_re, a_im, log_dt, b_re, b_im, c_re, c_im, d_skip, w_glu, b_glu):
    B_, T, _ = u_in.shape
    u = u_in.reshape(B_, T, S5_GROUPS, S5_GROUP_CH)
    a_re = a_re.astype(jnp.float32)
    a_im = a_im.astype(jnp.float32)
    step = jnp.exp(log_dt.astype(jnp.float32))[:, None]
    mag = jnp.exp(a_re * step)
    ang = a_im * step
    lb_re, lb_im = mag * jnp.cos(ang), mag * jnp.sin(ang)
    den = a_re * a_re + a_im * a_im
    f_re = ((lb_re - 1.0) * a_re + lb_im * a_im) / den
    f_im = (lb_im * a_re - (lb_re - 1.0) * a_im) / den
    b_re = b_re.astype(jnp.float32)
    b_im = b_im.astype(jnp.float32)
    bb_re = f_re[..., None] * b_re - f_im[..., None] * b_im
    bb_im = f_re[..., None] * b_im + f_im[..., None] * b_re
    bu_re = jnp.einsum('btgc,gnc->btgn', u, bb_re)
    bu_im = jnp.einsum('btgc,gnc->btgn', u, bb_im)
    bu_re = bu_re.at[:, 0].add(lb_re * s0_re - lb_im * s0_im)
    bu_im = bu_im.at[:, 0].add(lb_re * s0_im + lb_im * s0_re)
    shape = bu_re.shape
    _, _, x_re, x_im = lax.associative_scan(
        _complex_linear_combine,
        (jnp.broadcast_to(lb_re, shape), jnp.broadcast_to(lb_im, shape), bu_re, bu_im), axis=1)
    y = jnp.einsum('btgn,gcn->btgc', x_re, c_re) - jnp.einsum('btgn,gcn->btgc', x_im, c_im)
    y = y.reshape(B_, T, S5_WIDTH) + d_skip * u_in
    y = jax.nn.gelu(y)
    y = y * jax.nn.sigmoid(jnp.einsum('btc,cd->btd', y, w_glu) + b_glu)
    return y, x_re[:, -1], x_im[:, -1]


def memory_kv(mem, g, w_k, w_v):
    m = rmsnorm(mem, g)
    B_, M, _ = m.shape
    k = jnp.einsum('bmd,de->bme', m, w_k).reshape(B_, M, MEM_HEADS, MEM_HEAD_DIM)
    v = jnp.einsum('bmd,de->bme', m, w_v).reshape(B_, M, MEM_HEADS, MEM_HEAD_DIM)
    return k, v


def cross_attend(h, k, v, w_q, w_o):
    B_, T, _ = h.shape
    q = jnp.einsum('btd,de->bte', h, w_q).reshape(B_, T, MEM_HEADS, MEM_HEAD_DIM)
    s = jnp.einsum('bthe,bmhe->bhtm', q, k.astype(q.dtype)).astype(jnp.float32) * (MEM_HEAD_DIM ** -0.5)
    p = jax.nn.softmax(s, axis=-1).astype(h.dtype)
    o = jnp.einsum('bhtm,bmhe->bthe', p, v.astype(h.dtype)).reshape(B_, T, D_MODEL)
    return jnp.einsum('btd,de->bte', o, w_o)


def swiglu(h, w_gate, w_up, w_down):
    a = jnp.einsum('btd,df->btf', h, w_gate)
    b = jnp.einsum('btd,df->btf', h, w_up)
    return jnp.einsum('btf,fd->btd', jax.nn.silu(a) * b, w_down)


def setup_inputs(seed: int = 0) -> dict:
    key = jax.random.key(seed)
    ks = iter(list(jax.random.split(key, 64)))
    f32 = jnp.float32

    def nrm(shape, scale):
        return jax.random.normal(next(ks), shape, f32) * scale

    def unif(shape, lo, hi):
        return jax.random.uniform(next(ks), shape, f32, lo, hi)

    def gain(shape):
        return 1.0 + nrm(shape, 0.02)

    def inv_softplus_dt(shape):
        dt = jnp.exp(unif(shape, math.log(1e-3), math.log(1e-1)))
        return dt + jnp.log(-jnp.expm1(-dt))

    L = DEPTH
    K1 = CONV_WIDTH - 1
    d = {}
    d["x_prompt"] = nrm((BATCH, SEQ, D_MODEL), 1.0)
    d["x_sample"] = nrm((DEC_BATCH, DEC_SEQ, D_MODEL), 1.0)
    d["mem_prompt"] = nrm((BATCH, MEM_TOKENS, D_MODEL), 1.0)
    d["state_lru_h"] = nrm((L, DEC_BATCH, LRU_WIDTH), 0.5)
    d["cache_lru_conv"] = nrm((L, DEC_BATCH, K1, LRU_WIDTH), 1.0)
    d["state_ssd"] = nrm((L, DEC_BATCH, SSD_HEADS, SSD_HEAD_DIM, SSD_STATE), 0.1)
    d["cache_ssd_conv"] = nrm((L, DEC_BATCH, K1, SSD_CONV_DIM), 1.0)
    d["state_gdn"] = nrm((L, DEC_BATCH, GDN_HEADS, GDN_HEAD_DIM, GDN_HEAD_DIM), 0.1)
    d["cache_gdn_conv"] = nrm((L, DEC_BATCH, K1, GDN_CONV_DIM), 1.0)
    d["state_s5_re"] = nrm((L, DEC_BATCH, S5_GROUPS, S5_STATE), 0.05)
    d["state_s5_im"] = nrm((L, DEC_BATCH, S5_GROUPS, S5_STATE), 0.05)
    d["cache_mem_k"] = nrm((L, DEC_BATCH, MEM_TOKENS, MEM_HEADS, MEM_HEAD_DIM), 1.0)
    d["cache_mem_v"] = nrm((L, DEC_BATCH, MEM_TOKENS, MEM_HEADS, MEM_HEAD_DIM), 1.0)
    d["norm_mix_g"] = gain((L, D_MODEL))
    d["w_in"] = nrm((L, D_MODEL, IN_COLS), D_MODEL ** -0.5)
    d["w_out"] = nrm((L, MIX_WIDTH, D_MODEL), 0.5 * MIX_WIDTH ** -0.5)
    d["lru_conv_w"] = nrm((L, CONV_WIDTH, LRU_WIDTH), CONV_WIDTH ** -0.5)
    d["lru_conv_b"] = nrm((L, LRU_WIDTH), 0.02)
    d["lru_w_r"] = nrm((L, LRU_BLOCKS, LRU_BLOCK, LRU_BLOCK), LRU_BLOCK ** -0.5)
    d["lru_b_r"] = nrm((L, LRU_WIDTH), 0.1)
    d["lru_w_i"] = nrm((L, LRU_BLOCKS, LRU_BLOCK, LRU_BLOCK), LRU_BLOCK ** -0.5)
    d["lru_b_i"] = nrm((L, LRU_WIDTH), 0.1)
    s_lru = unif((L, LRU_WIDTH), 0.9, 0.999) ** (1.0 / LRU_C)
    d["lru_lambda"] = jnp.log(s_lru) - jnp.log1p(-s_lru)
    d["ssd_conv_w"] = nrm((L, CONV_WIDTH, SSD_CONV_DIM), CONV_WIDTH ** -0.5)
    d["ssd_conv_b"] = nrm((L, SSD_CONV_DIM), 0.02)
    d["ssd_dt_bias"] = inv_softplus_dt((L, SSD_HEADS))
    d["ssd_a_log"] = jnp.log(unif((L, SSD_HEADS), 1.0, 16.0))
    d["ssd_d"] = gain((L, SSD_HEADS))
    d["ssd_norm_g"] = gain((L, SSD_INNER))
    d["gdn_conv_w"] = nrm((L, CONV_WIDTH, GDN_CONV_DIM), CONV_WIDTH ** -0.5)
    d["gdn_dt_bias"] = inv_softplus_dt((L, GDN_HEADS))
    d["gdn_a_log"] = jnp.log(unif((L, GDN_HEADS), 1.0, 16.0))
    d["gdn_norm_g"] = gain((L, GDN_HEAD_DIM))
    d["s5_a_re"] = -0.5 + nrm((L, S5_GROUPS, S5_STATE), 0.01)
    d["s5_a_im"] = jnp.pi * jnp.arange(S5_STATE, dtype=f32) + nrm((L, S5_GROUPS, S5_STATE), 0.01)
    d["s5_log_dt"] = unif((L, S5_GROUPS), math.log(1e-3), math.log(1e-1))
    d["s5_b_re"] = nrm((L, S5_GROUPS, S5_STATE, S5_GROUP_CH), (2 * S5_GROUP_CH) ** -0.5)
    d["s5_b_im"] = nrm((L, S5_GROUPS, S5_STATE, S5_GROUP_CH), (2 * S5_GROUP_CH) ** -0.5)
    d["s5_c_re"] = nrm((L, S5_GROUPS, S5_GROUP_CH, S5_STATE), S5_STATE ** -0.5)
    d["s5_c_im"] = nrm((L, S5_GROUPS, S5_GROUP_CH, S5_STATE), S5_STATE ** -0.5)
    d["s5_d"] = gain((L, S5_WIDTH))
    d["s5_w_glu"] = nrm((L, S5_WIDTH, S5_WIDTH), S5_WIDTH ** -0.5)
    d["s5_b_glu"] = nrm((L, S5_WIDTH), 0.02)
    d["norm_mem_g"] = gain((L, D_MODEL))
    d["norm_cross_g"] = gain((L, D_MODEL))
    d["w_cq"] = nrm((L, D_MODEL, D_MODEL), D_MODEL ** -0.5)
    d["w_ck"] = nrm((L, D_MODEL, D_MODEL), D_MODEL ** -0.5)
    d["w_cv"] = nrm((L, D_MODEL, D_MODEL), D_MODEL ** -0.5)
    d["w_co"] = nrm((L, D_MODEL, D_MODEL), 0.5 * D_MODEL ** -0.5)
    d["norm_ffn_g"] = gain((L, D_MODEL))
    d["w_ffn_gate"] = nrm((L, D_MODEL, FFN_HIDDEN), D_MODEL ** -0.5)
    d["w_ffn_up"] = nrm((L, D_MODEL, FFN_HIDDEN), D_MODEL ** -0.5)
    d["w_ffn_down"] = nrm((L, FFN_HIDDEN, D_MODEL), 0.5 * FFN_HIDDEN ** -0.5)
    d["norm_final_g"] = gain((D_MODEL,))
    return d


def reference(x_prompt, x_sample, mem_prompt,
              state_lru_h, cache_lru_conv, state_ssd, cache_ssd_conv, state_gdn, cache_gdn_conv,
              state_s5_re, state_s5_im, cache_mem_k, cache_mem_v,
              norm_mix_g, w_in, w_out,
              lru_conv_w, lru_conv_b, lru_w_r, lru_b_r, lru_w_i, lru_b_i, lru_lambda,
              ssd_conv_w, ssd_conv_b, ssd_dt_bias, ssd_a_log, ssd_d, ssd_norm_g,
              gdn_conv_w, gdn_dt_bias, gdn_a_log, gdn_norm_g,
              s5_a_re, s5_a_im, s5_log_dt, s5_b_re, s5_b_im, s5_c_re, s5_c_im, s5_d, s5_w_glu, s5_b_glu,
              norm_mem_g, norm_cross_g, w_cq, w_ck, w_cv, w_co,
              norm_ffn_g, w_ffn_gate, w_ffn_up, w_ffn_down, norm_final_g):
    f32 = jnp.float32

    def layer(l, x, mem_k, mem_v, st):
        lru_h0, lru_buf0, ssd_h0, ssd_buf0, gdn_s0, gdn_buf0, s5_re0, s5_im0 = st
        h = rmsnorm(x, norm_mix_g[l])
        u = jnp.einsum('btd,dc->btc', h, w_in[l]).astype(f32)
        a_gate, a_x, b_z, b_xbc, b_dt, c_qkv, c_z, c_beta, c_a, d_u = _split_cols(u)
        y_a, lru_h, lru_buf = rglru_mixer(a_gate, a_x, lru_h0, lru_buf0, lru_conv_w[l], lru_conv_b[l],
                                          lru_w_r[l], lru_b_r[l], lru_w_i[l], lru_b_i[l], lru_lambda[l])
        y_b, ssd_h, ssd_buf = ssd_mixer(b_z, b_xbc, b_dt, ssd_h0, ssd_buf0, ssd_conv_w[l], ssd_conv_b[l],
                                        ssd_dt_bias[l], ssd_a_log[l], ssd_d[l], ssd_norm_g[l])
        y_c, gdn_s, gdn_buf = gdn_mixer(c_qkv, c_z, c_beta, c_a, gdn_s0, gdn_buf0, gdn_conv_w[l],
                                        gdn_dt_bias[l], gdn_a_log[l], gdn_norm_g[l])
        y_d, s5_re, s5_im = s5_mixer(d_u, s5_re0, s5_im0, s5_a_re[l], s5_a_im[l], s5_log_dt[l],
                                     s5_b_re[l], s5_b_im[l], s5_c_re[l], s5_c_im[l], s5_d[l],
                                     s5_w_glu[l], s5_b_glu[l])
        mix = jnp.concatenate([y_a, y_b, y_c, y_d], axis=-1).astype(x.dtype)
        x = x + jnp.einsum('btc,cd->btd', mix, w_out[l])
        x = x + cross_attend(rmsnorm(x, norm_cross_g[l]), mem_k, mem_v, w_cq[l], w_co[l])
        x = x + swiglu(rmsnorm(x, norm_ffn_g[l]), w_ffn_gate[l], w_ffn_up[l], w_ffn_down[l])
        return x, (lru_h, lru_buf, ssd_h, ssd_buf, gdn_s, gdn_buf, s5_re, s5_im)

    bp = x_prompt.shape[0]
    k1 = CONV_WIDTH - 1
    zero_states = (jnp.zeros((bp, LRU_WIDTH), f32),
                   jnp.zeros((bp, k1, LRU_WIDTH), f32),
                   jnp.zeros((bp, SSD_HEADS, SSD_HEAD_DIM, SSD_STATE), f32),
                   jnp.zeros((bp, k1, SSD_CONV_DIM), f32),
                   jnp.zeros((bp, GDN_HEADS, GDN_HEAD_DIM, GDN_HEAD_DIM), f32),
                   jnp.zeros((bp, k1, GDN_CONV_DIM), f32),
                   jnp.zeros((bp, S5_GROUPS, S5_STATE), f32),
                   jnp.zeros((bp, S5_GROUPS, S5_STATE), f32))

    xp, xs = x_prompt, x_sample
    p_states, s_states, p_mk, p_mv = [], [], [], []
    for l in range(DEPTH):
        mk, mv = memory_kv(mem_prompt, norm_mem_g[l], w_ck[l], w_cv[l])
        xp, sp = layer(l, xp, mk, mv, zero_states)
        s_in = (state_lru_h[l], cache_lru_conv[l], state_ssd[l], cache_ssd_conv[l],
                state_gdn[l], cache_gdn_conv[l], state_s5_re[l], state_s5_im[l])
        xs, ss = layer(l, xs, cache_mem_k[l], cache_mem_v[l], s_in)
        p_states.append(sp)
        s_states.append(ss)
        p_mk.append(mk)
        p_mv.append(mv)

    y_prompt = rmsnorm(xp, norm_final_g)
    y_sample = rmsnorm(xs, norm_final_g)

    def stack(states, j):
        return jnp.stack([st[j] for st in states], axis=0)

    p_lru_h, p_lru_conv, p_ssd, p_ssd_conv = stack(p_states, 0), stack(p_states, 1), stack(p_states, 2), stack(p_states, 3)
    p_gdn, p_gdn_conv, p_s5_re, p_s5_im = stack(p_states, 4), stack(p_states, 5), stack(p_states, 6), stack(p_states, 7)
    p_mem_k = jnp.stack(p_mk, axis=0)
    p_mem_v = jnp.stack(p_mv, axis=0)
    s_lru_h, s_lru_conv, s_ssd, s_ssd_conv = stack(s_states, 0), stack(s_states, 1), stack(s_states, 2), stack(s_states, 3)
    s_gdn, s_gdn_conv, s_s5_re, s_s5_im = stack(s_states, 4), stack(s_states, 5), stack(s_states, 6), stack(s_states, 7)
    return (y_prompt, y_sample,
            p_lru_h, p_lru_conv, p_ssd, p_ssd_conv, p_gdn, p_gdn_conv, p_s5_re, p_s5_im, p_mem_k, p_mem_v,
            s_lru_h, s_lru_conv, s_ssd, s_ssd_conv, s_gdn, s_gdn_conv, s_s5_re, s_s5_im)
```

```python
import functools
import math

import jax
import jax.numpy as jnp
import numpy as np
from jax import lax
from jax.experimental import pallas as pl
from jax.experimental.pallas import tpu as pltpu

F32 = jnp.float32
MXU_DTYPE = jnp.bfloat16

D_MODEL = 1024
DEPTH = 4
CONV_WIDTH = 4
RMS_EPS = 1e-6
LRU_WIDTH = 256
LRU_BLOCKS = 4
LRU_C = 8.0
SSD_INNER = 256
SSD_HEADS = 4
SSD_GROUPS = 2
SSD_STATE = 128
SSD_CONV_DIM = 768
GDN_WIDTH = 256
GDN_HEAD_DIM = 64
GDN_HEADS = 4
GDN_CONV_DIM = 768
S5_WIDTH = 256
S5_GROUP_CH = 16
S5_GROUPS = 16
S5_STATE = 64
S5_N = S5_GROUPS * S5_STATE
MEM_TOKENS = 256
MEM_HEADS = 4
MEM_HEAD_DIM = 256
FFN_HIDDEN = 2816
IN_COLS = 2828

U_GATE, U_LRUX, U_SSDZ, U_XBC, U_QKV, U_GDNZ, U_S5, U_SMALL = 0, 256, 512, 768, 1536, 2304, 2560, 2816
U_COLS = 2944
SM_DT, SM_BETA, SM_A = 0, 4, 8

LANE = 128
SUBLANE = 8
VMEM_LIMIT = 56 * 1024 * 1024
HIST = SUBLANE


def _mm(a, b):
    return jnp.dot(a.astype(MXU_DTYPE), b.astype(MXU_DTYPE), preferred_element_type=F32)


def _mm_nt(a, b):
    return lax.dot_general(a.astype(MXU_DTYPE), b.astype(MXU_DTYPE), (((1,), (1,)), ((), ())),
                           preferred_element_type=F32)


def _mm_tn(a, b):
    return lax.dot_general(a.astype(MXU_DTYPE), b.astype(MXU_DTYPE), (((0,), (0,)), ((), ())),
                           preferred_element_type=F32)


def _mm_exact(a, b):
    return jnp.dot(a, b, preferred_element_type=F32, precision=lax.Precision.HIGHEST)


def _rms(x, g):
    ms = jnp.mean(x * x, axis=-1, keepdims=True)
    return x * lax.rsqrt(ms + RMS_EPS) * g


def _silu(x):
    return x * jax.nn.sigmoid(x)


def _gelu(x):
    return jax.nn.gelu(x, approximate=True)


def _shift_rows(x, k, row, fill):
    return jnp.where(row >= k, pltpu.roll(x, k, 0), fill)


def _in_proj_kernel(x_ref, g_ref, w_ref, u_ref):
    h = _rms(x_ref[...], g_ref[...])
    u_ref[...] = _mm(h, w_ref[...])


def _in_proj(x2d, g_all, w_all, l):
    R = x2d.shape[0]
    TM = min(R, 256)
    return pl.pallas_call(
        _in_proj_kernel,
        grid=(R // TM,),
        in_specs=[pl.BlockSpec((TM, D_MODEL), lambda i: (i, 0)),
                  pl.BlockSpec((None, 1, D_MODEL), lambda i: (l, 0, 0)),
                  pl.BlockSpec((None, D_MODEL, U_COLS), lambda i: (l, 0, 0))],
        out_specs=pl.BlockSpec((TM, U_COLS), lambda i: (i, 0)),
        out_shape=jax.ShapeDtypeStruct((R, U_COLS), F32),
        compiler_params=pltpu.CompilerParams(dimension_semantics=("parallel",), vmem_limit_bytes=VMEM_LIMIT),
        name="in_proj",
    )(x2d, g_all, w_all)


def _conv_block(xp_ref, x, w_ref, w_row0, TB):
    xp_ref[pl.ds(HIST, TB), :] = x
    y = x * w_ref[w_row0 + CONV_WIDTH - 1:w_row0 + CONV_WIDTH, :]
    for j in range(CONV_WIDTH - 1):
        off = HIST - (CONV_WIDTH - 1) + j
        y = y + xp_ref[pl.ds(off, TB), :] * w_ref[w_row0 + j:w_row0 + j + 1, :]
    return y


def _inv_unit_lower(low, Q):
    r = lax.broadcasted_iota(jnp.int32, (Q, Q), 0)
    c = lax.broadcasted_iota(jnp.int32, (Q, Q), 1)
    p = -low
    inv = jnp.where(r == c, 1.0, 0.0) + p
    for _ in range(int(math.log2(Q)) - 1):
        p = _mm_exact(p, p)
        inv = inv + _mm_exact(inv, p)
    return inv


def _mixer_kernel(TB, QG, NT,
                  u_ref, lru_h0, lru_b0, ssd_s0, ssd_b0, gdn_s0, gdn_b0, s5r0, s5i0,
                  v256, v768, v128, v1024, wri, bmat, cmat, wglu,
                  mix_ref, lru_h_o, lru_b_o, ssd_s_o, ssd_b_o, gdn_s_o, gdn_b_o, s5r_o, s5i_o,
                  xp_lru, xp_ssd, xp_gdn, h_lru, s_ssd, s_gdn, x_s5):
    t = pl.program_id(1)
    K1 = CONV_WIDTH - 1

    @pl.when(t == 0)
    def _init():
        for xp, b0 in ((xp_lru, lru_b0), (xp_ssd, ssd_b0), (xp_gdn, gdn_b0)):
            xp[pl.ds(0, HIST), :] = jnp.zeros((HIST, xp.shape[1]), F32)
            xp[pl.ds(HIST - K1, K1), :] = b0[0]
        h_lru[...] = jnp.zeros(h_lru.shape, F32)
        h_lru[0:1, :] = lru_h0[0]
        s_ssd[...] = ssd_s0[0]
        s_gdn[...] = gdn_s0[0]
        x_s5[...] = jnp.zeros(x_s5.shape, F32)
        x_s5[0:1, :] = s5r0[0]
        x_s5[1:2, :] = s5i0[0]

    row256 = lax.broadcasted_iota(jnp.int32, (TB, LRU_WIDTH), 0)
    gate = u_ref[0, :, U_GATE:U_GATE + LRU_WIDTH]
    xc = _conv_block(xp_lru, u_ref[0, :, U_LRUX:U_LRUX + LRU_WIDTH], v256, 0, TB) + v256[4:5, :]
    ri = _mm(xc, wri[...])
    r = jax.nn.sigmoid(ri[:, :LRU_WIDTH] + v256[5:6, :])
    ig = jax.nn.sigmoid(ri[:, LRU_WIDTH:] + v256[6:7, :])
    log_a = (-LRU_C) * r * v256[7:8, :]
    a = jnp.exp(log_a)
    bt = jnp.sqrt(-jnp.tanh(log_a) * (a * a + 1.0)) * (ig * xc)
    bt = bt + jnp.where(row256 == 0, a * h_lru[0:1, :], 0.0)
    acc_a, hs = a, bt
    k = 1
    while k < TB:
        hs_new = hs + acc_a * _shift_rows(hs, k, row256, 0.0)
        if 2 * k < TB:
            acc_a = acc_a * _shift_rows(acc_a, k, row256, 1.0)
        hs = hs_new
        k *= 2
    h_lru[0:1, :] = hs[TB - 1:TB, :]
    mix_ref[0, :, 0:LRU_WIDTH] = hs * _gelu(gate)

    row128 = lax.broadcasted_iota(jnp.int32, (TB, LANE), 0)
    lane128 = lax.broadcasted_iota(jnp.int32, (TB, LANE), 1)
    small = u_ref[0, :, U_SMALL:U_SMALL + LANE]
    sp = jax.nn.softplus(small + v128[0:1, :])
    sig = jax.nn.sigmoid(small)
    gcum = sp * v128[1:2, :]
    rk = jnp.where(lane128 < SM_BETA, row128, row128 & (QG - 1))
    k = 1
    while k < TB:
        gcum = gcum + jnp.where(rk >= k, pltpu.roll(gcum, k, 0), 0.0)
        k *= 2
    if TB < LANE:
        gcum_t = jnp.concatenate([gcum, jnp.zeros((LANE - TB, LANE), F32)], axis=0).T
    else:
        gcum_t = gcum.T

    z = u_ref[0, :, U_SSDZ:U_SSDZ + SSD_INNER]
    xbc = _silu(_conv_block(xp_ssd, u_ref[0, :, U_XBC:U_XBC + SSD_CONV_DIM], v768, 0, TB) + v768[4:5, :])
    rr = lax.broadcasted_iota(jnp.int32, (TB, TB), 0)
    cc = lax.broadcasted_iota(jnp.int32, (TB, TB), 1)
    causal = rr >= cc
    lo = lane128 < (LANE // 2)
    row_s = lax.broadcasted_iota(jnp.int32, (LANE, SSD_STATE), 0) < (LANE // 2)
    y_pairs = []
    for p in range(SSD_GROUPS):
        xs_p = xbc[:, p * LANE:(p + 1) * LANE]
        b_p = xbc[:, SSD_INNER + p * SSD_STATE:SSD_INNER + (p + 1) * SSD_STATE]
        c_p = xbc[:, SSD_INNER + SSD_GROUPS * SSD_STATE + p * SSD_STATE:
                  SSD_INNER + SSD_GROUPS * SSD_STATE + (p + 1) * SSD_STATE]
        cb = _mm_nt(c_p, b_p)
        h0, h1 = 2 * p, 2 * p + 1
        xdt = xs_p * jnp.where(lo, sp[:, SM_DT + h0:SM_DT + h0 + 1], sp[:, SM_DT + h1:SM_DT + h1 + 1])
        s_pair = s_ssd[p]
        ys = []
        for h in (h0, h1):
            gcol = gcum[:, SM_DT + h:SM_DT + h + 1]
            grow = gcum_t[SM_DT + h:SM_DT + h + 1, 0:TB]
            seg = jnp.exp(jnp.where(causal, gcol - grow, -jnp.inf))
            y_diag = _mm(cb * seg, xdt)
            y_off = _mm_nt(c_p * jnp.exp(gcol), s_pair)
            ys.append(y_diag + y_off)
        y_pairs.append(jnp.where(lo, ys[0], ys[1]))
        g0 = gcum[:, SM_DT + h0:SM_DT + h0 + 1]
        g1 = gcum[:, SM_DT + h1:SM_DT + h1 + 1]
        gl0 = g0[TB - 1:TB, :]
        gl1 = g1[TB - 1:TB, :]
        to_end = jnp.where(lo, jnp.exp(gl0 - g0), jnp.exp(gl1 - g1))
        s_ssd[p] = jnp.where(row_s, jnp.exp(gl0), jnp.exp(gl1)) * s_pair + _mm_tn(xdt * to_end, b_p)
    y = jnp.concatenate(y_pairs, axis=-1) + v256[8:9, :] * xbc[:, 0:SSD_INNER]
    yg = y * _silu(z)
    gs = SSD_INNER // SSD_GROUPS
    for p in range(SSD_GROUPS):
        mix_ref[0, :, LRU_WIDTH + p * gs:LRU_WIDTH + (p + 1) * gs] = _rms(
            yg[:, p * gs:(p + 1) * gs], v256[9:10, p * gs:(p + 1) * gs])

    def head_masks(rows):
        lane = lax.broadcasted_iota(jnp.int32, (rows, GDN_WIDTH), 1)
        return [(lane >= h * GDN_HEAD_DIM) & (lane < (h + 1) * GDN_HEAD_DIM) for h in range(GDN_HEADS)]

    hmask = head_masks(TB)
    hm = head_masks(QG)

    def by_head(vals, masks):
        return jnp.where(masks[0], vals[0], jnp.where(masks[1], vals[1], jnp.where(masks[2], vals[2], vals[3])))

    def head_sum(x, masks):
        return by_head([jnp.sum(jnp.where(m, x, 0.0), axis=-1, keepdims=True) for m in masks], masks)

    qkv = _silu(_conv_block(xp_gdn, u_ref[0, :, U_QKV:U_QKV + GDN_CONV_DIM], v768, 5, TB))
    qf = qkv[:, 0:GDN_WIDTH]
    kf = qkv[:, GDN_WIDTH:2 * GDN_WIDTH]
    vf = qkv[:, 2 * GDN_WIDTH:3 * GDN_WIDTH]
    qf = qf * lax.rsqrt(head_sum(qf * qf, hmask) + 1e-6) * (GDN_HEAD_DIM ** -0.5)
    kf = kf * lax.rsqrt(head_sum(kf * kf, hmask) + 1e-6)
    beta_f = by_head([sig[:, SM_BETA + h:SM_BETA + h + 1] for h in range(GDN_HEADS)], hmask)
    gc_f = by_head([gcum[:, SM_A + h:SM_A + h + 1] for h in range(GDN_HEADS)], hmask)
    zg = u_ref[0, :, U_GDNZ:U_GDNZ + GDN_WIDTH]
    qr = lax.broadcasted_iota(jnp.int32, (QG, QG), 0)
    qc_ = lax.broadcasted_iota(jnp.int32, (QG, QG), 1)
    incl = qr >= qc_
    strict = qr > qc_
    head_shift = int(math.log2(GDN_HEAD_DIM))
    r2 = lax.shift_right_logical(lax.broadcasted_iota(jnp.int32, (GDN_WIDTH, GDN_WIDTH), 0), head_shift)
    c2 = lax.shift_right_logical(lax.broadcasted_iota(jnp.int32, (GDN_WIDTH, GDN_WIDTH), 1), head_shift)
    blockdiag = r2 == c2
    for c in range(TB // QG):
        sl = slice(c * QG, (c + 1) * QG)
        q_c, k_c, v_c, b_c, g_c = qf[sl], kf[sl], vf[sl], beta_f[sl], gc_f[sl]
        eg = jnp.exp(g_c)
        g_last = g_c[QG - 1:QG, :]
        rhs_u = b_c * v_c
        rhs_w = b_c * eg * k_c
        us, ws, qks = [], [], []
        for h in range(GDN_HEADS):
            gcol = gcum[sl, SM_A + h:SM_A + h + 1]
            grow = gcum_t[SM_A + h:SM_A + h + 1, c * QG:(c + 1) * QG]
            dec = jnp.exp(jnp.where(incl, gcol - grow, -jnp.inf))
            kk = _mm_nt(jnp.where(hm[h], k_c, 0.0), k_c)
            low = jnp.where(strict, sig[sl, SM_BETA + h:SM_BETA + h + 1] * kk * dec, 0.0)
            inv = _inv_unit_lower(low, QG)
            us.append(_mm_exact(inv, rhs_u))
            ws.append(_mm_exact(inv, rhs_w))
            qks.append(_mm_nt(jnp.where(hm[h], q_c, 0.0), k_c) * dec)
        u_c = by_head(us, hm)
        w_c = by_head(ws, hm)
        s_bd = s_gdn[...]
        delta = u_c - _mm(w_c, s_bd)
        o = _mm(q_c * eg, s_bd) + by_head([_mm(qks[h], delta) for h in range(GDN_HEADS)], hm)
        k_dec = k_c * jnp.exp(g_last - g_c)
        s_gdn[...] = s_bd * jnp.exp(g_last) + jnp.where(blockdiag, _mm_tn(k_dec, delta), 0.0)
        ms = head_sum(o * o, hm) * (1.0 / GDN_HEAD_DIM)
        o = o * lax.rsqrt(ms + RMS_EPS) * v256[10:11, :]
        mix_ref[0, sl, 2 * LRU_WIDTH:2 * LRU_WIDTH + GDN_WIDTH] = o * _silu(zg[sl])

    row_n = lax.broadcasted_iota(jnp.int32, (TB, S5_N), 0)
    us5 = u_ref[0, :, U_S5:U_S5 + S5_WIDTH]
    bu = _mm(us5, bmat[...])
    xr = bu[:, 0:S5_N]
    xi = bu[:, S5_N:2 * S5_N]
    pr = v1024[0:1, :]
    pi = v1024[1:2, :]
    sr0 = x_s5[0:1, :]
    si0 = x_s5[1:2, :]
    first = row_n == 0
    xr = xr + jnp.where(first, pr * sr0 - pi * si0, 0.0)
    xi = xi + jnp.where(first, pr * si0 + pi * sr0, 0.0)
    k = 1
    while k < TB:
        sr = _shift_rows(xr, k, row_n, 0.0)
        si = _shift_rows(xi, k, row_n, 0.0)
        xr, xi = xr + (pr * sr - pi * si), xi + (pr * si + pi * sr)
        pr, pi = pr * pr - pi * pi, 2.0 * (pr * pi)
        k *= 2
    x_s5[0:1, :] = xr[TB - 1:TB, :]
    x_s5[1:2, :] = xi[TB - 1:TB, :]
    y5 = _mm(xr, cmat[0:S5_N, :]) + _mm(xi, cmat[S5_N:2 * S5_N, :])
    y5 = _gelu(y5 + v256[11:12, :] * us5)
    mix_ref[0, :, 3 * LRU_WIDTH:3 * LRU_WIDTH + S5_WIDTH] = y5 * jax.nn.sigmoid(_mm(y5, wglu[...]) + v256[12:13, :])

    for xp in (xp_lru, xp_ssd, xp_gdn):
        xp[pl.ds(0, HIST), :] = xp[pl.ds(TB, HIST), :]

    @pl.when(t == NT - 1)
    def _emit():
        lru_h_o[0] = h_lru[0:1, :]
        lru_b_o[0] = xp_lru[pl.ds(HIST - K1, K1), :]
        ssd_b_o[0] = xp_ssd[pl.ds(HIST - K1, K1), :]
        gdn_b_o[0] = xp_gdn[pl.ds(HIST - K1, K1), :]
        ssd_s_o[0] = s_ssd[...]
        gdn_s_o[0] = s_gdn[...]
        s5r_o[0] = x_s5[0:1, :]
        s5i_o[0] = x_s5[1:2, :]


def _mixer(u, states, pk, l):
    B, T, _ = u.shape
    TB = min(T, 256)
    QG = min(TB, 64)
    NT = T // TB
    lru_h0, lru_b0, ssd_s0, ssd_b0, gdn_s0, gdn_b0, s5r0, s5i0 = states

    def per_b(shape):
        nd = len(shape)
        return pl.BlockSpec((1,) + tuple(shape[1:]), lambda b, t: (b,) + (0,) * (nd - 1))

    def per_l(arr):
        nd = arr.ndim
        return pl.BlockSpec((None,) + tuple(arr.shape[1:]), lambda b, t: (l,) + (0,) * (nd - 1))

    params = (pk["v256"], pk["v768"], pk["v128"], pk["v1024"], pk["wri"], pk["bmat"], pk["cmat"], pk["wglu"])
    state_shapes = [s.shape for s in states]
    out_shape = [jax.ShapeDtypeStruct((B, T, D_MODEL), F32)] + [jax.ShapeDtypeStruct(s, F32) for s in state_shapes]
    outs = pl.pallas_call(
        functools.partial(_mixer_kernel, TB, QG, NT),
        grid=(B, NT),
        in_specs=[pl.BlockSpec((1, TB, U_COLS), lambda b, t: (b, t, 0))]
                 + [per_b(s) for s in state_shapes] + [per_l(p) for p in params],
        out_specs=[pl.BlockSpec((1, TB, D_MODEL), lambda b, t: (b, t, 0))] + [per_b(s) for s in state_shapes],
        out_shape=out_shape,
        scratch_shapes=[pltpu.VMEM((TB + HIST, LRU_WIDTH), F32),
                        pltpu.VMEM((TB + HIST, SSD_CONV_DIM), F32),
                        pltpu.VMEM((TB + HIST, GDN_CONV_DIM), F32),
                        pltpu.VMEM((SUBLANE, LRU_WIDTH), F32),
                        pltpu.VMEM((SSD_GROUPS, LANE, SSD_STATE), F32),
                        pltpu.VMEM((GDN_WIDTH, GDN_WIDTH), F32),
                        pltpu.VMEM((SUBLANE, S5_N), F32)],
        compiler_params=pltpu.CompilerParams(dimension_semantics=("arbitrary", "arbitrary"),
                                             vmem_limit_bytes=VMEM_LIMIT),
        name="mixer",
    )(u, *states, *params)
    return outs[0], tuple(outs[1:])


def _attn_kernel(x_ref, mix_ref, wout, g_ref, wq, k_ref, v_ref, wo, o_ref):
    x1 = x_ref[0] + _mm(mix_ref[0], wout[...])
    q = _mm(_rms(x1, g_ref[...]), wq[...])
    heads = []
    for h in range(MEM_HEADS):
        sl = slice(h * MEM_HEAD_DIM, (h + 1) * MEM_HEAD_DIM)
        s = _mm_nt(q[:, sl], k_ref[0, :, sl]) * (MEM_HEAD_DIM ** -0.5)
        e = jnp.exp(s - jnp.max(s, axis=-1, keepdims=True))
        p = e / jnp.sum(e, axis=-1, keepdims=True)
        heads.append(_mm(p, v_ref[0, :, sl]))
    o_ref[0] = x1 + _mm(jnp.concatenate(heads, axis=-1), wo[...])


def _attn(x, mix, mem_k, mem_v, pk, l):
    B, T, _ = x.shape
    TB = min(T, 256)

    def per_l(arr):
        nd = arr.ndim
        return pl.BlockSpec((None,) + tuple(arr.shape[1:]), lambda b, t: (l,) + (0,) * (nd - 1))

    blk = pl.BlockSpec((1, TB, D_MODEL), lambda b, t: (b, t, 0))
    kv = pl.BlockSpec((1, MEM_TOKENS, D_MODEL), lambda b, t: (b, 0, 0))
    return pl.pallas_call(
        _attn_kernel,
        grid=(B, T // TB),
        in_specs=[blk, blk, per_l(pk["w_out"]), per_l(pk["g_cross"]), per_l(pk["w_cq"]), kv, kv, per_l(pk["w_co"])],
        out_specs=blk,
        out_shape=jax.ShapeDtypeStruct((B, T, D_MODEL), F32),
        compiler_params=pltpu.CompilerParams(dimension_semantics=("parallel", "parallel"),
                                             vmem_limit_bytes=VMEM_LIMIT),
        name="attn",
    )(x, mix, pk["w_out"], pk["g_cross"], pk["w_cq"], mem_k, mem_v, pk["w_co"])


def _ffn_kernel(final, x_ref, g_ref, wg, wu, wd, gf_ref, o_ref):
    x = x_ref[...]
    h = _rms(x, g_ref[...]).astype(MXU_DTYPE)
    y = x + _mm(_silu(_mm(h, wg[...])) * _mm(h, wu[...]), wd[...])
    if final:
        y = _rms(y, gf_ref[...])
    o_ref[...] = y


def _ffn(x2d, pk, l, final):
    R = x2d.shape[0]
    TM = min(R, 256)

    def per_l(arr):
        nd = arr.ndim
        return pl.BlockSpec((None,) + tuple(arr.shape[1:]), lambda i: (l,) + (0,) * (nd - 1))

    blk = pl.BlockSpec((TM, D_MODEL), lambda i: (i, 0))
    return pl.pallas_call(
        functools.partial(_ffn_kernel, final),
        grid=(R // TM,),
        in_specs=[blk, per_l(pk["g_ffn"]), per_l(pk["w_gate"]), per_l(pk["w_up"]), per_l(pk["w_down"]),
                  pl.BlockSpec((1, D_MODEL), lambda i: (0, 0))],
        out_specs=blk,
        out_shape=jax.ShapeDtypeStruct((R, D_MODEL), F32),
        compiler_params=pltpu.CompilerParams(dimension_semantics=("parallel",), vmem_limit_bytes=VMEM_LIMIT),
        name="ffn",
    )(x2d, pk["g_ffn"], pk["w_gate"], pk["w_up"], pk["w_down"], pk["g_final"])


def _memkv_kernel(m_ref, g_ref, wk, wv, k_ref, v_ref):
    h = _rms(m_ref[0], g_ref[...]).astype(MXU_DTYPE)
    k_ref[0] = _mm(h, wk[...])
    v_ref[0] = _mm(h, wv[...])


def _memory_kv(mem, pk, l):
    B = mem.shape[0]

    def per_l(arr):
        nd = arr.ndim
        return pl.BlockSpec((None,) + tuple(arr.shape[1:]), lambda b: (l,) + (0,) * (nd - 1))

    blk = pl.BlockSpec((1, MEM_TOKENS, D_MODEL), lambda b: (b, 0, 0))
    return pl.pallas_call(
        _memkv_kernel,
        grid=(B,),
        in_specs=[blk, per_l(pk["g_mem"]), per_l(pk["w_ck"]), per_l(pk["w_cv"])],
        out_specs=[blk, blk],
        out_shape=[jax.ShapeDtypeStruct((B, MEM_TOKENS, D_MODEL), F32)] * 2,
        compiler_params=pltpu.CompilerParams(dimension_semantics=("parallel",), vmem_limit_bytes=VMEM_LIMIT),
        name="memory_kv",
    )(mem, pk["g_mem"], pk["w_ck"], pk["w_cv"])


def _block_diag(w):
    L, n, a, b = w.shape
    return jnp.einsum('lnab,nm->lnamb', w, jnp.eye(n, dtype=w.dtype)).reshape(L, n * a, n * b)


def _pack_rows(rows, n_rows):
    L, width = rows[0].shape[0], rows[0].shape[-1]
    rows = [r.reshape(L, -1, width) for r in rows]
    used = sum(r.shape[1] for r in rows)
    return jnp.concatenate(rows + [jnp.zeros((L, n_rows - used, width), F32)], axis=1)


def _prepare(norm_mix_g, w_in, w_out, lru_conv_w, lru_conv_b, lru_w_r, lru_b_r, lru_w_i, lru_b_i, lru_lambda,
             ssd_conv_w, ssd_conv_b, ssd_dt_bias, ssd_a_log, ssd_d, ssd_norm_g,
             gdn_conv_w, gdn_dt_bias, gdn_a_log, gdn_norm_g,
             s5_a_re, s5_a_im, s5_log_dt, s5_b_re, s5_b_im, s5_c_re, s5_c_im, s5_d, s5_w_glu, s5_b_glu,
             norm_mem_g, norm_cross_g, w_cq, w_ck, w_cv, w_co,
             norm_ffn_g, w_ffn_gate, w_ffn_up, w_ffn_down, norm_final_g):
    L = DEPTH
    bf = MXU_DTYPE
    pk = {}
    pk["g_mix"] = norm_mix_g.reshape(L, 1, D_MODEL)
    pk["w_in"] = jnp.concatenate(
        [w_in[:, :, 0:1536], w_in[:, :, 1540:2564], w_in[:, :, 2572:2828], w_in[:, :, 1536:1540],
         w_in[:, :, 2564:2572], jnp.zeros((L, D_MODEL, U_COLS - IN_COLS), F32)], axis=2).astype(bf)
    pk["w_out"] = w_out.astype(bf)
    sp_lam = jax.nn.softplus(-lru_lambda)
    pk["v256"] = _pack_rows(
        [lru_conv_w, lru_conv_b, lru_b_r, lru_b_i, sp_lam, jnp.repeat(ssd_d, SSD_INNER // SSD_HEADS, axis=1),
         ssd_norm_g, jnp.tile(gdn_norm_g, (1, GDN_HEADS)), s5_d, s5_b_glu], 16)
    pk["v768"] = _pack_rows([ssd_conv_w, ssd_conv_b, gdn_conv_w], 16)
    zeros4 = jnp.zeros((L, 4), F32)
    bias = jnp.concatenate([ssd_dt_bias, zeros4, gdn_dt_bias, jnp.zeros((L, LANE - 12), F32)], axis=1)
    avec = jnp.concatenate([-jnp.exp(ssd_a_log.astype(F32)), zeros4, -jnp.exp(gdn_a_log),
                            jnp.zeros((L, LANE - 12), F32)], axis=1)
    pk["v128"] = _pack_rows([bias, avec], 8)
    pk["wri"] = jnp.concatenate([_block_diag(lru_w_r), _block_diag(lru_w_i)], axis=-1).astype(bf)
    a_re = s5_a_re.astype(F32)
    a_im = s5_a_im.astype(F32)
    step = jnp.exp(s5_log_dt.astype(F32))[:, :, None]
    mag = jnp.exp(a_re * step)
    ang = a_im * step
    lb_re, lb_im = mag * jnp.cos(ang), mag * jnp.sin(ang)
    den = a_re * a_re + a_im * a_im
    f_re = ((lb_re - 1.0) * a_re + lb_im * a_im) / den
    f_im = (lb_im * a_re - (lb_re - 1.0) * a_im) / den
    b_re = s5_b_re.astype(F32)
    b_im = s5_b_im.astype(F32)
    bb_re = f_re[..., None] * b_re - f_im[..., None] * b_im
    bb_im = f_re[..., None] * b_im + f_im[..., None] * b_re
    pk["v1024"] = _pack_rows([lb_re.reshape(L, S5_N), lb_im.reshape(L, S5_N)], 8)
    pk["bmat"] = jnp.concatenate([_block_diag(jnp.swapaxes(bb_re, 2, 3)), _block_diag(jnp.swapaxes(bb_im, 2, 3))],
                                 axis=-1).astype(bf)
    pk["cmat"] = jnp.concatenate([_block_diag(jnp.swapaxes(s5_c_re, 2, 3)), -_block_diag(jnp.swapaxes(s5_c_im, 2, 3))],
                                 axis=1).astype(bf)
    pk["wglu"] = s5_w_glu.astype(bf)
    pk["g_mem"] = norm_mem_g.reshape(L, 1, D_MODEL)
    pk["g_cross"] = norm_cross_g.reshape(L, 1, D_MODEL)
    pk["w_cq"], pk["w_ck"], pk["w_cv"], pk["w_co"] = (w.astype(bf) for w in (w_cq, w_ck, w_cv, w_co))
    pk["g_ffn"] = norm_ffn_g.reshape(L, 1, D_MODEL)
    pk["w_gate"], pk["w_up"], pk["w_down"] = (w.astype(bf) for w in (w_ffn_gate, w_ffn_up, w_ffn_down))
    pk["g_final"] = norm_final_g.reshape(1, D_MODEL)
    return pk


def _states_to_kernel(lru_h, lru_buf, ssd, ssd_buf, gdn, gdn_buf, s5_re, s5_im):
    B = lru_h.shape[0]
    eye = jnp.eye(GDN_HEADS, dtype=F32)
    gdn_bd = jnp.einsum('bhkv,hg->bhkgv', gdn, eye).reshape(B, GDN_WIDTH, GDN_WIDTH)
    return (lru_h.reshape(B, 1, LRU_WIDTH), lru_buf, ssd.reshape(B, SSD_GROUPS, LANE, SSD_STATE), ssd_buf,
            gdn_bd, gdn_buf, s5_re.reshape(B, 1, S5_N), s5_im.reshape(B, 1, S5_N))


def _states_from_kernel(st):
    lru_h, lru_buf, ssd, ssd_buf, gdn_bd, gdn_buf, s5_re, s5_im = st
    B = lru_h.shape[0]
    gdn = jnp.einsum('bhkhv->bhkv', gdn_bd.reshape(B, GDN_HEADS, GDN_HEAD_DIM, GDN_HEADS, GDN_HEAD_DIM))
    return (lru_h.reshape(B, LRU_WIDTH), lru_buf, ssd.reshape(B, SSD_HEADS, SSD_INNER // SSD_HEADS, SSD_STATE),
            ssd_buf, gdn, gdn_buf, s5_re.reshape(B, S5_GROUPS, S5_STATE), s5_im.reshape(B, S5_GROUPS, S5_STATE))


def _layer(l, x, mem_k, mem_v, states, pk, final):
    B, T, _ = x.shape
    u = _in_proj(x.reshape(B * T, D_MODEL), pk["g_mix"], pk["w_in"], l).reshape(B, T, U_COLS)
    mix, new_states = _mixer(u, states, pk, l)
    x = _attn(x, mix, mem_k, mem_v, pk, l)
    x = _ffn(x.reshape(B * T, D_MODEL), pk, l, final).reshape(B, T, D_MODEL)
    return x, new_states


def kernel(x_prompt, x_sample, mem_prompt, state_lru_h, cache_lru_conv, state_ssd, cache_ssd_conv, state_gdn, cache_gdn_conv, state_s5_re, state_s5_im, cache_mem_k, cache_mem_v, norm_mix_g, w_in, w_out, lru_conv_w, lru_conv_b, lru_w_r, lru_b_r, lru_w_i, lru_b_i, lru_lambda, ssd_conv_w, ssd_conv_b, ssd_dt_bias, ssd_a_log, ssd_d, ssd_norm_g, gdn_conv_w, gdn_dt_bias, gdn_a_log, gdn_norm_g, s5_a_re, s5_a_im, s5_log_dt, s5_b_re, s5_b_im, s5_c_re, s5_c_im, s5_d, s5_w_glu, s5_b_glu, norm_mem_g, norm_cross_g, w_cq, w_ck, w_cv, w_co, norm_ffn_g, w_ffn_gate, w_ffn_up, w_ffn_down, norm_final_g):
    pk = _prepare(norm_mix_g, w_in, w_out, lru_conv_w, lru_conv_b, lru_w_r, lru_b_r, lru_w_i, lru_b_i, lru_lambda,
                  ssd_conv_w, ssd_conv_b, ssd_dt_bias, ssd_a_log, ssd_d, ssd_norm_g,
                  gdn_conv_w, gdn_dt_bias, gdn_a_log, gdn_norm_g,
                  s5_a_re, s5_a_im, s5_log_dt, s5_b_re, s5_b_im, s5_c_re, s5_c_im, s5_d, s5_w_glu, s5_b_glu,
                  norm_mem_g, norm_cross_g, w_cq, w_ck, w_cv, w_co,
                  norm_ffn_g, w_ffn_gate, w_ffn_up, w_ffn_down, norm_final_g)
    bp = x_prompt.shape[0]
    bs = x_sample.shape[0]
    k1 = CONV_WIDTH - 1
    zero_states = _states_to_kernel(
        jnp.zeros((bp, LRU_WIDTH), F32), jnp.zeros((bp, k1, LRU_WIDTH), F32),
        jnp.zeros((bp, SSD_HEADS, SSD_INNER // SSD_HEADS, SSD_STATE), F32), jnp.zeros((bp, k1, SSD_CONV_DIM), F32),
        jnp.zeros((bp, GDN_HEADS, GDN_HEAD_DIM, GDN_HEAD_DIM), F32), jnp.zeros((bp, k1, GDN_CONV_DIM), F32),
        jnp.zeros((bp, S5_GROUPS, S5_STATE), F32), jnp.zeros((bp, S5_GROUPS, S5_STATE), F32))
    xp, xs = x_prompt, x_sample
    p_states, s_states, p_mk, p_mv = [], [], [], []
    for l in range(DEPTH):
        final = l == DEPTH - 1
        mk, mv = _memory_kv(mem_prompt, pk, l)
        xp, sp = _layer(l, xp, mk, mv, zero_states, pk, final)
        s_in = _states_to_kernel(state_lru_h[l], cache_lru_conv[l], state_ssd[l], cache_ssd_conv[l],
                                 state_gdn[l], cache_gdn_conv[l], state_s5_re[l], state_s5_im[l])
        xs, ss = _layer(l, xs, cache_mem_k[l].reshape(bs, MEM_TOKENS, D_MODEL),
                        cache_mem_v[l].reshape(bs, MEM_TOKENS, D_MODEL), s_in, pk, final)
        p_states.append(_states_from_kernel(sp))
        s_states.append(_states_from_kernel(ss))
        p_mk.append(mk.reshape(bp, MEM_TOKENS, MEM_HEADS, MEM_HEAD_DIM))
        p_mv.append(mv.reshape(bp, MEM_TOKENS, MEM_HEADS, MEM_HEAD_DIM))

    def stack(states, j):
        return jnp.stack([st[j] for st in states], axis=0)

    return ((xp, xs) + tuple(stack(p_states, j) for j in range(8)) + (jnp.stack(p_mk, axis=0), jnp.stack(p_mv, axis=0))
            + tuple(stack(s_states, j) for j in range(8)))
```

```python
import functools
import math

import jax
import jax.numpy as jnp
import numpy as np
from jax import lax
from jax.experimental import pallas as pl
from jax.experimental.pallas import tpu as pltpu

F32 = jnp.float32
MXU_DTYPE = jnp.bfloat16

D_MODEL = 1024
DEPTH = 4
CONV_WIDTH = 4
RMS_EPS = 1e-6
LRU_WIDTH = 256
LRU_BLOCKS = 4
LRU_C = 8.0
SSD_INNER = 256
SSD_HEADS = 4
SSD_GROUPS = 2
SSD_STATE = 128
SSD_CONV_DIM = 768
GDN_WIDTH = 256
GDN_HEAD_DIM = 64
GDN_HEADS = 4
GDN_CONV_DIM = 768
S5_WIDTH = 256
S5_GROUP_CH = 16
S5_GROUPS = 16
S5_STATE = 64
S5_N = S5_GROUPS * S5_STATE
MEM_TOKENS = 256
MEM_HEADS = 4
MEM_HEAD_DIM = 256
FFN_HIDDEN = 2816
IN_COLS = 2828

U_GATE, U_LRUX, U_SSDZ, U_XBC, U_QKV, U_GDNZ, U_S5, U_SMALL = 0, 256, 512, 768, 1536, 2304, 2560, 2816
U_COLS = 2944
SM_DT, SM_BETA, SM_A = 0, 4, 8

LANE = 128
SUBLANE = 8
VMEM_LIMIT = 56 * 1024 * 1024
HIST = SUBLANE


def _mm(a, b):
    return jnp.dot(a.astype(MXU_DTYPE), b.astype(MXU_DTYPE), preferred_element_type=F32)


def _mm_nt(a, b):
    return lax.dot_general(a.astype(MXU_DTYPE), b.astype(MXU_DTYPE), (((1,), (1,)), ((), ())),
                           preferred_element_type=F32)


def _mm_tn(a, b):
    return lax.dot_general(a.astype(MXU_DTYPE), b.astype(MXU_DTYPE), (((0,), (0,)), ((), ())),
                           preferred_element_type=F32)


def _rms(x, g):
    ms = jnp.mean(x * x, axis=-1, keepdims=True)
    return x * lax.rsqrt(ms + RMS_EPS) * g


def _silu(x):
    return x * jax.nn.sigmoid(x)


def _gelu(x):
    return jax.nn.gelu(x, approximate=True)


def _shift_rows(x, k, row, fill):
    return jnp.where(row >= k, pltpu.roll(x, k, 0), fill)


def _in_proj_kernel(x_ref, g_ref, w_ref, u_ref):
    h = _rms(x_ref[...], g_ref[...])
    u_ref[...] = _mm(h, w_ref[...])


def _in_proj(x2d, g_all, w_all, l):
    R = x2d.shape[0]
    TM = min(R, 256)
    return pl.pallas_call(
        _in_proj_kernel,
        grid=(R // TM,),
        in_specs=[pl.BlockSpec((TM, D_MODEL), lambda i: (i, 0)),
                  pl.BlockSpec((None, 1, D_MODEL), lambda i: (l, 0, 0)),
                  pl.BlockSpec((None, D_MODEL, U_COLS), lambda i: (l, 0, 0))],
        out_specs=pl.BlockSpec((TM, U_COLS), lambda i: (i, 0)),
        out_shape=jax.ShapeDtypeStruct((R, U_COLS), F32),
        compiler_params=pltpu.CompilerParams(dimension_semantics=("parallel",), vmem_limit_bytes=VMEM_LIMIT),
        name="in_proj",
    )(x2d, g_all, w_all)


def _conv_block(xp_ref, x, w_ref, w_row0, TB):
    xp_ref[pl.ds(HIST, TB), :] = x
    y = x * w_ref[w_row0 + CONV_WIDTH - 1:w_row0 + CONV_WIDTH, :]
    for j in range(CONV_WIDTH - 1):
        off = HIST - (CONV_WIDTH - 1) + j
        y = y + xp_ref[pl.ds(off, TB), :] * w_ref[w_row0 + j:w_row0 + j + 1, :]
    return y


def _interleave(streams):
    streams = list(streams)
    while streams:
        for s in list(streams):
            try:
                next(s)
            except StopIteration:
                streams.remove(s)


def _mixer_kernel(TB, QG, NT,
                  u_ref, lru_h0, lru_b0, ssd_s0, ssd_b0, gdn_s0, gdn_b0, s5r0, s5i0,
                  v256, v768, v128, v1024, wri, bmat, cmat, wglu,
                  mix_ref, lru_h_o, lru_b_o, ssd_s_o, ssd_b_o, gdn_s_o, gdn_b_o, s5r_o, s5i_o,
                  xp_lru, xp_ssd, xp_gdn, h_lru, s_ssd, s_gdn, x_s5, hs_scr, x5_scr):
    t = pl.program_id(1)
    K1 = CONV_WIDTH - 1
    NG = TB // SUBLANE

    @pl.when(t == 0)
    def _init():
        for xp, b0 in ((xp_lru, lru_b0), (xp_ssd, ssd_b0), (xp_gdn, gdn_b0)):
            xp[pl.ds(0, HIST), :] = jnp.zeros((HIST, xp.shape[1]), F32)
            xp[pl.ds(HIST - K1, K1), :] = b0[0]
        h_lru[...] = jnp.zeros(h_lru.shape, F32)
        h_lru[0:1, :] = lru_h0[0]
        s_ssd[...] = ssd_s0[0]
        s_gdn[...] = gdn_s0[0]
        x_s5[...] = jnp.zeros(x_s5.shape, F32)
        x_s5[0, SUBLANE - 1:SUBLANE, :] = s5r0[0]
        x_s5[1, SUBLANE - 1:SUBLANE, :] = s5i0[0]

    row128 = lax.broadcasted_iota(jnp.int32, (TB, LANE), 0)
    lane128 = lax.broadcasted_iota(jnp.int32, (TB, LANE), 1)
    small = u_ref[0, :, U_SMALL:U_SMALL + LANE]
    sp = jax.nn.softplus(small + v128[0:1, :])
    sig = jax.nn.sigmoid(small)
    gcum = sp * v128[1:2, :]
    rk = jnp.where(lane128 < SM_BETA, row128, row128 & (QG - 1))
    k = 1
    while k < TB:
        gcum = gcum + jnp.where(rk >= k, pltpu.roll(gcum, k, 0), 0.0)
        k *= 2
    if TB < LANE:
        gcum_t = jnp.concatenate([gcum, jnp.zeros((LANE - TB, LANE), F32)], axis=0).T
    else:
        gcum_t = gcum.T
    rr = lax.broadcasted_iota(jnp.int32, (TB, TB), 0)
    cc = lax.broadcasted_iota(jnp.int32, (TB, TB), 1)

    def lru():
        gate = u_ref[0, :, U_GATE:U_GATE + LRU_WIDTH]
        xc = _conv_block(xp_lru, u_ref[0, :, U_LRUX:U_LRUX + LRU_WIDTH], v256, 0, TB) + v256[4:5, :]
        ri = _mm(xc, wri[...])
        yield
        r = jax.nn.sigmoid(ri[:, :LRU_WIDTH] + v256[5:6, :])
        ig = jax.nn.sigmoid(ri[:, LRU_WIDTH:] + v256[6:7, :])
        log_a = (-LRU_C) * r * v256[7:8, :]
        a = jnp.exp(log_a)
        bt = jnp.sqrt(-jnp.tanh(log_a) * (a * a + 1.0)) * (ig * xc)
        yield
        row8 = lax.broadcasted_iota(jnp.int32, (SUBLANE, LRU_WIDTH), 0)
        h_prev = h_lru[0:1, :]
        for i in range(NG):
            a_g = a[i * SUBLANE:(i + 1) * SUBLANE]
            b_g = bt[i * SUBLANE:(i + 1) * SUBLANE]
            k = 1
            while k < SUBLANE:
                b_g = b_g + a_g * _shift_rows(b_g, k, row8, 0.0)
                a_g = a_g * _shift_rows(a_g, k, row8, 1.0)
                k *= 2
            h_g = b_g + a_g * h_prev
            h_prev = h_g[SUBLANE - 1:SUBLANE, :]
            hs_scr[pl.ds(i * SUBLANE, SUBLANE), :] = h_g
            if i % 4 == 3:
                yield
        h_lru[0:1, :] = h_prev
        mix_ref[0, :, 0:LRU_WIDTH] = hs_scr[...] * _gelu(gate)

    def ssd():
        z = u_ref[0, :, U_SSDZ:U_SSDZ + SSD_INNER]
        xbc = _silu(_conv_block(xp_ssd, u_ref[0, :, U_XBC:U_XBC + SSD_CONV_DIM], v768, 0, TB) + v768[4:5, :])
        yield
        causal = rr >= cc
        lo = lane128 < (LANE // 2)
        row_s = lax.broadcasted_iota(jnp.int32, (LANE, SSD_STATE), 0) < (LANE // 2)
        y_pairs = []
        for p in range(SSD_GROUPS):
            xs_p = xbc[:, p * LANE:(p + 1) * LANE]
            b_p = xbc[:, SSD_INNER + p * SSD_STATE:SSD_INNER + (p + 1) * SSD_STATE]
            c_p = xbc[:, SSD_INNER + SSD_GROUPS * SSD_STATE + p * SSD_STATE:
                      SSD_INNER + SSD_GROUPS * SSD_STATE + (p + 1) * SSD_STATE]
            cb = _mm_nt(c_p, b_p)
            h0, h1 = 2 * p, 2 * p + 1
            xdt = xs_p * jnp.where(lo, sp[:, SM_DT + h0:SM_DT + h0 + 1], sp[:, SM_DT + h1:SM_DT + h1 + 1])
            s_pair = s_ssd[p]
            ys = []
            for h in (h0, h1):
                gcol = gcum[:, SM_DT + h:SM_DT + h + 1]
                grow = gcum_t[SM_DT + h:SM_DT + h + 1, 0:TB]
                seg = jnp.exp(jnp.where(causal, gcol - grow, -jnp.inf))
                y_diag = _mm(cb * seg, xdt)
                y_off = _mm_nt(c_p * jnp.exp(gcol), s_pair)
                ys.append(y_diag + y_off)
                yield
            y_pairs.append(jnp.where(lo, ys[0], ys[1]))
            g0 = gcum[:, SM_DT + h0:SM_DT + h0 + 1]
            g1 = gcum[:, SM_DT + h1:SM_DT + h1 + 1]
            gl0 = g0[TB - 1:TB, :]
            gl1 = g1[TB - 1:TB, :]
            to_end = jnp.where(lo, jnp.exp(gl0 - g0), jnp.exp(gl1 - g1))
            s_ssd[p] = jnp.where(row_s, jnp.exp(gl0), jnp.exp(gl1)) * s_pair + _mm_tn(xdt * to_end, b_p)
            yield
        y = jnp.concatenate(y_pairs, axis=-1) + v256[8:9, :] * xbc[:, 0:SSD_INNER]
        yg = y * _silu(z)
        gs = SSD_INNER // SSD_GROUPS
        for p in range(SSD_GROUPS):
            mix_ref[0, :, LRU_WIDTH + p * gs:LRU_WIDTH + (p + 1) * gs] = _rms(
                yg[:, p * gs:(p + 1) * gs], v256[9:10, p * gs:(p + 1) * gs])

    def gdn():
        lane = lax.broadcasted_iota(jnp.int32, (TB, GDN_WIDTH), 1)
        hmask = [(lane >= h * GDN_HEAD_DIM) & (lane < (h + 1) * GDN_HEAD_DIM) for h in range(GDN_HEADS)]

        def by_head(vals):
            return jnp.where(hmask[0], vals[0], jnp.where(hmask[1], vals[1], jnp.where(hmask[2], vals[2], vals[3])))

        def head_sum(x):
            return by_head([jnp.sum(jnp.where(m, x, 0.0), axis=-1, keepdims=True) for m in hmask])

        qkv = _silu(_conv_block(xp_gdn, u_ref[0, :, U_QKV:U_QKV + GDN_CONV_DIM], v768, 5, TB))
        qf = qkv[:, 0:GDN_WIDTH]
        kf = qkv[:, GDN_WIDTH:2 * GDN_WIDTH]
        vf = qkv[:, 2 * GDN_WIDTH:3 * GDN_WIDTH]
        qf = qf * lax.rsqrt(head_sum(qf * qf) + 1e-6) * (GDN_HEAD_DIM ** -0.5)
        kf = kf * lax.rsqrt(head_sum(kf * kf) + 1e-6)
        beta_f = by_head([sig[:, SM_BETA + h:SM_BETA + h + 1] for h in range(GDN_HEADS)])
        gc_f = by_head([gcum[:, SM_A + h:SM_A + h + 1] for h in range(GDN_HEADS)])
        chunk_shift = int(math.log2(QG))
        same_chunk = lax.shift_right_logical(rr, chunk_shift) == lax.shift_right_logical(cc, chunk_shift)
        incl = same_chunk & (rr >= cc)
        strict = same_chunk & (rr > cc)
        eg = jnp.exp(gc_f)
        kq = _mm_nt(jnp.concatenate([jnp.where(m, kf, 0.0) for m in hmask] + [jnp.where(m, qf, 0.0) for m in hmask],
                                    axis=0), kf)
        rhs = jnp.concatenate([beta_f * vf, beta_f * eg * kf], axis=1)
        yield
        decs, pws, rs = [], [], []
        for h in range(GDN_HEADS):
            gcol = gcum[:, SM_A + h:SM_A + h + 1]
            grow = gcum_t[SM_A + h:SM_A + h + 1, 0:TB]
            dec = jnp.exp(jnp.where(incl, gcol - grow, -jnp.inf))
            low = jnp.where(strict, sig[:, SM_BETA + h:SM_BETA + h + 1] * kq[h * TB:(h + 1) * TB] * dec, 0.0)
            decs.append(dec)
            pws.append(-low)
            rs.append(-low)
        yield
        for _ in range(chunk_shift - 1):
            for h in range(GDN_HEADS):
                pws[h] = _mm(pws[h], pws[h])
                rs[h] = rs[h] + pws[h] + _mm(rs[h], pws[h])
                yield
        sols, qks = [], []
        for h in range(GDN_HEADS):
            sols.append(rhs + _mm(rs[h], rhs))
            qks.append(kq[(GDN_HEADS + h) * TB:(GDN_HEADS + h + 1) * TB] * decs[h])
            yield
        u_all = by_head([s[:, 0:GDN_WIDTH] for s in sols])
        w_all = by_head([s[:, GDN_WIDTH:2 * GDN_WIDTH] for s in sols])
        q_dec = qf * eg
        head_shift = int(math.log2(GDN_HEAD_DIM))
        r2 = lax.shift_right_logical(lax.broadcasted_iota(jnp.int32, (GDN_WIDTH, GDN_WIDTH), 0), head_shift)
        c2 = lax.shift_right_logical(lax.broadcasted_iota(jnp.int32, (GDN_WIDTH, GDN_WIDTH), 1), head_shift)
        blockdiag = r2 == c2
        s_bd = s_gdn[...]
        deltas, o_state = [], []
        for c in range(TB // QG):
            sl = slice(c * QG, (c + 1) * QG)
            ws = _mm(jnp.concatenate([w_all[sl], q_dec[sl]], axis=0), s_bd)
            delta = u_all[sl] - ws[0:QG]
            o_state.append(ws[QG:2 * QG])
            yield
            g_c = gc_f[sl]
            g_last = g_c[QG - 1:QG, :]
            s_bd = s_bd * jnp.exp(g_last) + jnp.where(blockdiag, _mm_tn(kf[sl] * jnp.exp(g_last - g_c), delta), 0.0)
            deltas.append(delta)
            yield
        s_gdn[...] = s_bd
        delta_all = jnp.concatenate(deltas, axis=0)
        o = jnp.concatenate(o_state, axis=0) + by_head([_mm(qks[h], delta_all) for h in range(GDN_HEADS)])
        yield
        ms = head_sum(o * o) * (1.0 / GDN_HEAD_DIM)
        o = o * lax.rsqrt(ms + RMS_EPS) * v256[10:11, :]
        zg = u_ref[0, :, U_GDNZ:U_GDNZ + GDN_WIDTH]
        mix_ref[0, :, 2 * LRU_WIDTH:2 * LRU_WIDTH + GDN_WIDTH] = o * _silu(zg)

    def s5():
        row_n = lax.broadcasted_iota(jnp.int32, (SUBLANE, S5_N), 0)
        us5 = u_ref[0, :, U_S5:U_S5 + S5_WIDTH]
        bu = _mm(us5, bmat[...])
        yield
        lam_r, lam_i = v1024[0:1, :], v1024[1:2, :]
        pows = []
        pr, pi = lam_r, lam_i
        k = 1
        while k < SUBLANE:
            pows.append((k, jnp.where(row_n >= k, pr, 0.0), jnp.where(row_n >= k, pi, 0.0)))
            pr, pi = pr * pr - pi * pi, 2.0 * (pr * pi)
            k *= 2
        in_r = jnp.where(row_n == 0, lam_r, 0.0)
        in_i = jnp.where(row_n == 0, lam_i, 0.0)
        g_r = x_s5[0]
        g_i = x_s5[1]
        for i in range(NG):
            p_r = pltpu.roll(g_r, 1, 0)
            p_i = pltpu.roll(g_i, 1, 0)
            g_r = bu[i * SUBLANE:(i + 1) * SUBLANE, 0:S5_N] + (in_r * p_r - in_i * p_i)
            g_i = bu[i * SUBLANE:(i + 1) * SUBLANE, S5_N:2 * S5_N] + (in_r * p_i + in_i * p_r)
            for k, pr, pi in pows:
                s_r = pltpu.roll(g_r, k, 0)
                s_i = pltpu.roll(g_i, k, 0)
                g_r, g_i = g_r + (pr * s_r - pi * s_i), g_i + (pr * s_i + pi * s_r)
            x5_scr[pl.ds(i * SUBLANE, SUBLANE), 0:S5_N] = g_r
            x5_scr[pl.ds(i * SUBLANE, SUBLANE), S5_N:2 * S5_N] = g_i
            yield
        x_s5[0] = g_r
        x_s5[1] = g_i
        y5 = _mm(x5_scr[...], cmat[...])
        y5 = _gelu(y5 + v256[11:12, :] * us5)
        yield
        mix_ref[0, :, 3 * LRU_WIDTH:3 * LRU_WIDTH + S5_WIDTH] = y5 * jax.nn.sigmoid(_mm(y5, wglu[...]) + v256[12:13, :])

    _interleave([gdn(), s5(), ssd(), lru()])

    for xp in (xp_lru, xp_ssd, xp_gdn):
        xp[pl.ds(0, HIST), :] = xp[pl.ds(TB, HIST), :]

    @pl.when(t == NT - 1)
    def _emit():
        lru_h_o[0] = h_lru[0:1, :]
        lru_b_o[0] = xp_lru[pl.ds(HIST - K1, K1), :]
        ssd_b_o[0] = xp_ssd[pl.ds(HIST - K1, K1), :]
        gdn_b_o[0] = xp_gdn[pl.ds(HIST - K1, K1), :]
        ssd_s_o[0] = s_ssd[...]
        gdn_s_o[0] = s_gdn[...]
        s5r_o[0] = x_s5[0, SUBLANE - 1:SUBLANE, :]
        s5i_o[0] = x_s5[1, SUBLANE - 1:SUBLANE, :]


def _mixer(u, states, pk, l):
    B, T, _ = u.shape
    TB = min(T, 256)
    QG = min(TB, 64)
    NT = T // TB
    lru_h0, lru_b0, ssd_s0, ssd_b0, gdn_s0, gdn_b0, s5r0, s5i0 = states

    def per_b(shape):
        nd = len(shape)
        return pl.BlockSpec((1,) + tuple(shape[1:]), lambda b, t: (b,) + (0,) * (nd - 1))

    def per_l(arr):
        nd = arr.ndim
        return pl.BlockSpec((None,) + tuple(arr.shape[1:]), lambda b, t: (l,) + (0,) * (nd - 1))

    params = (pk["v256"], pk["v768"], pk["v128"], pk["v1024"], pk["wri"], pk["bmat"], pk["cmat"], pk["wglu"])
    state_shapes = [s.shape for s in states]
    out_shape = [jax.ShapeDtypeStruct((B, T, D_MODEL), F32)] + [jax.ShapeDtypeStruct(s, F32) for s in state_shapes]
    outs = pl.pallas_call(
        functools.partial(_mixer_kernel, TB, QG, NT),
        grid=(B, NT),
        in_specs=[pl.BlockSpec((1, TB, U_COLS), lambda b, t: (b, t, 0))]
                 + [per_b(s) for s in state_shapes] + [per_l(p) for p in params],
        out_specs=[pl.BlockSpec((1, TB, D_MODEL), lambda b, t: (b, t, 0))] + [per_b(s) for s in state_shapes],
        out_shape=out_shape,
        scratch_shapes=[pltpu.VMEM((TB + HIST, LRU_WIDTH), F32),
                        pltpu.VMEM((TB + HIST, SSD_CONV_DIM), F32),
                        pltpu.VMEM((TB + HIST, GDN_CONV_DIM), F32),
                        pltpu.VMEM((SUBLANE, LRU_WIDTH), F32),
                        pltpu.VMEM((SSD_GROUPS, LANE, SSD_STATE), F32),
                        pltpu.VMEM((GDN_WIDTH, GDN_WIDTH), F32),
                        pltpu.VMEM((2, SUBLANE, S5_N), F32),
                        pltpu.VMEM((TB, LRU_WIDTH), F32),
                        pltpu.VMEM((TB, 2 * S5_N), F32)],
        compiler_params=pltpu.CompilerParams(dimension_semantics=("arbitrary", "arbitrary"),
                                             vmem_limit_bytes=VMEM_LIMIT),
        name="mixer",
    )(u, *states, *params)
    return outs[0], tuple(outs[1:])


def _attn_kernel(x_ref, mix_ref, wout, g_ref, wq, k_ref, v_ref, wo, o_ref):
    x1 = x_ref[0] + _mm(mix_ref[0], wout[...])
    q = _mm(_rms(x1, g_ref[...]), wq[...])
    heads = []
    for h in range(MEM_HEADS):
        sl = slice(h * MEM_HEAD_DIM, (h + 1) * MEM_HEAD_DIM)
        s = _mm_nt(q[:, sl], k_ref[0, :, sl]) * (MEM_HEAD_DIM ** -0.5)
        e = jnp.exp(s - jnp.max(s, axis=-1, keepdims=True))
        p = e / jnp.sum(e, axis=-1, keepdims=True)
        heads.append(_mm(p, v_ref[0, :, sl]))
    o_ref[0] = x1 + _mm(jnp.concatenate(heads, axis=-1), wo[...])


def _attn(x, mix, mem_k, mem_v, pk, l):
    B, T, _ = x.shape
    TB = min(T, 256)

    def per_l(arr):
        nd = arr.ndim
        return pl.BlockSpec((None,) + tuple(arr.shape[1:]), lambda b, t: (l,) + (0,) * (nd - 1))

    blk = pl.BlockSpec((1, TB, D_MODEL), lambda b, t: (b, t, 0))
    kv = pl.BlockSpec((1, MEM_TOKENS, D_MODEL), lambda b, t: (b, 0, 0))
    return pl.pallas_call(
        _attn_kernel,
        grid=(B, T // TB),
        in_specs=[blk, blk, per_l(pk["w_out"]), per_l(pk["g_cross"]), per_l(pk["w_cq"]), kv, kv, per_l(pk["w_co"])],
        out_specs=blk,
        out_shape=jax.ShapeDtypeStruct((B, T, D_MODEL), F32),
        compiler_params=pltpu.CompilerParams(dimension_semantics=("parallel", "parallel"),
                                             vmem_limit_bytes=VMEM_LIMIT),
        name="attn",
    )(x, mix, pk["w_out"], pk["g_cross"], pk["w_cq"], mem_k, mem_v, pk["w_co"])


def _ffn_kernel(final, x_ref, g_ref, wg, wu, wd, gf_ref, o_ref):
    x = x_ref[...]
    h = _rms(x, g_ref[...]).astype(MXU_DTYPE)
    y = x + _mm(_silu(_mm(h, wg[...])) * _mm(h, wu[...]), wd[...])
    if final:
        y = _rms(y, gf_ref[...])
    o_ref[...] = y


def _ffn(x2d, pk, l, final):
    R = x2d.shape[0]
    TM = min(R, 256)

    def per_l(arr):
        nd = arr.ndim
        return pl.BlockSpec((None,) + tuple(arr.shape[1:]), lambda i: (l,) + (0,) * (nd - 1))

    blk = pl.BlockSpec((TM, D_MODEL), lambda i: (i, 0))
    return pl.pallas_call(
        functools.partial(_ffn_kernel, final),
        grid=(R // TM,),
        in_specs=[blk, per_l(pk["g_ffn"]), per_l(pk["w_gate"]), per_l(pk["w_up"]), per_l(pk["w_down"]),
                  pl.BlockSpec((1, D_MODEL), lambda i: (0, 0))],
        out_specs=blk,
        out_shape=jax.ShapeDtypeStruct((R, D_MODEL), F32),
        compiler_params=pltpu.CompilerParams(dimension_semantics=("parallel",), vmem_limit_bytes=VMEM_LIMIT),
        name="ffn",
    )(x2d, pk["g_ffn"], pk["w_gate"], pk["w_up"], pk["w_down"], pk["g_final"])


def _memkv_kernel(m_ref, g_ref, wk, wv, k_ref, v_ref):
    h = _rms(m_ref[0], g_ref[...]).astype(MXU_DTYPE)
    k_ref[0] = _mm(h, wk[...])
    v_ref[0] = _mm(h, wv[...])


def _memory_kv(mem, pk, l):
    B = mem.shape[0]

    def per_l(arr):
        nd = arr.ndim
        return pl.BlockSpec((None,) + tuple(arr.shape[1:]), lambda b: (l,) + (0,) * (nd - 1))

    blk = pl.BlockSpec((1, MEM_TOKENS, D_MODEL), lambda b: (b, 0, 0))
    return pl.pallas_call(
        _memkv_kernel,
        grid=(B,),
        in_specs=[blk, per_l(pk["g_mem"]), per_l(pk["w_ck"]), per_l(pk["w_cv"])],
        out_specs=[blk, blk],
        out_shape=[jax.ShapeDtypeStruct((B, MEM_TOKENS, D_MODEL), F32)] * 2,
        compiler_params=pltpu.CompilerParams(dimension_semantics=("parallel",), vmem_limit_bytes=VMEM_LIMIT),
        name="memory_kv",
    )(mem, pk["g_mem"], pk["w_ck"], pk["w_cv"])


def _block_diag(w):
    L, n, a, b = w.shape
    return jnp.einsum('lnab,nm->lnamb', w, jnp.eye(n, dtype=w.dtype)).reshape(L, n * a, n * b)


def _pack_rows(rows, n_rows):
    L, width = rows[0].shape[0], rows[0].shape[-1]
    rows = [r.reshape(L, -1, width) for r in rows]
    used = sum(r.shape[1] for r in rows)
    return jnp.concatenate(rows + [jnp.zeros((L, n_rows - used, width), F32)], axis=1)


def _prepare(norm_mix_g, w_in, w_out, lru_conv_w, lru_conv_b, lru_w_r, lru_b_r, lru_w_i, lru_b_i, lru_lambda,
             ssd_conv_w, ssd_conv_b, ssd_dt_bias, ssd_a_log, ssd_d, ssd_norm_g,
             gdn_conv_w, gdn_dt_bias, gdn_a_log, gdn_norm_g,
             s5_a_re, s5_a_im, s5_log_dt, s5_b_re, s5_b_im, s5_c_re, s5_c_im, s5_d, s5_w_glu, s5_b_glu,
             norm_mem_g, norm_cross_g, w_cq, w_ck, w_cv, w_co,
             norm_ffn_g, w_ffn_gate, w_ffn_up, w_ffn_down, norm_final_g):
    L = DEPTH
    bf = MXU_DTYPE
    pk = {}
    pk["g_mix"] = norm_mix_g.reshape(L, 1, D_MODEL)
    pk["w_in"] = jnp.concatenate(
        [w_in[:, :, 0:1536], w_in[:, :, 1540:2564], w_in[:, :, 2572:2828], w_in[:, :, 1536:1540],
         w_in[:, :, 2564:2572], jnp.zeros((L, D_MODEL, U_COLS - IN_COLS), F32)], axis=2).astype(bf)
    pk["w_out"] = w_out.astype(bf)
    sp_lam = jax.nn.softplus(-lru_lambda)
    pk["v256"] = _pack_rows(
        [lru_conv_w, lru_conv_b, lru_b_r, lru_b_i, sp_lam, jnp.repeat(ssd_d, SSD_INNER // SSD_HEADS, axis=1),
         ssd_norm_g, jnp.tile(gdn_norm_g, (1, GDN_HEADS)), s5_d, s5_b_glu], 16)
    pk["v768"] = _pack_rows([ssd_conv_w, ssd_conv_b, gdn_conv_w], 16)
    zeros4 = jnp.zeros((L, 4), F32)
    bias = jnp.concatenate([ssd_dt_bias, zeros4, gdn_dt_bias, jnp.zeros((L, LANE - 12), F32)], axis=1)
    avec = jnp.concatenate([-jnp.exp(ssd_a_log.astype(F32)), zeros4, -jnp.exp(gdn_a_log),
                            jnp.zeros((L, LANE - 12), F32)], axis=1)
    pk["v128"] = _pack_rows([bias, avec], 8)
    pk["wri"] = jnp.concatenate([_block_diag(lru_w_r), _block_diag(lru_w_i)], axis=-1).astype(bf)
    a_re = s5_a_re.astype(F32)
    a_im = s5_a_im.astype(F32)
    step = jnp.exp(s5_log_dt.astype(F32))[:, :, None]
    mag = jnp.exp(a_re * step)
    ang = a_im * step
    lb_re, lb_im = mag * jnp.cos(ang), mag * jnp.sin(ang)
    den = a_re * a_re + a_im * a_im
    f_re = ((lb_re - 1.0) * a_re + lb_im * a_im) / den
    f_im = (lb_im * a_re - (lb_re - 1.0) * a_im) / den
    b_re = s5_b_re.astype(F32)
    b_im = s5_b_im.astype(F32)
    bb_re = f_re[..., None] * b_re - f_im[..., None] * b_im
    bb_im = f_re[..., None] * b_im + f_im[..., None] * b_re
    pk["v1024"] = _pack_rows([lb_re.reshape(L, S5_N), lb_im.reshape(L, S5_N)], 8)
    pk["bmat"] = jnp.concatenate([_block_diag(jnp.swapaxes(bb_re, 2, 3)), _block_diag(jnp.swapaxes(bb_im, 2, 3))],
                                 axis=-1).astype(bf)
    pk["cmat"] = jnp.concatenate([_block_diag(jnp.swapaxes(s5_c_re, 2, 3)), -_block_diag(jnp.swapaxes(s5_c_im, 2, 3))],
                                 axis=1).astype(bf)
    pk["wglu"] = s5_w_glu.astype(bf)
    pk["g_mem"] = norm_mem_g.reshape(L, 1, D_MODEL)
    pk["g_cross"] = norm_cross_g.reshape(L, 1, D_MODEL)
    pk["w_cq"], pk["w_ck"], pk["w_cv"], pk["w_co"] = (w.astype(bf) for w in (w_cq, w_ck, w_cv, w_co))
    pk["g_ffn"] = norm_ffn_g.reshape(L, 1, D_MODEL)
    pk["w_gate"], pk["w_up"], pk["w_down"] = (w.astype(bf) for w in (w_ffn_gate, w_ffn_up, w_ffn_down))
    pk["g_final"] = norm_final_g.reshape(1, D_MODEL)
    return pk


def _states_to_kernel(lru_h, lru_buf, ssd, ssd_buf, gdn, gdn_buf, s5_re, s5_im):
    B = lru_h.shape[0]
    eye = jnp.eye(GDN_HEADS, dtype=F32)
    gdn_bd = jnp.einsum('bhkv,hg->bhkgv', gdn, eye).reshape(B, GDN_WIDTH, GDN_WIDTH)
    return (lru_h.reshape(B, 1, LRU_WIDTH), lru_buf, ssd.reshape(B, SSD_GROUPS, LANE, SSD_STATE), ssd_buf,
            gdn_bd, gdn_buf, s5_re.reshape(B, 1, S5_N), s5_im.reshape(B, 1, S5_N))


def _states_from_kernel(st):
    lru_h, lru_buf, ssd, ssd_buf, gdn_bd, gdn_buf, s5_re, s5_im = st
    B = lru_h.shape[0]
    gdn = jnp.einsum('bhkhv->bhkv', gdn_bd.reshape(B, GDN_HEADS, GDN_HEAD_DIM, GDN_HEADS, GDN_HEAD_DIM))
    return (lru_h.reshape(B, LRU_WIDTH), lru_buf, ssd.reshape(B, SSD_HEADS, SSD_INNER // SSD_HEADS, SSD_STATE),
            ssd_buf, gdn, gdn_buf, s5_re.reshape(B, S5_GROUPS, S5_STATE), s5_im.reshape(B, S5_GROUPS, S5_STATE))


def _layer(l, x, mem_k, mem_v, states, pk, final):
    B, T, _ = x.shape
    u = _in_proj(x.reshape(B * T, D_MODEL), pk["g_mix"], pk["w_in"], l).reshape(B, T, U_COLS)
    mix, new_states = _mixer(u, states, pk, l)
    x = _attn(x, mix, mem_k, mem_v, pk, l)
    x = _ffn(x.reshape(B * T, D_MODEL), pk, l, final).reshape(B, T, D_MODEL)
    return x, new_states


def kernel(x_prompt, x_sample, mem_prompt, state_lru_h, cache_lru_conv, state_ssd, cache_ssd_conv, state_gdn, cache_gdn_conv, state_s5_re, state_s5_im, cache_mem_k, cache_mem_v, norm_mix_g, w_in, w_out, lru_conv_w, lru_conv_b, lru_w_r, lru_b_r, lru_w_i, lru_b_i, lru_lambda, ssd_conv_w, ssd_conv_b, ssd_dt_bias, ssd_a_log, ssd_d, ssd_norm_g, gdn_conv_w, gdn_dt_bias, gdn_a_log, gdn_norm_g, s5_a_re, s5_a_im, s5_log_dt, s5_b_re, s5_b_im, s5_c_re, s5_c_im, s5_d, s5_w_glu, s5_b_glu, norm_mem_g, norm_cross_g, w_cq, w_ck, w_cv, w_co, norm_ffn_g, w_ffn_gate, w_ffn_up, w_ffn_down, norm_final_g):
    pk = _prepare(norm_mix_g, w_in, w_out, lru_conv_w, lru_conv_b, lru_w_r, lru_b_r, lru_w_i, lru_b_i, lru_lambda,
                  ssd_conv_w, ssd_conv_b, ssd_dt_bias, ssd_a_log, ssd_d, ssd_norm_g,
                  gdn_conv_w, gdn_dt_bias, gdn_a_log, gdn_norm_g,
                  s5_a_re, s5_a_im, s5_log_dt, s5_b_re, s5_b_im, s5_c_re, s5_c_im, s5_d, s5_w_glu, s5_b_glu,
                  norm_mem_g, norm_cross_g, w_cq, w_ck, w_cv, w_co,
                  norm_ffn_g, w_ffn_gate, w_ffn_up, w_ffn_down, norm_final_g)
    bp = x_prompt.shape[0]
    bs = x_sample.shape[0]
    k1 = CONV_WIDTH - 1
    zero_states = _states_to_kernel(
        jnp.zeros((bp, LRU_WIDTH), F32), jnp.zeros((bp, k1, LRU_WIDTH), F32),
        jnp.zeros((bp, SSD_HEADS, SSD_INNER // SSD_HEADS, SSD_STATE), F32), jnp.zeros((bp, k1, SSD_CONV_DIM), F32),
        jnp.zeros((bp, GDN_HEADS, GDN_HEAD_DIM, GDN_HEAD_DIM), F32), jnp.zeros((bp, k1, GDN_CONV_DIM), F32),
        jnp.zeros((bp, S5_GROUPS, S5_STATE), F32), jnp.zeros((bp, S5_GROUPS, S5_STATE), F32))
    xp, xs = x_prompt, x_sample
    p_states, s_states, p_mk, p_mv = [], [], [], []
    for l in range(DEPTH):
        final = l == DEPTH - 1
        mk, mv = _memory_kv(mem_prompt, pk, l)
        xp, sp = _layer(l, xp, mk, mv, zero_states, pk, final)
        s_in = _states_to_kernel(state_lru_h[l], cache_lru_conv[l], state_ssd[l], cache_ssd_conv[l],
                                 state_gdn[l], cache_gdn_conv[l], state_s5_re[l], state_s5_im[l])
        xs, ss = _layer(l, xs, cache_mem_k[l].reshape(bs, MEM_TOKENS, D_MODEL),
                        cache_mem_v[l].reshape(bs, MEM_TOKENS, D_MODEL), s_in, pk, final)
        p_states.append(_states_from_kernel(sp))
        s_states.append(_states_from_kernel(ss))
        p_mk.append(mk.reshape(bp, MEM_TOKENS, MEM_HEADS, MEM_HEAD_DIM))
        p_mv.append(mv.reshape(bp, MEM_TOKENS, MEM_HEADS, MEM_HEAD_DIM))

    def stack(states, j):
        return jnp.stack([st[j] for st in states], axis=0)

    return ((xp, xs) + tuple(stack(p_states, j) for j in range(8)) + (jnp.stack(p_mk, axis=0), jnp.stack(p_mv, axis=0))
            + tuple(stack(s_states, j) for j in range(8)))
```

```python
import functools
import math

import jax
import jax.numpy as jnp
import numpy as np
from jax import lax
from jax.experimental import pallas as pl
from jax.experimental.pallas import tpu as pltpu

F32 = jnp.float32
MXU_DTYPE = jnp.bfloat16

D_MODEL = 1024
DEPTH = 4
CONV_WIDTH = 4
RMS_EPS = 1e-6
LRU_WIDTH = 256
LRU_BLOCKS = 4
LRU_C = 8.0
SSD_INNER = 256
SSD_HEADS = 4
SSD_GROUPS = 2
SSD_STATE = 128
SSD_CONV_DIM = 768
GDN_WIDTH = 256
GDN_HEAD_DIM = 64
GDN_HEADS = 4
GDN_CONV_DIM = 768
S5_WIDTH = 256
S5_GROUP_CH = 16
S5_GROUPS = 16
S5_STATE = 64
S5_N = S5_GROUPS * S5_STATE
MEM_TOKENS = 256
MEM_HEADS = 4
MEM_HEAD_DIM = 256
FFN_HIDDEN = 2816
IN_COLS = 2828

U_GATE, U_LRUX, U_SSDZ, U_XBC, U_QKV, U_GDNZ, U_S5, U_SMALL = 0, 256, 512, 768, 1536, 2304, 2560, 2816
U_COLS = 2944
SM_DT, SM_BETA, SM_A = 0, 4, 8

LANE = 128
SUBLANE = 8
VMEM_LIMIT = 56 * 1024 * 1024
HIST = SUBLANE
ROW_TILE = 512
MIXER_BLOCK = 256


def _mm(a, b):
    return jnp.dot(a.astype(MXU_DTYPE), b.astype(MXU_DTYPE), preferred_element_type=F32)


def _mm_nt(a, b):
    return lax.dot_general(a.astype(MXU_DTYPE), b.astype(MXU_DTYPE), (((1,), (1,)), ((), ())),
                           preferred_element_type=F32)


def _mm_tn(a, b):
    return lax.dot_general(a.astype(MXU_DTYPE), b.astype(MXU_DTYPE), (((0,), (0,)), ((), ())),
                           preferred_element_type=F32)


def _rms(x, g):
    ms = jnp.mean(x * x, axis=-1, keepdims=True)
    return x * lax.rsqrt(ms + RMS_EPS) * g


def _silu(x):
    return x * jax.nn.sigmoid(x)


def _gelu(x):
    return jax.nn.gelu(x, approximate=True)


def _shift_rows(x, k, row, fill):
    return jnp.where(row >= k, pltpu.roll(x, k, 0), fill)


def _in_proj_kernel(x_ref, g_ref, w_ref, u_ref):
    h = _rms(x_ref[...], g_ref[...])
    u_ref[...] = _mm(h, w_ref[...])


def _in_proj(x2d, g_all, w_all, l):
    R = x2d.shape[0]
    TM = min(R, MIXER_BLOCK)
    return pl.pallas_call(
        _in_proj_kernel,
        grid=(R // TM,),
        in_specs=[pl.BlockSpec((TM, D_MODEL), lambda i: (i, 0)),
                  pl.BlockSpec((None, 1, D_MODEL), lambda i: (l, 0, 0)),
                  pl.BlockSpec((None, D_MODEL, U_COLS), lambda i: (l, 0, 0))],
        out_specs=pl.BlockSpec((TM, U_COLS), lambda i: (i, 0)),
        out_shape=jax.ShapeDtypeStruct((R, U_COLS), F32),
        compiler_params=pltpu.CompilerParams(dimension_semantics=("parallel",), vmem_limit_bytes=VMEM_LIMIT),
        name="in_proj",
    )(x2d, g_all, w_all)


def _conv_block(xp_ref, x, w_ref, w_row0, TB):
    xp_ref[pl.ds(HIST, TB), :] = x
    y = x * w_ref[w_row0 + CONV_WIDTH - 1:w_row0 + CONV_WIDTH, :]
    for j in range(CONV_WIDTH - 1):
        off = HIST - (CONV_WIDTH - 1) + j
        y = y + xp_ref[pl.ds(off, TB), :] * w_ref[w_row0 + j:w_row0 + j + 1, :]
    return y


def _interleave(streams):
    streams = list(streams)
    while streams:
        for s in list(streams):
            try:
                next(s)
            except StopIteration:
                streams.remove(s)


def _mixer_kernel(TB, QG, NT, fused, *refs):
    if fused:
        x_ref, g_ref, win_ref = refs[:3]
        hb = _rms(x_ref[0], g_ref[...]).astype(MXU_DTYPE)
    else:
        u_ref = refs[0]

    def useg(start, width):
        if fused:
            return _mm(hb, win_ref[:, start:start + width])
        return u_ref[0, :, start:start + width]

    _mixer_body(TB, QG, NT, useg, *refs[3 if fused else 1:])


def _mixer_body(TB, QG, NT, useg,
                  lru_h0, lru_b0, ssd_s0, ssd_b0, gdn_s0, gdn_b0, s5r0, s5i0,
                  v256, v768, v128, v1024, wri, bmat, cmat, wglu,
                  mix_ref, lru_h_o, lru_b_o, ssd_s_o, ssd_b_o, gdn_s_o, gdn_b_o, s5r_o, s5i_o,
                  xp_lru, xp_ssd, xp_gdn, h_lru, s_ssd, s_gdn, x_s5, hs_scr, x5_scr):
    t = pl.program_id(1)
    K1 = CONV_WIDTH - 1
    NG = TB // SUBLANE

    @pl.when(t == 0)
    def _init():
        for xp, b0 in ((xp_lru, lru_b0), (xp_ssd, ssd_b0), (xp_gdn, gdn_b0)):
            xp[pl.ds(0, HIST), :] = jnp.zeros((HIST, xp.shape[1]), F32)
            xp[pl.ds(HIST - K1, K1), :] = b0[0]
        h_lru[...] = jnp.zeros(h_lru.shape, F32)
        h_lru[0:1, :] = lru_h0[0]
        s_ssd[...] = ssd_s0[0]
        s_gdn[...] = gdn_s0[0]
        x_s5[...] = jnp.zeros(x_s5.shape, F32)
        x_s5[0, SUBLANE - 1:SUBLANE, :] = s5r0[0]
        x_s5[1, SUBLANE - 1:SUBLANE, :] = s5i0[0]

    row128 = lax.broadcasted_iota(jnp.int32, (TB, LANE), 0)
    lane128 = lax.broadcasted_iota(jnp.int32, (TB, LANE), 1)
    small = useg(U_SMALL, LANE)
    sp = jax.nn.softplus(small + v128[0:1, :])
    sig = jax.nn.sigmoid(small)
    gcum = sp * v128[1:2, :]
    rk = jnp.where(lane128 < SM_BETA, row128, row128 & (QG - 1))
    k = 1
    while k < TB:
        gcum = gcum + jnp.where(rk >= k, pltpu.roll(gcum, k, 0), 0.0)
        k *= 2
    if TB < LANE:
        gcum_t = jnp.concatenate([gcum, jnp.zeros((LANE - TB, LANE), F32)], axis=0).T
    else:
        gcum_t = gcum.T
    rr = lax.broadcasted_iota(jnp.int32, (TB, TB), 0)
    cc = lax.broadcasted_iota(jnp.int32, (TB, TB), 1)

    def lru():
        u_lru = useg(U_GATE, 2 * LRU_WIDTH)
        gate = u_lru[:, 0:LRU_WIDTH]
        xc = _conv_block(xp_lru, u_lru[:, LRU_WIDTH:2 * LRU_WIDTH], v256, 0, TB) + v256[4:5, :]
        ri = _mm(xc, wri[...])
        yield
        r = jax.nn.sigmoid(ri[:, :LRU_WIDTH] + v256[5:6, :])
        ig = jax.nn.sigmoid(ri[:, LRU_WIDTH:] + v256[6:7, :])
        log_a = (-LRU_C) * r * v256[7:8, :]
        a = jnp.exp(log_a)
        bt = jnp.sqrt(-jnp.tanh(log_a) * (a * a + 1.0)) * (ig * xc)
        yield
        row8 = lax.broadcasted_iota(jnp.int32, (SUBLANE, LRU_WIDTH), 0)
        h_prev = h_lru[0:1, :]
        for i in range(NG):
            a_g = a[i * SUBLANE:(i + 1) * SUBLANE]
            b_g = bt[i * SUBLANE:(i + 1) * SUBLANE]
            k = 1
            while k < SUBLANE:
                b_g = b_g + a_g * _shift_rows(b_g, k, row8, 0.0)
                a_g = a_g * _shift_rows(a_g, k, row8, 1.0)
                k *= 2
            h_g = b_g + a_g * h_prev
            h_prev = h_g[SUBLANE - 1:SUBLANE, :]
            hs_scr[pl.ds(i * SUBLANE, SUBLANE), :] = h_g
            if i % 4 == 3:
                yield
        h_lru[0:1, :] = h_prev
        mix_ref[0, :, 0:LRU_WIDTH] = hs_scr[...] * _gelu(gate)

    def ssd():
        u_ssd = useg(U_SSDZ, SSD_INNER + SSD_CONV_DIM)
        z = u_ssd[:, 0:SSD_INNER]
        xbc = _silu(_conv_block(xp_ssd, u_ssd[:, SSD_INNER:SSD_INNER + SSD_CONV_DIM], v768, 0, TB) + v768[4:5, :])
        yield
        causal = rr >= cc
        lo = lane128 < (LANE // 2)
        row_s = lax.broadcasted_iota(jnp.int32, (LANE, SSD_STATE), 0) < (LANE // 2)
        y_pairs = []
        for p in range(SSD_GROUPS):
            xs_p = xbc[:, p * LANE:(p + 1) * LANE]
            b_p = xbc[:, SSD_INNER + p * SSD_STATE:SSD_INNER + (p + 1) * SSD_STATE]
            c_p = xbc[:, SSD_INNER + SSD_GROUPS * SSD_STATE + p * SSD_STATE:
                      SSD_INNER + SSD_GROUPS * SSD_STATE + (p + 1) * SSD_STATE]
            cb = _mm_nt(c_p, b_p)
            h0, h1 = 2 * p, 2 * p + 1
            xdt = xs_p * jnp.where(lo, sp[:, SM_DT + h0:SM_DT + h0 + 1], sp[:, SM_DT + h1:SM_DT + h1 + 1])
            s_pair = s_ssd[p]
            ys = []
            for h in (h0, h1):
                gcol = gcum[:, SM_DT + h:SM_DT + h + 1]
                grow = gcum_t[SM_DT + h:SM_DT + h + 1, 0:TB]
                seg = jnp.exp(jnp.where(causal, gcol - grow, -jnp.inf))
                y_diag = _mm(cb * seg, xdt)
                y_off = _mm_nt(c_p * jnp.exp(gcol), s_pair)
                ys.append(y_diag + y_off)
                yield
            y_pairs.append(jnp.where(lo, ys[0], ys[1]))
            g0 = gcum[:, SM_DT + h0:SM_DT + h0 + 1]
            g1 = gcum[:, SM_DT + h1:SM_DT + h1 + 1]
            gl0 = g0[TB - 1:TB, :]
            gl1 = g1[TB - 1:TB, :]
            to_end = jnp.where(lo, jnp.exp(gl0 - g0), jnp.exp(gl1 - g1))
            s_ssd[p] = jnp.where(row_s, jnp.exp(gl0), jnp.exp(gl1)) * s_pair + _mm_tn(xdt * to_end, b_p)
            yield
        y = jnp.concatenate(y_pairs, axis=-1) + v256[8:9, :] * xbc[:, 0:SSD_INNER]
        yg = y * _silu(z)
        gs = SSD_INNER // SSD_GROUPS
        for p in range(SSD_GROUPS):
            mix_ref[0, :, LRU_WIDTH + p * gs:LRU_WIDTH + (p + 1) * gs] = _rms(
                yg[:, p * gs:(p + 1) * gs], v256[9:10, p * gs:(p + 1) * gs])

    def gdn():
        lane = lax.broadcasted_iota(jnp.int32, (TB, GDN_WIDTH), 1)
        hmask = [(lane >= h * GDN_HEAD_DIM) & (lane < (h + 1) * GDN_HEAD_DIM) for h in range(GDN_HEADS)]

        def by_head(vals):
            return jnp.where(hmask[0], vals[0], jnp.where(hmask[1], vals[1], jnp.where(hmask[2], vals[2], vals[3])))

        def head_sum(x):
            return by_head([jnp.sum(jnp.where(m, x, 0.0), axis=-1, keepdims=True) for m in hmask])

        u_gdn = useg(U_QKV, GDN_CONV_DIM + GDN_WIDTH)
        qkv = _silu(_conv_block(xp_gdn, u_gdn[:, 0:GDN_CONV_DIM], v768, 5, TB))
        qf = qkv[:, 0:GDN_WIDTH]
        kf = qkv[:, GDN_WIDTH:2 * GDN_WIDTH]
        vf = qkv[:, 2 * GDN_WIDTH:3 * GDN_WIDTH]
        qf = qf * lax.rsqrt(head_sum(qf * qf) + 1e-6) * (GDN_HEAD_DIM ** -0.5)
        kf = kf * lax.rsqrt(head_sum(kf * kf) + 1e-6)
        beta_f = by_head([sig[:, SM_BETA + h:SM_BETA + h + 1] for h in range(GDN_HEADS)])
        gc_f = by_head([gcum[:, SM_A + h:SM_A + h + 1] for h in range(GDN_HEADS)])
        chunk_shift = int(math.log2(QG))
        same_chunk = lax.shift_right_logical(rr, chunk_shift) == lax.shift_right_logical(cc, chunk_shift)
        incl = same_chunk & (rr >= cc)
        strict = same_chunk & (rr > cc)
        eg = jnp.exp(gc_f)
        kq = _mm_nt(jnp.concatenate([jnp.where(m, kf, 0.0) for m in hmask] + [jnp.where(m, qf, 0.0) for m in hmask],
                                    axis=0), kf)
        rhs = jnp.concatenate([beta_f * vf, beta_f * eg * kf], axis=1)
        yield
        decs, pws, rs = [], [], []
        for h in range(GDN_HEADS):
            gcol = gcum[:, SM_A + h:SM_A + h + 1]
            grow = gcum_t[SM_A + h:SM_A + h + 1, 0:TB]
            dec = jnp.exp(jnp.where(incl, gcol - grow, -jnp.inf))
            low = jnp.where(strict, sig[:, SM_BETA + h:SM_BETA + h + 1] * kq[h * TB:(h + 1) * TB] * dec, 0.0)
            decs.append(dec)
            pws.append(-low)
            rs.append(-low)
        yield
        for _ in range(chunk_shift - 1):
            for h in range(GDN_HEADS):
                pws[h] = _mm(pws[h], pws[h])
                rs[h] = rs[h] + pws[h] + _mm(rs[h], pws[h])
                yield
        sols, qks = [], []
        for h in range(GDN_HEADS):
            sols.append(rhs + _mm(rs[h], rhs))
            qks.append(kq[(GDN_HEADS + h) * TB:(GDN_HEADS + h + 1) * TB] * decs[h])
            yield
        u_all = by_head([s[:, 0:GDN_WIDTH] for s in sols])
        w_all = by_head([s[:, GDN_WIDTH:2 * GDN_WIDTH] for s in sols])
        q_dec = qf * eg
        head_shift = int(math.log2(GDN_HEAD_DIM))
        r2 = lax.shift_right_logical(lax.broadcasted_iota(jnp.int32, (GDN_WIDTH, GDN_WIDTH), 0), head_shift)
        c2 = lax.shift_right_logical(lax.broadcasted_iota(jnp.int32, (GDN_WIDTH, GDN_WIDTH), 1), head_shift)
        blockdiag = r2 == c2
        s_bd = s_gdn[...]
        deltas, o_state = [], []
        for c in range(TB // QG):
            sl = slice(c * QG, (c + 1) * QG)
            ws = _mm(jnp.concatenate([w_all[sl], q_dec[sl]], axis=0), s_bd)
            delta = u_all[sl] - ws[0:QG]
            o_state.append(ws[QG:2 * QG])
            yield
            g_c = gc_f[sl]
            g_last = g_c[QG - 1:QG, :]
            s_bd = s_bd * jnp.exp(g_last) + jnp.where(blockdiag, _mm_tn(kf[sl] * jnp.exp(g_last - g_c), delta), 0.0)
            deltas.append(delta)
            yield
        s_gdn[...] = s_bd
        delta_all = jnp.concatenate(deltas, axis=0)
        o = jnp.concatenate(o_state, axis=0) + by_head([_mm(qks[h], delta_all) for h in range(GDN_HEADS)])
        yield
        ms = head_sum(o * o) * (1.0 / GDN_HEAD_DIM)
        o = o * lax.rsqrt(ms + RMS_EPS) * v256[10:11, :]
        zg = u_gdn[:, GDN_CONV_DIM:GDN_CONV_DIM + GDN_WIDTH]
        mix_ref[0, :, 2 * LRU_WIDTH:2 * LRU_WIDTH + GDN_WIDTH] = o * _silu(zg)

    def s5():
        row_n = lax.broadcasted_iota(jnp.int32, (SUBLANE, S5_N), 0)
        us5 = useg(U_S5, S5_WIDTH)
        bu = _mm(us5, bmat[...])
        yield
        lam_r, lam_i = v1024[0:1, :], v1024[1:2, :]
        pows = []
        pr, pi = lam_r, lam_i
        k = 1
        while k < SUBLANE:
            pows.append((k, jnp.where(row_n >= k, pr, 0.0), jnp.where(row_n >= k, pi, 0.0)))
            pr, pi = pr * pr - pi * pi, 2.0 * (pr * pi)
            k *= 2
        in_r = jnp.where(row_n == 0, lam_r, 0.0)
        in_i = jnp.where(row_n == 0, lam_i, 0.0)
        g_r = x_s5[0]
        g_i = x_s5[1]
        for i in range(NG):
            p_r = pltpu.roll(g_r, 1, 0)
            p_i = pltpu.roll(g_i, 1, 0)
            g_r = bu[i * SUBLANE:(i + 1) * SUBLANE, 0:S5_N] + (in_r * p_r - in_i * p_i)
            g_i = bu[i * SUBLANE:(i + 1) * SUBLANE, S5_N:2 * S5_N] + (in_r * p_i + in_i * p_r)
            for k, pr, pi in pows:
                s_r = pltpu.roll(g_r, k, 0)
                s_i = pltpu.roll(g_i, k, 0)
                g_r, g_i = g_r + (pr * s_r - pi * s_i), g_i + (pr * s_i + pi * s_r)
            x5_scr[pl.ds(i * SUBLANE, SUBLANE), 0:S5_N] = g_r
            x5_scr[pl.ds(i * SUBLANE, SUBLANE), S5_N:2 * S5_N] = g_i
            yield
        x_s5[0] = g_r
        x_s5[1] = g_i
        y5 = _mm(x5_scr[...], cmat[...])
        y5 = _gelu(y5 + v256[11:12, :] * us5)
        yield
        mix_ref[0, :, 3 * LRU_WIDTH:3 * LRU_WIDTH + S5_WIDTH] = y5 * jax.nn.sigmoid(_mm(y5, wglu[...]) + v256[12:13, :])

    _interleave([gdn(), s5(), ssd(), lru()])

    for xp in (xp_lru, xp_ssd, xp_gdn):
        xp[pl.ds(0, HIST), :] = xp[pl.ds(TB, HIST), :]

    @pl.when(t == NT - 1)
    def _emit():
        lru_h_o[0] = h_lru[0:1, :]
        lru_b_o[0] = xp_lru[pl.ds(HIST - K1, K1), :]
        ssd_b_o[0] = xp_ssd[pl.ds(HIST - K1, K1), :]
        gdn_b_o[0] = xp_gdn[pl.ds(HIST - K1, K1), :]
        ssd_s_o[0] = s_ssd[...]
        gdn_s_o[0] = s_gdn[...]
        s5r_o[0] = x_s5[0, SUBLANE - 1:SUBLANE, :]
        s5i_o[0] = x_s5[1, SUBLANE - 1:SUBLANE, :]


def _mixer(src, states, pk, l, fused):
    B, T, _ = src.shape
    TB = min(T, MIXER_BLOCK)
    QG = min(TB, 64)
    NT = T // TB
    lru_h0, lru_b0, ssd_s0, ssd_b0, gdn_s0, gdn_b0, s5r0, s5i0 = states

    def per_b(shape):
        nd = len(shape)
        return pl.BlockSpec((1,) + tuple(shape[1:]), lambda b, t: (b,) + (0,) * (nd - 1))

    def per_l(arr):
        nd = arr.ndim
        return pl.BlockSpec((None,) + tuple(arr.shape[1:]), lambda b, t: (l,) + (0,) * (nd - 1))

    proj = (pk["g_mix"], pk["w_in"]) if fused else ()
    params = (pk["v256"], pk["v768"], pk["v128"], pk["v1024"], pk["wri"], pk["bmat"], pk["cmat"], pk["wglu"])
    state_shapes = [s.shape for s in states]
    out_shape = [jax.ShapeDtypeStruct((B, T, D_MODEL), F32)] + [jax.ShapeDtypeStruct(s, F32) for s in state_shapes]
    outs = pl.pallas_call(
        functools.partial(_mixer_kernel, TB, QG, NT, fused),
        grid=(B, NT),
        in_specs=[pl.BlockSpec((1, TB, src.shape[2]), lambda b, t: (b, t, 0))] + [per_l(p) for p in proj]
                 + [per_b(s) for s in state_shapes] + [per_l(p) for p in params],
        out_specs=[pl.BlockSpec((1, TB, D_MODEL), lambda b, t: (b, t, 0))] + [per_b(s) for s in state_shapes],
        out_shape=out_shape,
        scratch_shapes=[pltpu.VMEM((TB + HIST, LRU_WIDTH), F32),
                        pltpu.VMEM((TB + HIST, SSD_CONV_DIM), F32),
                        pltpu.VMEM((TB + HIST, GDN_CONV_DIM), F32),
                        pltpu.VMEM((SUBLANE, LRU_WIDTH), F32),
                        pltpu.VMEM((SSD_GROUPS, LANE, SSD_STATE), F32),
                        pltpu.VMEM((GDN_WIDTH, GDN_WIDTH), F32),
                        pltpu.VMEM((2, SUBLANE, S5_N), F32),
                        pltpu.VMEM((TB, LRU_WIDTH), F32),
                        pltpu.VMEM((TB, 2 * S5_N), F32)],
        compiler_params=pltpu.CompilerParams(dimension_semantics=("arbitrary", "arbitrary"),
                                             vmem_limit_bytes=VMEM_LIMIT),
        name="mixer",
    )(src, *proj, *states, *params)
    return outs[0], tuple(outs[1:])


def _attn_kernel(x_ref, mix_ref, wout, g_ref, wq, k_ref, v_ref, wo, o_ref):
    x1 = x_ref[0] + _mm(mix_ref[0], wout[...])
    q = _mm(_rms(x1, g_ref[...]), wq[...])
    heads = []
    for h in range(MEM_HEADS):
        sl = slice(h * MEM_HEAD_DIM, (h + 1) * MEM_HEAD_DIM)
        s = _mm_nt(q[:, sl], k_ref[0, :, sl]) * (MEM_HEAD_DIM ** -0.5)
        e = jnp.exp(s - jnp.max(s, axis=-1, keepdims=True))
        p = e / jnp.sum(e, axis=-1, keepdims=True)
        heads.append(_mm(p, v_ref[0, :, sl]))
    o_ref[0] = x1 + _mm(jnp.concatenate(heads, axis=-1), wo[...])


def _attn(x, mix, mem_k, mem_v, lkv, pk, l):
    B, T, _ = x.shape
    TB = min(T, ROW_TILE)

    def per_l(arr):
        nd = arr.ndim
        return pl.BlockSpec((None,) + tuple(arr.shape[1:]), lambda b, t: (l,) + (0,) * (nd - 1),
                            pipeline_mode=pl.Buffered(1))

    blk = pl.BlockSpec((1, TB, D_MODEL), lambda b, t: (b, t, 0))
    kv = pl.BlockSpec((None, 1, MEM_TOKENS, D_MODEL), lambda b, t: (lkv, b, 0, 0))
    return pl.pallas_call(
        _attn_kernel,
        grid=(B, T // TB),
        in_specs=[blk, blk, per_l(pk["w_out"]), per_l(pk["g_cross"]), per_l(pk["w_cq"]), kv, kv, per_l(pk["w_co"])],
        out_specs=blk,
        out_shape=jax.ShapeDtypeStruct((B, T, D_MODEL), F32),
        compiler_params=pltpu.CompilerParams(dimension_semantics=("parallel", "parallel"),
                                             vmem_limit_bytes=VMEM_LIMIT),
        name="attn",
    )(x, mix, pk["w_out"], pk["g_cross"], pk["w_cq"], mem_k, mem_v, pk["w_co"])


def _ffn_kernel(final, x_ref, g_ref, wg, wu, wd, gf_ref, o_ref):
    x = x_ref[...]
    h = _rms(x, g_ref[...]).astype(MXU_DTYPE)
    y = x + _mm(_silu(_mm(h, wg[...])) * _mm(h, wu[...]), wd[...])
    if final:
        y = _rms(y, gf_ref[...])
    o_ref[...] = y


def _ffn(x2d, pk, l, final):
    R = x2d.shape[0]
    TM = min(R, ROW_TILE)

    def per_l(arr):
        nd = arr.ndim
        return pl.BlockSpec((None,) + tuple(arr.shape[1:]), lambda i: (l,) + (0,) * (nd - 1),
                            pipeline_mode=pl.Buffered(1))

    blk = pl.BlockSpec((TM, D_MODEL), lambda i: (i, 0))
    return pl.pallas_call(
        functools.partial(_ffn_kernel, final),
        grid=(R // TM,),
        in_specs=[blk, per_l(pk["g_ffn"]), per_l(pk["w_gate"]), per_l(pk["w_up"]), per_l(pk["w_down"]),
                  pl.BlockSpec((1, D_MODEL), lambda i: (0, 0))],
        out_specs=blk,
        out_shape=jax.ShapeDtypeStruct((R, D_MODEL), F32),
        compiler_params=pltpu.CompilerParams(dimension_semantics=("parallel",), vmem_limit_bytes=VMEM_LIMIT),
        name="ffn",
    )(x2d, pk["g_ffn"], pk["w_gate"], pk["w_up"], pk["w_down"], pk["g_final"])


def _memkv_kernel(m_ref, g_ref, wk, wv, k_ref, v_ref):
    h = _rms(m_ref[0], g_ref[...]).astype(MXU_DTYPE)
    k_ref[0] = _mm(h, wk[...])
    v_ref[0] = _mm(h, wv[...])


def _memory_kv(mem, pk, l):
    B = mem.shape[0]

    def per_l(arr):
        nd = arr.ndim
        return pl.BlockSpec((None,) + tuple(arr.shape[1:]), lambda b: (l,) + (0,) * (nd - 1))

    blk = pl.BlockSpec((1, MEM_TOKENS, D_MODEL), lambda b: (b, 0, 0))
    return pl.pallas_call(
        _memkv_kernel,
        grid=(B,),
        in_specs=[blk, per_l(pk["g_mem"]), per_l(pk["w_ck"]), per_l(pk["w_cv"])],
        out_specs=[blk, blk],
        out_shape=[jax.ShapeDtypeStruct((B, MEM_TOKENS, D_MODEL), F32)] * 2,
        compiler_params=pltpu.CompilerParams(dimension_semantics=("parallel",), vmem_limit_bytes=VMEM_LIMIT),
        name="memory_kv",
    )(mem, pk["g_mem"], pk["w_ck"], pk["w_cv"])


def _block_diag(w):
    L, n, a, b = w.shape
    return jnp.einsum('lnab,nm->lnamb', w, jnp.eye(n, dtype=w.dtype)).reshape(L, n * a, n * b)


def _pack_rows(rows, n_rows):
    L, width = rows[0].shape[0], rows[0].shape[-1]
    rows = [r.reshape(L, -1, width) for r in rows]
    used = sum(r.shape[1] for r in rows)
    return jnp.concatenate(rows + [jnp.zeros((L, n_rows - used, width), F32)], axis=1)


def _prepare(norm_mix_g, w_in, w_out, lru_conv_w, lru_conv_b, lru_w_r, lru_b_r, lru_w_i, lru_b_i, lru_lambda,
             ssd_conv_w, ssd_conv_b, ssd_dt_bias, ssd_a_log, ssd_d, ssd_norm_g,
             gdn_conv_w, gdn_dt_bias, gdn_a_log, gdn_norm_g,
             s5_a_re, s5_a_im, s5_log_dt, s5_b_re, s5_b_im, s5_c_re, s5_c_im, s5_d, s5_w_glu, s5_b_glu,
             norm_mem_g, norm_cross_g, w_cq, w_ck, w_cv, w_co,
             norm_ffn_g, w_ffn_gate, w_ffn_up, w_ffn_down, norm_final_g):
    L = DEPTH
    bf = MXU_DTYPE
    pk = {}
    pk["g_mix"] = norm_mix_g.reshape(L, 1, D_MODEL)
    pk["w_in"] = jnp.concatenate(
        [w_in[:, :, 0:1536], w_in[:, :, 1540:2564], w_in[:, :, 2572:2828], w_in[:, :, 1536:1540],
         w_in[:, :, 2564:2572], jnp.zeros((L, D_MODEL, U_COLS - IN_COLS), F32)], axis=2).astype(bf)
    pk["w_out"] = w_out.astype(bf)
    sp_lam = jax.nn.softplus(-lru_lambda)
    pk["v256"] = _pack_rows(
        [lru_conv_w, lru_conv_b, lru_b_r, lru_b_i, sp_lam, jnp.repeat(ssd_d, SSD_INNER // SSD_HEADS, axis=1),
         ssd_norm_g, jnp.tile(gdn_norm_g, (1, GDN_HEADS)), s5_d, s5_b_glu], 16)
    pk["v768"] = _pack_rows([ssd_conv_w, ssd_conv_b, gdn_conv_w], 16)
    zeros4 = jnp.zeros((L, 4), F32)
    bias = jnp.concatenate([ssd_dt_bias, zeros4, gdn_dt_bias, jnp.zeros((L, LANE - 12), F32)], axis=1)
    avec = jnp.concatenate([-jnp.exp(ssd_a_log.astype(F32)), zeros4, -jnp.exp(gdn_a_log),
                            jnp.zeros((L, LANE - 12), F32)], axis=1)
    pk["v128"] = _pack_rows([bias, avec], 8)
    pk["wri"] = jnp.concatenate([_block_diag(lru_w_r), _block_diag(lru_w_i)], axis=-1).astype(bf)
    a_re = s5_a_re.astype(F32)
    a_im = s5_a_im.astype(F32)
    step = jnp.exp(s5_log_dt.astype(F32))[:, :, None]
    mag = jnp.exp(a_re * step)
    ang = a_im * step
    lb_re, lb_im = mag * jnp.cos(ang), mag * jnp.sin(ang)
    den = a_re * a_re + a_im * a_im
    f_re = ((lb_re - 1.0) * a_re + lb_im * a_im) / den
    f_im = (lb_im * a_re - (lb_re - 1.0) * a_im) / den
    b_re = s5_b_re.astype(F32)
    b_im = s5_b_im.astype(F32)
    bb_re = f_re[..., None] * b_re - f_im[..., None] * b_im
    bb_im = f_re[..., None] * b_im + f_im[..., None] * b_re
    pk["v1024"] = _pack_rows([lb_re.reshape(L, S5_N), lb_im.reshape(L, S5_N)], 8)
    pk["bmat"] = jnp.concatenate([_block_diag(jnp.swapaxes(bb_re, 2, 3)), _block_diag(jnp.swapaxes(bb_im, 2, 3))],
                                 axis=-1).astype(bf)
    pk["cmat"] = jnp.concatenate([_block_diag(jnp.swapaxes(s5_c_re, 2, 3)), -_block_diag(jnp.swapaxes(s5_c_im, 2, 3))],
                                 axis=1).astype(bf)
    pk["wglu"] = s5_w_glu.astype(bf)
    pk["g_mem"] = norm_mem_g.reshape(L, 1, D_MODEL)
    pk["g_cross"] = norm_cross_g.reshape(L, 1, D_MODEL)
    pk["w_cq"], pk["w_ck"], pk["w_cv"], pk["w_co"] = (w.astype(bf) for w in (w_cq, w_ck, w_cv, w_co))
    pk["g_ffn"] = norm_ffn_g.reshape(L, 1, D_MODEL)
    pk["w_gate"], pk["w_up"], pk["w_down"] = (w.astype(bf) for w in (w_ffn_gate, w_ffn_up, w_ffn_down))
    pk["g_final"] = norm_final_g.reshape(1, D_MODEL)
    return pk


def _states_to_kernel(lru_h, lru_buf, ssd, ssd_buf, gdn, gdn_buf, s5_re, s5_im):
    B = lru_h.shape[0]
    eye = jnp.eye(GDN_HEADS, dtype=F32)
    gdn_bd = jnp.einsum('bhkv,hg->bhkgv', gdn, eye).reshape(B, GDN_WIDTH, GDN_WIDTH)
    return (lru_h.reshape(B, 1, LRU_WIDTH), lru_buf, ssd.reshape(B, SSD_GROUPS, LANE, SSD_STATE), ssd_buf,
            gdn_bd, gdn_buf, s5_re.reshape(B, 1, S5_N), s5_im.reshape(B, 1, S5_N))


def _states_from_kernel(st):
    lru_h, lru_buf, ssd, ssd_buf, gdn_bd, gdn_buf, s5_re, s5_im = st
    B = lru_h.shape[0]
    gdn = jnp.einsum('bhkhv->bhkv', gdn_bd.reshape(B, GDN_HEADS, GDN_HEAD_DIM, GDN_HEADS, GDN_HEAD_DIM))
    return (lru_h.reshape(B, LRU_WIDTH), lru_buf, ssd.reshape(B, SSD_HEADS, SSD_INNER // SSD_HEADS, SSD_STATE),
            ssd_buf, gdn, gdn_buf, s5_re.reshape(B, S5_GROUPS, S5_STATE), s5_im.reshape(B, S5_GROUPS, S5_STATE))


def _layer(l, x, mem_k, mem_v, lkv, states, pk, final):
    B, T, _ = x.shape
    if T >= MIXER_BLOCK:
        mix, new_states = _mixer(x, states, pk, l, True)
    else:
        u = _in_proj(x.reshape(B * T, D_MODEL), pk["g_mix"], pk["w_in"], l).reshape(B, T, U_COLS)
        mix, new_states = _mixer(u, states, pk, l, False)
    x = _attn(x, mix, mem_k, mem_v, lkv, pk, l)
    x = _ffn(x.reshape(B * T, D_MODEL), pk, l, final).reshape(B, T, D_MODEL)
    return x, new_states


def kernel(x_prompt, x_sample, mem_prompt, state_lru_h, cache_lru_conv, state_ssd, cache_ssd_conv, state_gdn, cache_gdn_conv, state_s5_re, state_s5_im, cache_mem_k, cache_mem_v, norm_mix_g, w_in, w_out, lru_conv_w, lru_conv_b, lru_w_r, lru_b_r, lru_w_i, lru_b_i, lru_lambda, ssd_conv_w, ssd_conv_b, ssd_dt_bias, ssd_a_log, ssd_d, ssd_norm_g, gdn_conv_w, gdn_dt_bias, gdn_a_log, gdn_norm_g, s5_a_re, s5_a_im, s5_log_dt, s5_b_re, s5_b_im, s5_c_re, s5_c_im, s5_d, s5_w_glu, s5_b_glu, norm_mem_g, norm_cross_g, w_cq, w_ck, w_cv, w_co, norm_ffn_g, w_ffn_gate, w_ffn_up, w_ffn_down, norm_final_g):
    pk = _prepare(norm_mix_g, w_in, w_out, lru_conv_w, lru_conv_b, lru_w_r, lru_b_r, lru_w_i, lru_b_i, lru_lambda,
                  ssd_conv_w, ssd_conv_b, ssd_dt_bias, ssd_a_log, ssd_d, ssd_norm_g,
                  gdn_conv_w, gdn_dt_bias, gdn_a_log, gdn_norm_g,
                  s5_a_re, s5_a_im, s5_log_dt, s5_b_re, s5_b_im, s5_c_re, s5_c_im, s5_d, s5_w_glu, s5_b_glu,
                  norm_mem_g, norm_cross_g, w_cq, w_ck, w_cv, w_co,
                  norm_ffn_g, w_ffn_gate, w_ffn_up, w_ffn_down, norm_final_g)
    bp = x_prompt.shape[0]
    bs = x_sample.shape[0]
    k1 = CONV_WIDTH - 1
    zero_states = _states_to_kernel(
        jnp.zeros((bp, LRU_WIDTH), F32), jnp.zeros((bp, k1, LRU_WIDTH), F32),
        jnp.zeros((bp, SSD_HEADS, SSD_INNER // SSD_HEADS, SSD_STATE), F32), jnp.zeros((bp, k1, SSD_CONV_DIM), F32),
        jnp.zeros((bp, GDN_HEADS, GDN_HEAD_DIM, GDN_HEAD_DIM), F32), jnp.zeros((bp, k1, GDN_CONV_DIM), F32),
        jnp.zeros((bp, S5_GROUPS, S5_STATE), F32), jnp.zeros((bp, S5_GROUPS, S5_STATE), F32))
    cache_k = cache_mem_k.reshape(DEPTH, bs, MEM_TOKENS, D_MODEL)
    cache_v = cache_mem_v.reshape(DEPTH, bs, MEM_TOKENS, D_MODEL)
    xp, xs = x_prompt, x_sample
    p_states, s_states, p_mk, p_mv = [], [], [], []
    for l in range(DEPTH):
        final = l == DEPTH - 1
        mk, mv = _memory_kv(mem_prompt, pk, l)
        xp, sp = _layer(l, xp, mk[None], mv[None], 0, zero_states, pk, final)
        s_in = _states_to_kernel(state_lru_h[l], cache_lru_conv[l], state_ssd[l], cache_ssd_conv[l],
                                 state_gdn[l], cache_gdn_conv[l], state_s5_re[l], state_s5_im[l])
        xs, ss = _layer(l, xs, cache_k, cache_v, l, s_in, pk, final)
        p_states.append(_states_from_kernel(sp))
        s_states.append(_states_from_kernel(ss))
        p_mk.append(mk.reshape(bp, MEM_TOKENS, MEM_HEADS, MEM_HEAD_DIM))
        p_mv.append(mv.reshape(bp, MEM_TOKENS, MEM_HEADS, MEM_HEAD_DIM))

    def stack(states, j):
        return jnp.stack([st[j] for st in states], axis=0)

    return ((xp, xs) + tuple(stack(p_states, j) for j in range(8)) + (jnp.stack(p_mk, axis=0), jnp.stack(p_mv, axis=0))
            + tuple(stack(s_states, j) for j in range(8)))
```

```python
import functools
import math

import jax
import jax.numpy as jnp
import numpy as np
from jax import lax
from jax.experimental import pallas as pl
from jax.experimental.pallas import tpu as pltpu

F32 = jnp.float32
MXU_DTYPE = jnp.bfloat16

D_MODEL = 1024
DEPTH = 4
CONV_WIDTH = 4
RMS_EPS = 1e-6
LRU_WIDTH = 256
LRU_BLOCKS = 4
LRU_C = 8.0
SSD_INNER = 256
SSD_HEADS = 4
SSD_GROUPS = 2
SSD_STATE = 128
SSD_CONV_DIM = 768
GDN_WIDTH = 256
GDN_HEAD_DIM = 64
GDN_HEADS = 4
GDN_CONV_DIM = 768
S5_WIDTH = 256
S5_GROUP_CH = 16
S5_GROUPS = 16
S5_STATE = 64
S5_N = S5_GROUPS * S5_STATE
MEM_TOKENS = 256
MEM_HEADS = 4
MEM_HEAD_DIM = 256
FFN_HIDDEN = 2816
IN_COLS = 2828

U_GATE, U_LRUX, U_SSDZ, U_XBC, U_QKV, U_GDNZ, U_S5, U_SMALL = 0, 256, 512, 768, 1536, 2304, 2560, 2816
U_COLS = 2944
SM_DT, SM_BETA, SM_A = 0, 4, 8

LANE = 128
SUBLANE = 8
VMEM_LIMIT = 56 * 1024 * 1024
HIST = SUBLANE
ROW_TILE = 512
MIXER_BLOCK = 256


def _mm(a, b):
    return jnp.dot(a.astype(MXU_DTYPE), b.astype(MXU_DTYPE), preferred_element_type=F32)


def _mm_nt(a, b):
    return lax.dot_general(a.astype(MXU_DTYPE), b.astype(MXU_DTYPE), (((1,), (1,)), ((), ())),
                           preferred_element_type=F32)


def _mm_tn(a, b):
    return lax.dot_general(a.astype(MXU_DTYPE), b.astype(MXU_DTYPE), (((0,), (0,)), ((), ())),
                           preferred_element_type=F32)


def _rms(x, g):
    ms = jnp.mean(x * x, axis=-1, keepdims=True)
    return x * lax.rsqrt(ms + RMS_EPS) * g


def _silu(x):
    return x * jax.nn.sigmoid(x)


def _gelu(x):
    return jax.nn.gelu(x, approximate=True)


def _shift_rows(x, k, row, fill):
    return jnp.where(row >= k, pltpu.roll(x, k, 0), fill)


def _in_proj_kernel(x_ref, g_ref, w_ref, u_ref):
    h = _rms(x_ref[...], g_ref[...])
    u_ref[...] = _mm(h, w_ref[...])


def _in_proj(x2d, g_all, w_all, l):
    R = x2d.shape[0]
    TM = min(R, MIXER_BLOCK)
    return pl.pallas_call(
        _in_proj_kernel,
        grid=(R // TM,),
        in_specs=[pl.BlockSpec((TM, D_MODEL), lambda i: (i, 0)),
                  pl.BlockSpec((None, 1, D_MODEL), lambda i: (l, 0, 0)),
                  pl.BlockSpec((None, D_MODEL, U_COLS), lambda i: (l, 0, 0))],
        out_specs=pl.BlockSpec((TM, U_COLS), lambda i: (i, 0)),
        out_shape=jax.ShapeDtypeStruct((R, U_COLS), F32),
        compiler_params=pltpu.CompilerParams(dimension_semantics=("parallel",), vmem_limit_bytes=VMEM_LIMIT),
        name="in_proj",
    )(x2d, g_all, w_all)


def _conv_block(xp_ref, x, w_ref, w_row0, TB):
    xp_ref[pl.ds(HIST, TB), :] = x
    y = x * w_ref[w_row0 + CONV_WIDTH - 1:w_row0 + CONV_WIDTH, :]
    for j in range(CONV_WIDTH - 1):
        off = HIST - (CONV_WIDTH - 1) + j
        y = y + xp_ref[pl.ds(off, TB), :] * w_ref[w_row0 + j:w_row0 + j + 1, :]
    return y


def _interleave(streams):
    streams = list(streams)
    while streams:
        for s in list(streams):
            try:
                next(s)
            except StopIteration:
                streams.remove(s)


def _mixer_kernel(TB, QG, NT, fused, *refs):
    if fused:
        x_ref, g_ref, win_ref = refs[:3]
        hb = _rms(x_ref[0], g_ref[...]).astype(MXU_DTYPE)
    else:
        u_ref = refs[0]

    def useg(start, width):
        if fused:
            return _mm(hb, win_ref[:, start:start + width])
        return u_ref[0, :, start:start + width]

    _mixer_body(TB, QG, NT, useg, *refs[3 if fused else 1:])


def _mixer_body(TB, QG, NT, useg,
                  lru_h0, lru_b0, ssd_s0, ssd_b0, gdn_s0, gdn_b0, s5r0, s5i0,
                  v256, v768, v128, v1024, wri, bmat, cmat, wglu,
                  mix_ref, lru_h_o, lru_b_o, ssd_s_o, ssd_b_o, gdn_s_o, gdn_b_o, s5r_o, s5i_o,
                  xp_lru, xp_ssd, xp_gdn, h_lru, s_ssd, s_gdn, x_s5, hs_scr, x5_scr):
    t = pl.program_id(1)
    K1 = CONV_WIDTH - 1
    NG = TB // SUBLANE

    @pl.when(t == 0)
    def _init():
        for xp, b0 in ((xp_lru, lru_b0), (xp_ssd, ssd_b0), (xp_gdn, gdn_b0)):
            xp[pl.ds(0, HIST), :] = jnp.zeros((HIST, xp.shape[1]), F32)
            xp[pl.ds(HIST - K1, K1), :] = b0[0]
        h_lru[...] = jnp.zeros(h_lru.shape, F32)
        h_lru[0:1, :] = lru_h0[0]
        s_ssd[...] = ssd_s0[0]
        s_gdn[...] = gdn_s0[0]
        x_s5[...] = jnp.zeros(x_s5.shape, F32)
        x_s5[0, SUBLANE - 1:SUBLANE, :] = s5r0[0]
        x_s5[1, SUBLANE - 1:SUBLANE, :] = s5i0[0]

    row128 = lax.broadcasted_iota(jnp.int32, (TB, LANE), 0)
    lane128 = lax.broadcasted_iota(jnp.int32, (TB, LANE), 1)
    small = useg(U_SMALL, LANE)
    sp = jax.nn.softplus(small + v128[0:1, :])
    sig = jax.nn.sigmoid(small)
    gcum = sp * v128[1:2, :]
    rk = jnp.where(lane128 < SM_BETA, row128, row128 & (QG - 1))
    k = 1
    while k < TB:
        gcum = gcum + jnp.where(rk >= k, pltpu.roll(gcum, k, 0), 0.0)
        k *= 2
    if TB < LANE:
        gcum_t = jnp.concatenate([gcum, jnp.zeros((LANE - TB, LANE), F32)], axis=0).T
    else:
        gcum_t = gcum.T
    rr = lax.broadcasted_iota(jnp.int32, (TB, TB), 0)
    cc = lax.broadcasted_iota(jnp.int32, (TB, TB), 1)

    def lru():
        u_lru = useg(U_GATE, 2 * LRU_WIDTH)
        gate = u_lru[:, 0:LRU_WIDTH]
        xc = _conv_block(xp_lru, u_lru[:, LRU_WIDTH:2 * LRU_WIDTH], v256, 0, TB) + v256[4:5, :]
        ri = _mm(xc, wri[...])
        yield
        r = jax.nn.sigmoid(ri[:, :LRU_WIDTH] + v256[5:6, :])
        ig = jax.nn.sigmoid(ri[:, LRU_WIDTH:] + v256[6:7, :])
        log_a = (-LRU_C) * r * v256[7:8, :]
        a = jnp.exp(log_a)
        bt = jnp.sqrt(-jnp.tanh(log_a) * (a * a + 1.0)) * (ig * xc)
        yield
        row8 = lax.broadcasted_iota(jnp.int32, (SUBLANE, LRU_WIDTH), 0)
        h_prev = h_lru[0:1, :]
        for i in range(NG):
            a_g = a[i * SUBLANE:(i + 1) * SUBLANE]
            b_g = bt[i * SUBLANE:(i + 1) * SUBLANE]
            k = 1
            while k < SUBLANE:
                b_g = b_g + a_g * _shift_rows(b_g, k, row8, 0.0)
                a_g = a_g * _shift_rows(a_g, k, row8, 1.0)
                k *= 2
            h_g = b_g + a_g * h_prev
            h_prev = h_g[SUBLANE - 1:SUBLANE, :]
            hs_scr[pl.ds(i * SUBLANE, SUBLANE), :] = h_g
            if i % 4 == 3:
                yield
        h_lru[0:1, :] = h_prev
        mix_ref[0, :, 0:LRU_WIDTH] = hs_scr[...] * _gelu(gate)

    def ssd():
        u_ssd = useg(U_SSDZ, SSD_INNER + SSD_CONV_DIM)
        z = u_ssd[:, 0:SSD_INNER]
        xbc = _silu(_conv_block(xp_ssd, u_ssd[:, SSD_INNER:SSD_INNER + SSD_CONV_DIM], v768, 0, TB) + v768[4:5, :])
        yield
        causal = rr >= cc
        lo = lane128 < (LANE // 2)
        row_s = lax.broadcasted_iota(jnp.int32, (LANE, SSD_STATE), 0) < (LANE // 2)
        y_pairs = []
        for p in range(SSD_GROUPS):
            xs_p = xbc[:, p * LANE:(p + 1) * LANE]
            b_p = xbc[:, SSD_INNER + p * SSD_STATE:SSD_INNER + (p + 1) * SSD_STATE]
            c_p = xbc[:, SSD_INNER + SSD_GROUPS * SSD_STATE + p * SSD_STATE:
                      SSD_INNER + SSD_GROUPS * SSD_STATE + (p + 1) * SSD_STATE]
            cb = _mm_nt(c_p, b_p)
            h0, h1 = 2 * p, 2 * p + 1
            xdt = xs_p * jnp.where(lo, sp[:, SM_DT + h0:SM_DT + h0 + 1], sp[:, SM_DT + h1:SM_DT + h1 + 1])
            s_pair = s_ssd[p]
            ys = []
            for h in (h0, h1):
                gcol = gcum[:, SM_DT + h:SM_DT + h + 1]
                grow = gcum_t[SM_DT + h:SM_DT + h + 1, 0:TB]
                seg = jnp.exp(jnp.where(causal, gcol - grow, -jnp.inf))
                y_diag = _mm(cb * seg, xdt)
                y_off = _mm_nt(c_p * jnp.exp(gcol), s_pair)
                ys.append(y_diag + y_off)
                yield
            y_pairs.append(jnp.where(lo, ys[0], ys[1]))
            g0 = gcum[:, SM_DT + h0:SM_DT + h0 + 1]
            g1 = gcum[:, SM_DT + h1:SM_DT + h1 + 1]
            gl0 = g0[TB - 1:TB, :]
            gl1 = g1[TB - 1:TB, :]
            to_end = jnp.where(lo, jnp.exp(gl0 - g0), jnp.exp(gl1 - g1))
            s_ssd[p] = jnp.where(row_s, jnp.exp(gl0), jnp.exp(gl1)) * s_pair + _mm_tn(xdt * to_end, b_p)
            yield
        y = jnp.concatenate(y_pairs, axis=-1) + v256[8:9, :] * xbc[:, 0:SSD_INNER]
        yg = y * _silu(z)
        gs = SSD_INNER // SSD_GROUPS
        for p in range(SSD_GROUPS):
            mix_ref[0, :, LRU_WIDTH + p * gs:LRU_WIDTH + (p + 1) * gs] = _rms(
                yg[:, p * gs:(p + 1) * gs], v256[9:10, p * gs:(p + 1) * gs])

    def gdn():
        lane = lax.broadcasted_iota(jnp.int32, (TB, GDN_WIDTH), 1)
        hmask = [(lane >= h * GDN_HEAD_DIM) & (lane < (h + 1) * GDN_HEAD_DIM) for h in range(GDN_HEADS)]

        def by_head(vals):
            return jnp.where(hmask[0], vals[0], jnp.where(hmask[1], vals[1], jnp.where(hmask[2], vals[2], vals[3])))

        def head_sum(x):
            return by_head([jnp.sum(jnp.where(m, x, 0.0), axis=-1, keepdims=True) for m in hmask])

        u_gdn = useg(U_QKV, GDN_CONV_DIM + GDN_WIDTH)
        qkv = _silu(_conv_block(xp_gdn, u_gdn[:, 0:GDN_CONV_DIM], v768, 5, TB))
        qf = qkv[:, 0:GDN_WIDTH]
        kf = qkv[:, GDN_WIDTH:2 * GDN_WIDTH]
        vf = qkv[:, 2 * GDN_WIDTH:3 * GDN_WIDTH]
        qf = qf * lax.rsqrt(head_sum(qf * qf) + 1e-6) * (GDN_HEAD_DIM ** -0.5)
        kf = kf * lax.rsqrt(head_sum(kf * kf) + 1e-6)
        beta_f = by_head([sig[:, SM_BETA + h:SM_BETA + h + 1] for h in range(GDN_HEADS)])
        gc_f = by_head([gcum[:, SM_A + h:SM_A + h + 1] for h in range(GDN_HEADS)])
        chunk_shift = int(math.log2(QG))
        same_chunk = lax.shift_right_logical(rr, chunk_shift) == lax.shift_right_logical(cc, chunk_shift)
        incl = same_chunk & (rr >= cc)
        strict = same_chunk & (rr > cc)
        eg = jnp.exp(gc_f)
        kq = _mm_nt(jnp.concatenate([jnp.where(m, kf, 0.0) for m in hmask] + [jnp.where(m, qf, 0.0) for m in hmask],
                                    axis=0), kf)
        rhs = jnp.concatenate([beta_f * vf, beta_f * eg * kf], axis=1)
        yield
        decs, pws, rs = [], [], []
        for h in range(GDN_HEADS):
            gcol = gcum[:, SM_A + h:SM_A + h + 1]
            grow = gcum_t[SM_A + h:SM_A + h + 1, 0:TB]
            dec = jnp.exp(jnp.where(incl, gcol - grow, -jnp.inf))
            low = jnp.where(strict, sig[:, SM_BETA + h:SM_BETA + h + 1] * kq[h * TB:(h + 1) * TB] * dec, 0.0)
            decs.append(dec)
            pws.append(-low)
            rs.append(-low)
        yield
        for _ in range(chunk_shift - 1):
            for h in range(GDN_HEADS):
                pws[h] = _mm(pws[h], pws[h])
                rs[h] = rs[h] + pws[h] + _mm(rs[h], pws[h])
                yield
        sols, qks = [], []
        for h in range(GDN_HEADS):
            sols.append(rhs + _mm(rs[h], rhs))
            qks.append(kq[(GDN_HEADS + h) * TB:(GDN_HEADS + h + 1) * TB] * decs[h])
            yield
        u_all = by_head([s[:, 0:GDN_WIDTH] for s in sols])
        w_all = by_head([s[:, GDN_WIDTH:2 * GDN_WIDTH] for s in sols])
        q_dec = qf * eg
        head_shift = int(math.log2(GDN_HEAD_DIM))
        r2 = lax.shift_right_logical(lax.broadcasted_iota(jnp.int32, (GDN_WIDTH, GDN_WIDTH), 0), head_shift)
        c2 = lax.shift_right_logical(lax.broadcasted_iota(jnp.int32, (GDN_WIDTH, GDN_WIDTH), 1), head_shift)
        blockdiag = r2 == c2
        s_bd = s_gdn[...]
        deltas, o_state = [], []
        for c in range(TB // QG):
            sl = slice(c * QG, (c + 1) * QG)
            ws = _mm(jnp.concatenate([w_all[sl], q_dec[sl]], axis=0), s_bd)
            delta = u_all[sl] - ws[0:QG]
            o_state.append(ws[QG:2 * QG])
            yield
            g_c = gc_f[sl]
            g_last = g_c[QG - 1:QG, :]
            s_bd = s_bd * jnp.exp(g_last) + jnp.where(blockdiag, _mm_tn(kf[sl] * jnp.exp(g_last - g_c), delta), 0.0)
            deltas.append(delta)
            yield
        s_gdn[...] = s_bd
        delta_all = jnp.concatenate(deltas, axis=0)
        o = jnp.concatenate(o_state, axis=0) + by_head([_mm(qks[h], delta_all) for h in range(GDN_HEADS)])
        yield
        ms = head_sum(o * o) * (1.0 / GDN_HEAD_DIM)
        o = o * lax.rsqrt(ms + RMS_EPS) * v256[10:11, :]
        zg = u_gdn[:, GDN_CONV_DIM:GDN_CONV_DIM + GDN_WIDTH]
        mix_ref[0, :, 2 * LRU_WIDTH:2 * LRU_WIDTH + GDN_WIDTH] = o * _silu(zg)

    def s5():
        row_n = lax.broadcasted_iota(jnp.int32, (SUBLANE, S5_N), 0)
        us5 = useg(U_S5, S5_WIDTH)
        bu = _mm(us5, bmat[...])
        yield
        lam_r, lam_i = v1024[0:1, :], v1024[1:2, :]
        pows = []
        pr, pi = lam_r, lam_i
        k = 1
        while k < SUBLANE:
            pows.append((k, jnp.where(row_n >= k, pr, 0.0), jnp.where(row_n >= k, pi, 0.0)))
            pr, pi = pr * pr - pi * pi, 2.0 * (pr * pi)
            k *= 2
        in_r = jnp.where(row_n == 0, lam_r, 0.0)
        in_i = jnp.where(row_n == 0, lam_i, 0.0)
        g_r = x_s5[0]
        g_i = x_s5[1]
        for i in range(NG):
            p_r = pltpu.roll(g_r, 1, 0)
            p_i = pltpu.roll(g_i, 1, 0)
            g_r = bu[i * SUBLANE:(i + 1) * SUBLANE, 0:S5_N] + (in_r * p_r - in_i * p_i)
            g_i = bu[i * SUBLANE:(i + 1) * SUBLANE, S5_N:2 * S5_N] + (in_r * p_i + in_i * p_r)
            for k, pr, pi in pows:
                s_r = pltpu.roll(g_r, k, 0)
                s_i = pltpu.roll(g_i, k, 0)
                g_r, g_i = g_r + (pr * s_r - pi * s_i), g_i + (pr * s_i + pi * s_r)
            x5_scr[pl.ds(i * SUBLANE, SUBLANE), 0:S5_N] = g_r
            x5_scr[pl.ds(i * SUBLANE, SUBLANE), S5_N:2 * S5_N] = g_i
            yield
        x_s5[0] = g_r
        x_s5[1] = g_i
        y5 = _mm(x5_scr[...], cmat[...])
        y5 = _gelu(y5 + v256[11:12, :] * us5)
        yield
        mix_ref[0, :, 3 * LRU_WIDTH:3 * LRU_WIDTH + S5_WIDTH] = y5 * jax.nn.sigmoid(_mm(y5, wglu[...]) + v256[12:13, :])

    _interleave([gdn(), s5(), ssd(), lru()])

    for xp in (xp_lru, xp_ssd, xp_gdn):
        xp[pl.ds(0, HIST), :] = xp[pl.ds(TB, HIST), :]

    @pl.when(t == NT - 1)
    def _emit():
        lru_h_o[0] = h_lru[0:1, :]
        lru_b_o[0] = xp_lru[pl.ds(HIST - K1, K1), :]
        ssd_b_o[0] = xp_ssd[pl.ds(HIST - K1, K1), :]
        gdn_b_o[0] = xp_gdn[pl.ds(HIST - K1, K1), :]
        ssd_s_o[0] = s_ssd[...]
        gdn_s_o[0] = s_gdn[...]
        s5r_o[0] = x_s5[0, SUBLANE - 1:SUBLANE, :]
        s5i_o[0] = x_s5[1, SUBLANE - 1:SUBLANE, :]


def _mixer(src, states, pk, l, fused):
    B, T, _ = src.shape
    TB = min(T, MIXER_BLOCK)
    QG = min(TB, 64)
    NT = T // TB
    lru_h0, lru_b0, ssd_s0, ssd_b0, gdn_s0, gdn_b0, s5r0, s5i0 = states

    def per_b(shape):
        nd = len(shape)
        return pl.BlockSpec((1,) + tuple(shape[1:]), lambda b, t: (b,) + (0,) * (nd - 1))

    def per_l(arr):
        nd = arr.ndim
        return pl.BlockSpec((None,) + tuple(arr.shape[1:]), lambda b, t: (l,) + (0,) * (nd - 1))

    proj = (pk["g_mix"], pk["w_in"]) if fused else ()
    params = (pk["v256"], pk["v768"], pk["v128"], pk["v1024"], pk["wri"], pk["bmat"], pk["cmat"], pk["wglu"])
    state_shapes = [s.shape for s in states]
    out_shape = [jax.ShapeDtypeStruct((B, T, D_MODEL), F32)] + [jax.ShapeDtypeStruct(s, F32) for s in state_shapes]
    outs = pl.pallas_call(
        functools.partial(_mixer_kernel, TB, QG, NT, fused),
        grid=(B, NT),
        in_specs=[pl.BlockSpec((1, TB, src.shape[2]), lambda b, t: (b, t, 0))] + [per_l(p) for p in proj]
                 + [per_b(s) for s in state_shapes] + [per_l(p) for p in params],
        out_specs=[pl.BlockSpec((1, TB, D_MODEL), lambda b, t: (b, t, 0))] + [per_b(s) for s in state_shapes],
        out_shape=out_shape,
        scratch_shapes=[pltpu.VMEM((TB + HIST, LRU_WIDTH), F32),
                        pltpu.VMEM((TB + HIST, SSD_CONV_DIM), F32),
                        pltpu.VMEM((TB + HIST, GDN_CONV_DIM), F32),
                        pltpu.VMEM((SUBLANE, LRU_WIDTH), F32),
                        pltpu.VMEM((SSD_GROUPS, LANE, SSD_STATE), F32),
                        pltpu.VMEM((GDN_WIDTH, GDN_WIDTH), F32),
                        pltpu.VMEM((2, SUBLANE, S5_N), F32),
                        pltpu.VMEM((TB, LRU_WIDTH), F32),
                        pltpu.VMEM((TB, 2 * S5_N), F32)],
        compiler_params=pltpu.CompilerParams(dimension_semantics=("arbitrary", "arbitrary"),
                                             vmem_limit_bytes=VMEM_LIMIT),
        name="mixer",
    )(src, *proj, *states, *params)
    return outs[0], tuple(outs[1:])


def _attend(q, k_ref, v_ref):
    heads = []
    for h in range(MEM_HEADS):
        sl = slice(h * MEM_HEAD_DIM, (h + 1) * MEM_HEAD_DIM)
        s = _mm_nt(q[:, sl], k_ref[:, sl]) * (MEM_HEAD_DIM ** -0.5)
        e = jnp.exp(s - jnp.max(s, axis=-1, keepdims=True))
        p = e / jnp.sum(e, axis=-1, keepdims=True)
        heads.append(_mm(p, v_ref[:, sl]))
    return jnp.concatenate(heads, axis=-1)


def _attn_kernel(x_ref, mix_ref, wout, g_ref, wq, k_ref, v_ref, wo, o_ref):
    x1 = x_ref[0] + _mm(mix_ref[0], wout[...])
    q = _mm(_rms(x1, g_ref[...]), wq[...])
    o_ref[0] = x1 + _mm(_attend(q, k_ref, v_ref), wo[...])


def _attn_short_kernel(T, x_ref, mix_ref, wout, g_ref, wq, k_ref, v_ref, wo, o_ref, x1_scr, q_scr, a_scr):
    b = pl.program_id(0)

    @pl.when(b == 0)
    def _project_in():
        x1 = x_ref[...] + _mm(mix_ref[...], wout[...])
        x1_scr[...] = x1
        q_scr[...] = _mm(_rms(x1, g_ref[...]), wq[...])

    rows = pl.ds(pl.multiple_of(b * T, T), T)
    a_scr[rows, :] = _attend(q_scr[rows, :], k_ref, v_ref)

    @pl.when(b == pl.num_programs(0) - 1)
    def _project_out():
        o_ref[...] = x1_scr[...] + _mm(a_scr[...], wo[...])


def _attn(x, mix, mem_k, mem_v, lkv, pk, l):
    B, T, _ = x.shape

    def per_l(arr):
        nd = arr.ndim
        return pl.BlockSpec((None,) + tuple(arr.shape[1:]), lambda *i: (l,) + (0,) * (nd - 1),
                            pipeline_mode=pl.Buffered(1))

    weights = (pk["w_out"], pk["g_cross"], pk["w_cq"])
    if T >= MIXER_BLOCK:
        TB = min(T, ROW_TILE)
        blk = pl.BlockSpec((1, TB, D_MODEL), lambda b, t: (b, t, 0))
        kv = pl.BlockSpec((None, None, MEM_TOKENS, D_MODEL), lambda b, t: (lkv, b, 0, 0))
        return pl.pallas_call(
            _attn_kernel,
            grid=(B, T // TB),
            in_specs=[blk, blk] + [per_l(w) for w in weights] + [kv, kv, per_l(pk["w_co"])],
            out_specs=blk,
            out_shape=jax.ShapeDtypeStruct((B, T, D_MODEL), F32),
            compiler_params=pltpu.CompilerParams(dimension_semantics=("parallel", "parallel"),
                                                 vmem_limit_bytes=VMEM_LIMIT),
            name="attn",
        )(x, mix, *weights, mem_k, mem_v, pk["w_co"])
    R = B * T
    rows = pl.BlockSpec((R, D_MODEL), lambda b: (0, 0))
    kv = pl.BlockSpec((None, None, MEM_TOKENS, D_MODEL), lambda b: (lkv, b, 0, 0))
    out = pl.pallas_call(
        functools.partial(_attn_short_kernel, T),
        grid=(B,),
        in_specs=[rows, rows] + [per_l(w) for w in weights] + [kv, kv, per_l(pk["w_co"])],
        out_specs=rows,
        out_shape=jax.ShapeDtypeStruct((R, D_MODEL), F32),
        scratch_shapes=[pltpu.VMEM((R, D_MODEL), F32)] * 3,
        compiler_params=pltpu.CompilerParams(dimension_semantics=("arbitrary",), vmem_limit_bytes=VMEM_LIMIT),
        name="attn_short",
    )(x.reshape(R, D_MODEL), mix.reshape(R, D_MODEL), *weights, mem_k, mem_v, pk["w_co"])
    return out.reshape(B, T, D_MODEL)


def _ffn_kernel(final, x_ref, g_ref, wg, wu, wd, gf_ref, o_ref):
    x = x_ref[...]
    h = _rms(x, g_ref[...]).astype(MXU_DTYPE)
    y = x + _mm(_silu(_mm(h, wg[...])) * _mm(h, wu[...]), wd[...])
    if final:
        y = _rms(y, gf_ref[...])
    o_ref[...] = y


def _ffn(x2d, pk, l, final):
    R = x2d.shape[0]
    TM = min(R, ROW_TILE)

    def per_l(arr):
        nd = arr.ndim
        return pl.BlockSpec((None,) + tuple(arr.shape[1:]), lambda i: (l,) + (0,) * (nd - 1),
                            pipeline_mode=pl.Buffered(1))

    blk = pl.BlockSpec((TM, D_MODEL), lambda i: (i, 0))
    return pl.pallas_call(
        functools.partial(_ffn_kernel, final),
        grid=(R // TM,),
        in_specs=[blk, per_l(pk["g_ffn"]), per_l(pk["w_gate"]), per_l(pk["w_up"]), per_l(pk["w_down"]),
                  pl.BlockSpec((1, D_MODEL), lambda i: (0, 0))],
        out_specs=blk,
        out_shape=jax.ShapeDtypeStruct((R, D_MODEL), F32),
        compiler_params=pltpu.CompilerParams(dimension_semantics=("parallel",), vmem_limit_bytes=VMEM_LIMIT),
        name="ffn",
    )(x2d, pk["g_ffn"], pk["w_gate"], pk["w_up"], pk["w_down"], pk["g_final"])


def _memkv_kernel(m_ref, g_ref, wk, wv, k_ref, v_ref):
    h = _rms(m_ref[0], g_ref[...]).astype(MXU_DTYPE)
    k_ref[0] = _mm(h, wk[...])
    v_ref[0] = _mm(h, wv[...])


def _memory_kv(mem, pk, l):
    B = mem.shape[0]

    def per_l(arr):
        nd = arr.ndim
        return pl.BlockSpec((None,) + tuple(arr.shape[1:]), lambda b: (l,) + (0,) * (nd - 1))

    blk = pl.BlockSpec((1, MEM_TOKENS, D_MODEL), lambda b: (b, 0, 0))
    return pl.pallas_call(
        _memkv_kernel,
        grid=(B,),
        in_specs=[blk, per_l(pk["g_mem"]), per_l(pk["w_ck"]), per_l(pk["w_cv"])],
        out_specs=[blk, blk],
        out_shape=[jax.ShapeDtypeStruct((B, MEM_TOKENS, D_MODEL), F32)] * 2,
        compiler_params=pltpu.CompilerParams(dimension_semantics=("parallel",), vmem_limit_bytes=VMEM_LIMIT),
        name="memory_kv",
    )(mem, pk["g_mem"], pk["w_ck"], pk["w_cv"])


def _block_diag(w):
    L, n, a, b = w.shape
    return jnp.einsum('lnab,nm->lnamb', w, jnp.eye(n, dtype=w.dtype)).reshape(L, n * a, n * b)


def _pack_rows(rows, n_rows):
    L, width = rows[0].shape[0], rows[0].shape[-1]
    rows = [r.reshape(L, -1, width) for r in rows]
    used = sum(r.shape[1] for r in rows)
    return jnp.concatenate(rows + [jnp.zeros((L, n_rows - used, width), F32)], axis=1)


def _prepare(norm_mix_g, w_in, w_out, lru_conv_w, lru_conv_b, lru_w_r, lru_b_r, lru_w_i, lru_b_i, lru_lambda,
             ssd_conv_w, ssd_conv_b, ssd_dt_bias, ssd_a_log, ssd_d, ssd_norm_g,
             gdn_conv_w, gdn_dt_bias, gdn_a_log, gdn_norm_g,
             s5_a_re, s5_a_im, s5_log_dt, s5_b_re, s5_b_im, s5_c_re, s5_c_im, s5_d, s5_w_glu, s5_b_glu,
             norm_mem_g, norm_cross_g, w_cq, w_ck, w_cv, w_co,
             norm_ffn_g, w_ffn_gate, w_ffn_up, w_ffn_down, norm_final_g):
    L = DEPTH
    bf = MXU_DTYPE
    pk = {}
    pk["g_mix"] = norm_mix_g.reshape(L, 1, D_MODEL)
    pk["w_in"] = jnp.concatenate(
        [w_in[:, :, 0:1536], w_in[:, :, 1540:2564], w_in[:, :, 2572:2828], w_in[:, :, 1536:1540],
         w_in[:, :, 2564:2572], jnp.zeros((L, D_MODEL, U_COLS - IN_COLS), F32)], axis=2).astype(bf)
    pk["w_out"] = w_out.astype(bf)
    sp_lam = jax.nn.softplus(-lru_lambda)
    pk["v256"] = _pack_rows(
        [lru_conv_w, lru_conv_b, lru_b_r, lru_b_i, sp_lam, jnp.repeat(ssd_d, SSD_INNER // SSD_HEADS, axis=1),
         ssd_norm_g, jnp.tile(gdn_norm_g, (1, GDN_HEADS)), s5_d, s5_b_glu], 16)
    pk["v768"] = _pack_rows([ssd_conv_w, ssd_conv_b, gdn_conv_w], 16)
    zeros4 = jnp.zeros((L, 4), F32)
    bias = jnp.concatenate([ssd_dt_bias, zeros4, gdn_dt_bias, jnp.zeros((L, LANE - 12), F32)], axis=1)
    avec = jnp.concatenate([-jnp.exp(ssd_a_log.astype(F32)), zeros4, -jnp.exp(gdn_a_log),
                            jnp.zeros((L, LANE - 12), F32)], axis=1)
    pk["v128"] = _pack_rows([bias, avec], 8)
    pk["wri"] = jnp.concatenate([_block_diag(lru_w_r), _block_diag(lru_w_i)], axis=-1).astype(bf)
    a_re = s5_a_re.astype(F32)
    a_im = s5_a_im.astype(F32)
    step = jnp.exp(s5_log_dt.astype(F32))[:, :, None]
    mag = jnp.exp(a_re * step)
    ang = a_im * step
    lb_re, lb_im = mag * jnp.cos(ang), mag * jnp.sin(ang)
    den = a_re * a_re + a_im * a_im
    f_re = ((lb_re - 1.0) * a_re + lb_im * a_im) / den
    f_im = (lb_im * a_re - (lb_re - 1.0) * a_im) / den
    b_re = s5_b_re.astype(F32)
    b_im = s5_b_im.astype(F32)
    bb_re = f_re[..., None] * b_re - f_im[..., None] * b_im
    bb_im = f_re[..., None] * b_im + f_im[..., None] * b_re
    pk["v1024"] = _pack_rows([lb_re.reshape(L, S5_N), lb_im.reshape(L, S5_N)], 8)
    pk["bmat"] = jnp.concatenate([_block_diag(jnp.swapaxes(bb_re, 2, 3)), _block_diag(jnp.swapaxes(bb_im, 2, 3))],
                                 axis=-1).astype(bf)
    pk["cmat"] = jnp.concatenate([_block_diag(jnp.swapaxes(s5_c_re, 2, 3)), -_block_diag(jnp.swapaxes(s5_c_im, 2, 3))],
                                 axis=1).astype(bf)
    pk["wglu"] = s5_w_glu.astype(bf)
    pk["g_mem"] = norm_mem_g.reshape(L, 1, D_MODEL)
    pk["g_cross"] = norm_cross_g.reshape(L, 1, D_MODEL)
    pk["w_cq"], pk["w_ck"], pk["w_cv"], pk["w_co"] = (w.astype(bf) for w in (w_cq, w_ck, w_cv, w_co))
    pk["g_ffn"] = norm_ffn_g.reshape(L, 1, D_MODEL)
    pk["w_gate"], pk["w_up"], pk["w_down"] = (w.astype(bf) for w in (w_ffn_gate, w_ffn_up, w_ffn_down))
    pk["g_final"] = norm_final_g.reshape(1, D_MODEL)
    return pk


def _states_to_kernel(lru_h, lru_buf, ssd, ssd_buf, gdn, gdn_buf, s5_re, s5_im):
    B = lru_h.shape[0]
    eye = jnp.eye(GDN_HEADS, dtype=F32)
    gdn_bd = jnp.einsum('bhkv,hg->bhkgv', gdn, eye).reshape(B, GDN_WIDTH, GDN_WIDTH)
    return (lru_h.reshape(B, 1, LRU_WIDTH), lru_buf, ssd.reshape(B, SSD_GROUPS, LANE, SSD_STATE), ssd_buf,
            gdn_bd, gdn_buf, s5_re.reshape(B, 1, S5_N), s5_im.reshape(B, 1, S5_N))


def _states_from_kernel(st):
    lru_h, lru_buf, ssd, ssd_buf, gdn_bd, gdn_buf, s5_re, s5_im = st
    B = lru_h.shape[0]
    gdn = jnp.einsum('bhkhv->bhkv', gdn_bd.reshape(B, GDN_HEADS, GDN_HEAD_DIM, GDN_HEADS, GDN_HEAD_DIM))
    return (lru_h.reshape(B, LRU_WIDTH), lru_buf, ssd.reshape(B, SSD_HEADS, SSD_INNER // SSD_HEADS, SSD_STATE),
            ssd_buf, gdn, gdn_buf, s5_re.reshape(B, S5_GROUPS, S5_STATE), s5_im.reshape(B, S5_GROUPS, S5_STATE))


def _layer(l, x, mem_k, mem_v, lkv, states, pk, final):
    B, T, _ = x.shape
    if T >= MIXER_BLOCK:
        mix, new_states = _mixer(x, states, pk, l, True)
    else:
        u = _in_proj(x.reshape(B * T, D_MODEL), pk["g_mix"], pk["w_in"], l).reshape(B, T, U_COLS)
        mix, new_states = _mixer(u, states, pk, l, False)
    x = _attn(x, mix, mem_k, mem_v, lkv, pk, l)
    x = _ffn(x.reshape(B * T, D_MODEL), pk, l, final).reshape(B, T, D_MODEL)
    return x, new_states


def kernel(x_prompt, x_sample, mem_prompt, state_lru_h, cache_lru_conv, state_ssd, cache_ssd_conv, state_gdn, cache_gdn_conv, state_s5_re, state_s5_im, cache_mem_k, cache_mem_v, norm_mix_g, w_in, w_out, lru_conv_w, lru_conv_b, lru_w_r, lru_b_r, lru_w_i, lru_b_i, lru_lambda, ssd_conv_w, ssd_conv_b, ssd_dt_bias, ssd_a_log, ssd_d, ssd_norm_g, gdn_conv_w, gdn_dt_bias, gdn_a_log, gdn_norm_g, s5_a_re, s5_a_im, s5_log_dt, s5_b_re, s5_b_im, s5_c_re, s5_c_im, s5_d, s5_w_glu, s5_b_glu, norm_mem_g, norm_cross_g, w_cq, w_ck, w_cv, w_co, norm_ffn_g, w_ffn_gate, w_ffn_up, w_ffn_down, norm_final_g):
    pk = _prepare(norm_mix_g, w_in, w_out, lru_conv_w, lru_conv_b, lru_w_r, lru_b_r, lru_w_i, lru_b_i, lru_lambda,
                  ssd_conv_w, ssd_conv_b, ssd_dt_bias, ssd_a_log, ssd_d, ssd_norm_g,
                  gdn_conv_w, gdn_dt_bias, gdn_a_log, gdn_norm_g,
                  s5_a_re, s5_a_im, s5_log_dt, s5_b_re, s5_b_im, s5_c_re, s5_c_im, s5_d, s5_w_glu, s5_b_glu,
                  norm_mem_g, norm_cross_g, w_cq, w_ck, w_cv, w_co,
                  norm_ffn_g, w_ffn_gate, w_ffn_up, w_ffn_down, norm_final_g)
    bp = x_prompt.shape[0]
    bs = x_sample.shape[0]
    k1 = CONV_WIDTH - 1
    zero_states = _states_to_kernel(
        jnp.zeros((bp, LRU_WIDTH), F32), jnp.zeros((bp, k1, LRU_WIDTH), F32),
        jnp.zeros((bp, SSD_HEADS, SSD_INNER // SSD_HEADS, SSD_STATE), F32), jnp.zeros((bp, k1, SSD_CONV_DIM), F32),
        jnp.zeros((bp, GDN_HEADS, GDN_HEAD_DIM, GDN_HEAD_DIM), F32), jnp.zeros((bp, k1, GDN_CONV_DIM), F32),
        jnp.zeros((bp, S5_GROUPS, S5_STATE), F32), jnp.zeros((bp, S5_GROUPS, S5_STATE), F32))
    cache_k = cache_mem_k.reshape(DEPTH, bs, MEM_TOKENS, D_MODEL).astype(MXU_DTYPE)
    cache_v = cache_mem_v.reshape(DEPTH, bs, MEM_TOKENS, D_MODEL).astype(MXU_DTYPE)
    xp, xs = x_prompt, x_sample
    p_states, s_states, p_mk, p_mv = [], [], [], []
    for l in range(DEPTH):
        final = l == DEPTH - 1
        mk, mv = _memory_kv(mem_prompt, pk, l)
        xp, sp = _layer(l, xp, mk[None], mv[None], 0, zero_states, pk, final)
        s_in = _states_to_kernel(state_lru_h[l], cache_lru_conv[l], state_ssd[l], cache_ssd_conv[l],
                                 state_gdn[l], cache_gdn_conv[l], state_s5_re[l], state_s5_im[l])
        xs, ss = _layer(l, xs, cache_k, cache_v, l, s_in, pk, final)
        p_states.append(_states_from_kernel(sp))
        s_states.append(_states_from_kernel(ss))
        p_mk.append(mk.reshape(bp, MEM_TOKENS, MEM_HEADS, MEM_HEAD_DIM))
        p_mv.append(mv.reshape(bp, MEM_TOKENS, MEM_HEADS, MEM_HEAD_DIM))

    def stack(states, j):
        return jnp.stack([st[j] for st in states], axis=0)

    return ((xp, xs) + tuple(stack(p_states, j) for j in range(8)) + (jnp.stack(p_mk, axis=0), jnp.stack(p_mv, axis=0))
            + tuple(stack(s_states, j) for j in range(8)))
```

```python
import functools
import math

import jax
import jax.numpy as jnp
import numpy as np
from jax import lax
from jax.experimental import pallas as pl
from jax.experimental.pallas import tpu as pltpu

F32 = jnp.float32
MXU_DTYPE = jnp.bfloat16

D_MODEL = 1024
DEPTH = 4
CONV_WIDTH = 4
RMS_EPS = 1e-6
LRU_WIDTH = 256
LRU_BLOCKS = 4
LRU_C = 8.0
SSD_INNER = 256
SSD_HEADS = 4
SSD_GROUPS = 2
SSD_STATE = 128
SSD_CONV_DIM = 768
GDN_WIDTH = 256
GDN_HEAD_DIM = 64
GDN_HEADS = 4
GDN_CONV_DIM = 768
S5_WIDTH = 256
S5_GROUP_CH = 16
S5_GROUPS = 16
S5_STATE = 64
S5_N = S5_GROUPS * S5_STATE
MEM_TOKENS = 256
MEM_HEADS = 4
MEM_HEAD_DIM = 256
FFN_HIDDEN = 2816
IN_COLS = 2828

U_GATE, U_LRUX, U_SSDZ, U_XBC, U_QKV, U_GDNZ, U_S5, U_SMALL = 0, 256, 512, 768, 1536, 2304, 2560, 2816
U_COLS = 2944
SM_DT, SM_BETA, SM_A = 0, 4, 8

LANE = 128
SUBLANE = 8
VMEM_LIMIT = 56 * 1024 * 1024
HIST = SUBLANE
ROW_TILE = 512
MIXER_BLOCK = 256
SHORT_SEQ_BATCH = 8


def _mm(a, b):
    return jnp.dot(a.astype(MXU_DTYPE), b.astype(MXU_DTYPE), preferred_element_type=F32)


def _mm_nt(a, b):
    return lax.dot_general(a.astype(MXU_DTYPE), b.astype(MXU_DTYPE), (((1,), (1,)), ((), ())),
                           preferred_element_type=F32)


def _mm_tn(a, b):
    return lax.dot_general(a.astype(MXU_DTYPE), b.astype(MXU_DTYPE), (((0,), (0,)), ((), ())),
                           preferred_element_type=F32)


def _rms(x, g):
    ms = jnp.mean(x * x, axis=-1, keepdims=True)
    return x * lax.rsqrt(ms + RMS_EPS) * g


def _silu(x):
    return x * jax.nn.sigmoid(x)


def _gelu(x):
    return jax.nn.gelu(x, approximate=True)


def _shift_rows(x, k, row, fill):
    return jnp.where(row >= k, pltpu.roll(x, k, 0), fill)


def _in_proj_kernel(x_ref, g_ref, w_ref, u_ref):
    h = _rms(x_ref[...], g_ref[...])
    u_ref[...] = _mm(h, w_ref[...])


def _in_proj(x2d, g_all, w_all, l):
    R = x2d.shape[0]
    TM = min(R, MIXER_BLOCK)
    return pl.pallas_call(
        _in_proj_kernel,
        grid=(R // TM,),
        in_specs=[pl.BlockSpec((TM, D_MODEL), lambda i: (i, 0)),
                  pl.BlockSpec((None, 1, D_MODEL), lambda i: (l, 0, 0)),
                  pl.BlockSpec((None, D_MODEL, U_COLS), lambda i: (l, 0, 0))],
        out_specs=pl.BlockSpec((TM, U_COLS), lambda i: (i, 0)),
        out_shape=jax.ShapeDtypeStruct((R, U_COLS), F32),
        compiler_params=pltpu.CompilerParams(dimension_semantics=("parallel",), vmem_limit_bytes=VMEM_LIMIT),
        name="in_proj",
    )(x2d, g_all, w_all)


def _conv_block(xp_ref, x, w_ref, w_row0, TB):
    xp_ref[pl.ds(HIST, TB), :] = x
    y = x * w_ref[w_row0 + CONV_WIDTH - 1:w_row0 + CONV_WIDTH, :]
    for j in range(CONV_WIDTH - 1):
        off = HIST - (CONV_WIDTH - 1) + j
        y = y + xp_ref[pl.ds(off, TB), :] * w_ref[w_row0 + j:w_row0 + j + 1, :]
    return y


def _interleave(streams):
    streams = list(streams)
    while streams:
        for s in list(streams):
            try:
                next(s)
            except StopIteration:
                streams.remove(s)


def _mixer_kernel(TB, QG, NT, NB, fused, *refs):
    n_in = (3 if fused else 1) + 8
    n_par = 8
    n_out = 9
    ins, params = refs[:n_in], refs[n_in:n_in + n_par]
    outs = refs[n_in + n_par:n_in + n_par + n_out]
    scratch = refs[n_in + n_par + n_out:]
    streams, finals = [], []
    for bb in range(NB):
        if fused:
            x_ref, g_ref, win_ref = ins[:3]
            hb = _rms(x_ref[bb], g_ref[...]).astype(MXU_DTYPE)
            states_in = ins[3:]

            def useg(start, width, hb=hb, win_ref=win_ref):
                return _mm(hb, win_ref[:, start:start + width])
        else:
            u_ref = ins[0].at[bb]
            states_in = ins[1:]

            def useg(start, width, u_ref=u_ref):
                return u_ref[:, start:start + width]

        seq_streams, finish = _mixer_sequence(TB, QG, NT, useg, *[r.at[bb] for r in states_in], *params,
                                              *[r.at[bb] for r in outs], *[r.at[bb] for r in scratch])
        streams.append(seq_streams)
        finals.append(finish)
    _interleave([seq[j] for j in range(4) for seq in streams])
    for finish in finals:
        finish()


def _mixer_sequence(TB, QG, NT, useg,
                  lru_h0, lru_b0, ssd_s0, ssd_b0, gdn_s0, gdn_b0, s5r0, s5i0,
                  v256, v768, v128, v1024, wri, bmat, cmat, wglu,
                  mix_ref, lru_h_o, lru_b_o, ssd_s_o, ssd_b_o, gdn_s_o, gdn_b_o, s5r_o, s5i_o,
                  xp_lru, xp_ssd, xp_gdn, h_lru, s_ssd, s_gdn, x_s5, hs_scr, x5_scr):
    t = pl.program_id(1)
    K1 = CONV_WIDTH - 1
    NG = TB // SUBLANE

    @pl.when(t == 0)
    def _init():
        for xp, b0 in ((xp_lru, lru_b0), (xp_ssd, ssd_b0), (xp_gdn, gdn_b0)):
            xp[pl.ds(0, HIST), :] = jnp.zeros((HIST, xp.shape[1]), F32)
            xp[pl.ds(HIST - K1, K1), :] = b0[...]
        h_lru[...] = jnp.zeros(h_lru.shape, F32)
        h_lru[0:1, :] = lru_h0[...]
        s_ssd[...] = ssd_s0[...]
        s_gdn[...] = gdn_s0[...]
        x_s5[...] = jnp.zeros(x_s5.shape, F32)
        x_s5[0, SUBLANE - 1:SUBLANE, :] = s5r0[...]
        x_s5[1, SUBLANE - 1:SUBLANE, :] = s5i0[...]

    row128 = lax.broadcasted_iota(jnp.int32, (TB, LANE), 0)
    lane128 = lax.broadcasted_iota(jnp.int32, (TB, LANE), 1)
    small = useg(U_SMALL, LANE)
    sp = jax.nn.softplus(small + v128[0:1, :])
    sig = jax.nn.sigmoid(small)
    gcum = sp * v128[1:2, :]
    rk = jnp.where(lane128 < SM_BETA, row128, row128 & (QG - 1))
    k = 1
    while k < TB:
        gcum = gcum + jnp.where(rk >= k, pltpu.roll(gcum, k, 0), 0.0)
        k *= 2
    if TB < LANE:
        gcum_t = jnp.concatenate([gcum, jnp.zeros((LANE - TB, LANE), F32)], axis=0).T
    else:
        gcum_t = gcum.T
    rr = lax.broadcasted_iota(jnp.int32, (TB, TB), 0)
    cc = lax.broadcasted_iota(jnp.int32, (TB, TB), 1)

    def lru():
        u_lru = useg(U_GATE, 2 * LRU_WIDTH)
        gate = u_lru[:, 0:LRU_WIDTH]
        xc = _conv_block(xp_lru, u_lru[:, LRU_WIDTH:2 * LRU_WIDTH], v256, 0, TB) + v256[4:5, :]
        ri = _mm(xc, wri[...])
        yield
        r = jax.nn.sigmoid(ri[:, :LRU_WIDTH] + v256[5:6, :])
        ig = jax.nn.sigmoid(ri[:, LRU_WIDTH:] + v256[6:7, :])
        log_a = (-LRU_C) * r * v256[7:8, :]
        a = jnp.exp(log_a)
        bt = jnp.sqrt(-jnp.tanh(log_a) * (a * a + 1.0)) * (ig * xc)
        yield
        row8 = lax.broadcasted_iota(jnp.int32, (SUBLANE, LRU_WIDTH), 0)
        h_prev = h_lru[0:1, :]
        for i in range(NG):
            a_g = a[i * SUBLANE:(i + 1) * SUBLANE]
            b_g = bt[i * SUBLANE:(i + 1) * SUBLANE]
            k = 1
            while k < SUBLANE:
                b_g = b_g + a_g * _shift_rows(b_g, k, row8, 0.0)
                a_g = a_g * _shift_rows(a_g, k, row8, 1.0)
                k *= 2
            h_g = b_g + a_g * h_prev
            h_prev = h_g[SUBLANE - 1:SUBLANE, :]
            hs_scr[pl.ds(i * SUBLANE, SUBLANE), :] = h_g
            if i % 4 == 3:
                yield
        h_lru[0:1, :] = h_prev
        mix_ref[:, 0:LRU_WIDTH] = hs_scr[...] * _gelu(gate)

    def ssd():
        u_ssd = useg(U_SSDZ, SSD_INNER + SSD_CONV_DIM)
        z = u_ssd[:, 0:SSD_INNER]
        xbc = _silu(_conv_block(xp_ssd, u_ssd[:, SSD_INNER:SSD_INNER + SSD_CONV_DIM], v768, 0, TB) + v768[4:5, :])
        yield
        causal = rr >= cc
        lo = lane128 < (LANE // 2)
        row_s = lax.broadcasted_iota(jnp.int32, (LANE, SSD_STATE), 0) < (LANE // 2)
        y_pairs = []
        for p in range(SSD_GROUPS):
            xs_p = xbc[:, p * LANE:(p + 1) * LANE]
            b_p = xbc[:, SSD_INNER + p * SSD_STATE:SSD_INNER + (p + 1) * SSD_STATE]
            c_p = xbc[:, SSD_INNER + SSD_GROUPS * SSD_STATE + p * SSD_STATE:
                      SSD_INNER + SSD_GROUPS * SSD_STATE + (p + 1) * SSD_STATE]
            cb = _mm_nt(c_p, b_p)
            h0, h1 = 2 * p, 2 * p + 1
            xdt = xs_p * jnp.where(lo, sp[:, SM_DT + h0:SM_DT + h0 + 1], sp[:, SM_DT + h1:SM_DT + h1 + 1])
            s_pair = s_ssd[p]
            ys = []
            for h in (h0, h1):
                gcol = gcum[:, SM_DT + h:SM_DT + h + 1]
                grow = gcum_t[SM_DT + h:SM_DT + h + 1, 0:TB]
                seg = jnp.exp(jnp.where(causal, gcol - grow, -jnp.inf))
                y_diag = _mm(cb * seg, xdt)
                y_off = _mm_nt(c_p * jnp.exp(gcol), s_pair)
                ys.append(y_diag + y_off)
                yield
            y_pairs.append(jnp.where(lo, ys[0], ys[1]))
            g0 = gcum[:, SM_DT + h0:SM_DT + h0 + 1]
            g1 = gcum[:, SM_DT + h1:SM_DT + h1 + 1]
            gl0 = g0[TB - 1:TB, :]
            gl1 = g1[TB - 1:TB, :]
            to_end = jnp.where(lo, jnp.exp(gl0 - g0), jnp.exp(gl1 - g1))
            s_ssd[p] = jnp.where(row_s, jnp.exp(gl0), jnp.exp(gl1)) * s_pair + _mm_tn(xdt * to_end, b_p)
            yield
        y = jnp.concatenate(y_pairs, axis=-1) + v256[8:9, :] * xbc[:, 0:SSD_INNER]
        yg = y * _silu(z)
        gs = SSD_INNER // SSD_GROUPS
        for p in range(SSD_GROUPS):
            mix_ref[:, LRU_WIDTH + p * gs:LRU_WIDTH + (p + 1) * gs] = _rms(
                yg[:, p * gs:(p + 1) * gs], v256[9:10, p * gs:(p + 1) * gs])

    def gdn():
        lane = lax.broadcasted_iota(jnp.int32, (TB, GDN_WIDTH), 1)
        hmask = [(lane >= h * GDN_HEAD_DIM) & (lane < (h + 1) * GDN_HEAD_DIM) for h in range(GDN_HEADS)]

        def by_head(vals):
            return jnp.where(hmask[0], vals[0], jnp.where(hmask[1], vals[1], jnp.where(hmask[2], vals[2], vals[3])))

        def head_sum(x):
            return by_head([jnp.sum(jnp.where(m, x, 0.0), axis=-1, keepdims=True) for m in hmask])

        u_gdn = useg(U_QKV, GDN_CONV_DIM + GDN_WIDTH)
        qkv = _silu(_conv_block(xp_gdn, u_gdn[:, 0:GDN_CONV_DIM], v768, 5, TB))
        qf = qkv[:, 0:GDN_WIDTH]
        kf = qkv[:, GDN_WIDTH:2 * GDN_WIDTH]
        vf = qkv[:, 2 * GDN_WIDTH:3 * GDN_WIDTH]
        qf = qf * lax.rsqrt(head_sum(qf * qf) + 1e-6) * (GDN_HEAD_DIM ** -0.5)
        kf = kf * lax.rsqrt(head_sum(kf * kf) + 1e-6)
        beta_f = by_head([sig[:, SM_BETA + h:SM_BETA + h + 1] for h in range(GDN_HEADS)])
        gc_f = by_head([gcum[:, SM_A + h:SM_A + h + 1] for h in range(GDN_HEADS)])
        chunk_shift = int(math.log2(QG))
        same_chunk = lax.shift_right_logical(rr, chunk_shift) == lax.shift_right_logical(cc, chunk_shift)
        incl = same_chunk & (rr >= cc)
        strict = same_chunk & (rr > cc)
        eg = jnp.exp(gc_f)
        kq = _mm_nt(jnp.concatenate([jnp.where(m, kf, 0.0) for m in hmask] + [jnp.where(m, qf, 0.0) for m in hmask],
                                    axis=0), kf)
        rhs = jnp.concatenate([beta_f * vf, beta_f * eg * kf], axis=1)
        yield
        decs, pws, rs = [], [], []
        for h in range(GDN_HEADS):
            gcol = gcum[:, SM_A + h:SM_A + h + 1]
            grow = gcum_t[SM_A + h:SM_A + h + 1, 0:TB]
            dec = jnp.exp(jnp.where(incl, gcol - grow, -jnp.inf))
            low = jnp.where(strict, sig[:, SM_BETA + h:SM_BETA + h + 1] * kq[h * TB:(h + 1) * TB] * dec, 0.0)
            decs.append(dec)
            pws.append(-low)
            rs.append(-low)
        yield
        for _ in range(chunk_shift - 1):
            for h in range(GDN_HEADS):
                pws[h] = _mm(pws[h], pws[h])
                rs[h] = rs[h] + pws[h] + _mm(rs[h], pws[h])
                yield
        sols, qks = [], []
        for h in range(GDN_HEADS):
            sols.append(rhs + _mm(rs[h], rhs))
            qks.append(kq[(GDN_HEADS + h) * TB:(GDN_HEADS + h + 1) * TB] * decs[h])
            yield
        u_all = by_head([s[:, 0:GDN_WIDTH] for s in sols])
        w_all = by_head([s[:, GDN_WIDTH:2 * GDN_WIDTH] for s in sols])
        q_dec = qf * eg
        head_shift = int(math.log2(GDN_HEAD_DIM))
        r2 = lax.shift_right_logical(lax.broadcasted_iota(jnp.int32, (GDN_WIDTH, GDN_WIDTH), 0), head_shift)
        c2 = lax.shift_right_logical(lax.broadcasted_iota(jnp.int32, (GDN_WIDTH, GDN_WIDTH), 1), head_shift)
        blockdiag = r2 == c2
        s_bd = s_gdn[...]
        deltas, o_state = [], []
        for c in range(TB // QG):
            sl = slice(c * QG, (c + 1) * QG)
            ws = _mm(jnp.concatenate([w_all[sl], q_dec[sl]], axis=0), s_bd)
            delta = u_all[sl] - ws[0:QG]
            o_state.append(ws[QG:2 * QG])
            yield
            g_c = gc_f[sl]
            g_last = g_c[QG - 1:QG, :]
            s_bd = s_bd * jnp.exp(g_last) + jnp.where(blockdiag, _mm_tn(kf[sl] * jnp.exp(g_last - g_c), delta), 0.0)
            deltas.append(delta)
            yield
        s_gdn[...] = s_bd
        delta_all = jnp.concatenate(deltas, axis=0)
        o = jnp.concatenate(o_state, axis=0) + by_head([_mm(qks[h], delta_all) for h in range(GDN_HEADS)])
        yield
        ms = head_sum(o * o) * (1.0 / GDN_HEAD_DIM)
        o = o * lax.rsqrt(ms + RMS_EPS) * v256[10:11, :]
        zg = u_gdn[:, GDN_CONV_DIM:GDN_CONV_DIM + GDN_WIDTH]
        mix_ref[:, 2 * LRU_WIDTH:2 * LRU_WIDTH + GDN_WIDTH] = o * _silu(zg)

    def s5():
        row_n = lax.broadcasted_iota(jnp.int32, (SUBLANE, S5_N), 0)
        us5 = useg(U_S5, S5_WIDTH)
        bu = _mm(us5, bmat[...])
        yield
        lam_r, lam_i = v1024[0:1, :], v1024[1:2, :]
        pows = []
        pr, pi = lam_r, lam_i
        k = 1
        while k < SUBLANE:
            pows.append((k, jnp.where(row_n >= k, pr, 0.0), jnp.where(row_n >= k, pi, 0.0)))
            pr, pi = pr * pr - pi * pi, 2.0 * (pr * pi)
            k *= 2
        in_r = jnp.where(row_n == 0, lam_r, 0.0)
        in_i = jnp.where(row_n == 0, lam_i, 0.0)
        g_r = x_s5[0]
        g_i = x_s5[1]
        for i in range(NG):
            p_r = pltpu.roll(g_r, 1, 0)
            p_i = pltpu.roll(g_i, 1, 0)
            g_r = bu[i * SUBLANE:(i + 1) * SUBLANE, 0:S5_N] + (in_r * p_r - in_i * p_i)
            g_i = bu[i * SUBLANE:(i + 1) * SUBLANE, S5_N:2 * S5_N] + (in_r * p_i + in_i * p_r)
            for k, pr, pi in pows:
                s_r = pltpu.roll(g_r, k, 0)
                s_i = pltpu.roll(g_i, k, 0)
                g_r, g_i = g_r + (pr * s_r - pi * s_i), g_i + (pr * s_i + pi * s_r)
            x5_scr[pl.ds(i * SUBLANE, SUBLANE), 0:S5_N] = g_r
            x5_scr[pl.ds(i * SUBLANE, SUBLANE), S5_N:2 * S5_N] = g_i
            yield
        x_s5[0] = g_r
        x_s5[1] = g_i
        y5 = _mm(x5_scr[...], cmat[...])
        y5 = _gelu(y5 + v256[11:12, :] * us5)
        yield
        mix_ref[:, 3 * LRU_WIDTH:3 * LRU_WIDTH + S5_WIDTH] = y5 * jax.nn.sigmoid(_mm(y5, wglu[...]) + v256[12:13, :])

    def finish():
        for xp in (xp_lru, xp_ssd, xp_gdn):
            xp[pl.ds(0, HIST), :] = xp[pl.ds(TB, HIST), :]

        pl.when(t == NT - 1)(_emit)

    def _emit():
        lru_h_o[...] = h_lru[0:1, :]
        lru_b_o[...] = xp_lru[pl.ds(HIST - K1, K1), :]
        ssd_b_o[...] = xp_ssd[pl.ds(HIST - K1, K1), :]
        gdn_b_o[...] = xp_gdn[pl.ds(HIST - K1, K1), :]
        ssd_s_o[...] = s_ssd[...]
        gdn_s_o[...] = s_gdn[...]
        s5r_o[...] = x_s5[0, SUBLANE - 1:SUBLANE, :]
        s5i_o[...] = x_s5[1, SUBLANE - 1:SUBLANE, :]

    return [gdn(), s5(), ssd(), lru()], finish


def _mixer(src, states, pk, l, fused):
    B, T, _ = src.shape
    TB = min(T, MIXER_BLOCK)
    QG = min(TB, 64)
    NT = T // TB
    NB = SHORT_SEQ_BATCH if (B % SHORT_SEQ_BATCH == 0 and TB < MIXER_BLOCK) else 1

    def per_b(shape):
        nd = len(shape)
        return pl.BlockSpec((NB,) + tuple(shape[1:]), lambda b, t: (b,) + (0,) * (nd - 1))

    def per_l(arr):
        nd = arr.ndim
        return pl.BlockSpec((None,) + tuple(arr.shape[1:]), lambda b, t: (l,) + (0,) * (nd - 1))

    proj = (pk["g_mix"], pk["w_in"]) if fused else ()
    params = (pk["v256"], pk["v768"], pk["v128"], pk["v1024"], pk["wri"], pk["bmat"], pk["cmat"], pk["wglu"])
    state_shapes = [s.shape for s in states]
    out_shape = [jax.ShapeDtypeStruct((B, T, D_MODEL), F32)] + [jax.ShapeDtypeStruct(s, F32) for s in state_shapes]
    outs = pl.pallas_call(
        functools.partial(_mixer_kernel, TB, QG, NT, NB, fused),
        grid=(B // NB, NT),
        in_specs=[pl.BlockSpec((NB, TB, src.shape[2]), lambda b, t: (b, t, 0))] + [per_l(p) for p in proj]
                 + [per_b(s) for s in state_shapes] + [per_l(p) for p in params],
        out_specs=[pl.BlockSpec((NB, TB, D_MODEL), lambda b, t: (b, t, 0))] + [per_b(s) for s in state_shapes],
        out_shape=out_shape,
        scratch_shapes=[pltpu.VMEM((NB, TB + HIST, LRU_WIDTH), F32),
                        pltpu.VMEM((NB, TB + HIST, SSD_CONV_DIM), F32),
                        pltpu.VMEM((NB, TB + HIST, GDN_CONV_DIM), F32),
                        pltpu.VMEM((NB, SUBLANE, LRU_WIDTH), F32),
                        pltpu.VMEM((NB, SSD_GROUPS, LANE, SSD_STATE), F32),
                        pltpu.VMEM((NB, GDN_WIDTH, GDN_WIDTH), F32),
                        pltpu.VMEM((NB, 2, SUBLANE, S5_N), F32),
                        pltpu.VMEM((NB, TB, LRU_WIDTH), F32),
                        pltpu.VMEM((NB, TB, 2 * S5_N), F32)],
        compiler_params=pltpu.CompilerParams(dimension_semantics=("arbitrary", "arbitrary"),
                                             vmem_limit_bytes=VMEM_LIMIT),
        name="mixer",
    )(src, *proj, *states, *params)
    return outs[0], tuple(outs[1:])


def _attend(q, k_ref, v_ref):
    heads = []
    for h in range(MEM_HEADS):
        sl = slice(h * MEM_HEAD_DIM, (h + 1) * MEM_HEAD_DIM)
        s = _mm_nt(q[:, sl], k_ref[:, sl]) * (MEM_HEAD_DIM ** -0.5)
        e = jnp.exp(s - jnp.max(s, axis=-1, keepdims=True))
        p = e / jnp.sum(e, axis=-1, keepdims=True)
        heads.append(_mm(p, v_ref[:, sl]))
    return jnp.concatenate(heads, axis=-1)


def _attn_kernel(x_ref, mix_ref, wout, g_ref, wq, k_ref, v_ref, wo, o_ref):
    x1 = x_ref[0] + _mm(mix_ref[0], wout[...])
    q = _mm(_rms(x1, g_ref[...]), wq[...])
    o_ref[0] = x1 + _mm(_attend(q, k_ref, v_ref), wo[...])


def _attn_short_kernel(T, x_ref, mix_ref, wout, g_ref, wq, k_ref, v_ref, wo, o_ref, x1_scr, q_scr, a_scr):
    b = pl.program_id(0)

    @pl.when(b == 0)
    def _project_in():
        x1 = x_ref[...] + _mm(mix_ref[...], wout[...])
        x1_scr[...] = x1
        q_scr[...] = _mm(_rms(x1, g_ref[...]), wq[...])

    rows = pl.ds(pl.multiple_of(b * T, T), T)
    a_scr[rows, :] = _attend(q_scr[rows, :], k_ref, v_ref)

    @pl.when(b == pl.num_programs(0) - 1)
    def _project_out():
        o_ref[...] = x1_scr[...] + _mm(a_scr[...], wo[...])


def _attn(x, mix, mem_k, mem_v, lkv, pk, l):
    B, T, _ = x.shape

    def per_l(arr):
        nd = arr.ndim
        return pl.BlockSpec((None,) + tuple(arr.shape[1:]), lambda *i: (l,) + (0,) * (nd - 1),
                            pipeline_mode=pl.Buffered(1))

    weights = (pk["w_out"], pk["g_cross"], pk["w_cq"])
    if T >= MIXER_BLOCK:
        TB = min(T, ROW_TILE)
        blk = pl.BlockSpec((1, TB, D_MODEL), lambda b, t: (b, t, 0))
        kv = pl.BlockSpec((None, None, MEM_TOKENS, D_MODEL), lambda b, t: (lkv, b, 0, 0))
        return pl.pallas_call(
            _attn_kernel,
            grid=(B, T // TB),
            in_specs=[blk, blk] + [per_l(w) for w in weights] + [kv, kv, per_l(pk["w_co"])],
            out_specs=blk,
            out_shape=jax.ShapeDtypeStruct((B, T, D_MODEL), F32),
            compiler_params=pltpu.CompilerParams(dimension_semantics=("parallel", "parallel"),
                                                 vmem_limit_bytes=VMEM_LIMIT),
            name="attn",
        )(x, mix, *weights, mem_k, mem_v, pk["w_co"])
    R = B * T
    rows = pl.BlockSpec((R, D_MODEL), lambda b: (0, 0))
    kv = pl.BlockSpec((None, None, MEM_TOKENS, D_MODEL), lambda b: (lkv, b, 0, 0))
    out = pl.pallas_call(
        functools.partial(_attn_short_kernel, T),
        grid=(B,),
        in_specs=[rows, rows] + [per_l(w) for w in weights] + [kv, kv, per_l(pk["w_co"])],
        out_specs=rows,
        out_shape=jax.ShapeDtypeStruct((R, D_MODEL), F32),
        scratch_shapes=[pltpu.VMEM((R, D_MODEL), F32)] * 3,
        compiler_params=pltpu.CompilerParams(dimension_semantics=("arbitrary",), vmem_limit_bytes=VMEM_LIMIT),
        name="attn_short",
    )(x.reshape(R, D_MODEL), mix.reshape(R, D_MODEL), *weights, mem_k, mem_v, pk["w_co"])
    return out.reshape(B, T, D_MODEL)


def _ffn_kernel(final, x_ref, g_ref, wg, wu, wd, gf_ref, o_ref):
    x = x_ref[...]
    h = _rms(x, g_ref[...]).astype(MXU_DTYPE)
    y = x + _mm(_silu(_mm(h, wg[...])) * _mm(h, wu[...]), wd[...])
    if final:
        y = _rms(y, gf_ref[...])
    o_ref[...] = y


def _ffn(x2d, pk, l, final):
    R = x2d.shape[0]
    TM = min(R, ROW_TILE)

    def per_l(arr):
        nd = arr.ndim
        return pl.BlockSpec((None,) + tuple(arr.shape[1:]), lambda i: (l,) + (0,) * (nd - 1),
                            pipeline_mode=pl.Buffered(1))

    blk = pl.BlockSpec((TM, D_MODEL), lambda i: (i, 0))
    return pl.pallas_call(
        functools.partial(_ffn_kernel, final),
        grid=(R // TM,),
        in_specs=[blk, per_l(pk["g_ffn"]), per_l(pk["w_gate"]), per_l(pk["w_up"]), per_l(pk["w_down"]),
                  pl.BlockSpec((1, D_MODEL), lambda i: (0, 0))],
        out_specs=blk,
        out_shape=jax.ShapeDtypeStruct((R, D_MODEL), F32),
        compiler_params=pltpu.CompilerParams(dimension_semantics=("parallel",), vmem_limit_bytes=VMEM_LIMIT),
        name="ffn",
    )(x2d, pk["g_ffn"], pk["w_gate"], pk["w_up"], pk["w_down"], pk["g_final"])


def _memkv_kernel(m_ref, g_ref, wk, wv, k_ref, v_ref):
    h = _rms(m_ref[0], g_ref[...]).astype(MXU_DTYPE)
    k_ref[0] = _mm(h, wk[...])
    v_ref[0] = _mm(h, wv[...])


def _memory_kv(mem, pk, l):
    B = mem.shape[0]

    def per_l(arr):
        nd = arr.ndim
        return pl.BlockSpec((None,) + tuple(arr.shape[1:]), lambda b: (l,) + (0,) * (nd - 1))

    blk = pl.BlockSpec((1, MEM_TOKENS, D_MODEL), lambda b: (b, 0, 0))
    return pl.pallas_call(
        _memkv_kernel,
        grid=(B,),
        in_specs=[blk, per_l(pk["g_mem"]), per_l(pk["w_ck"]), per_l(pk["w_cv"])],
        out_specs=[blk, blk],
        out_shape=[jax.ShapeDtypeStruct((B, MEM_TOKENS, D_MODEL), F32)] * 2,
        compiler_params=pltpu.CompilerParams(dimension_semantics=("parallel",), vmem_limit_bytes=VMEM_LIMIT),
        name="memory_kv",
    )(mem, pk["g_mem"], pk["w_ck"], pk["w_cv"])


def _block_diag(w):
    L, n, a, b = w.shape
    return jnp.einsum('lnab,nm->lnamb', w, jnp.eye(n, dtype=w.dtype)).reshape(L, n * a, n * b)


def _pack_rows(rows, n_rows):
    L, width = rows[0].shape[0], rows[0].shape[-1]
    rows = [r.reshape(L, -1, width) for r in rows]
    used = sum(r.shape[1] for r in rows)
    return jnp.concatenate(rows + [jnp.zeros((L, n_rows - used, width), F32)], axis=1)


def _prepare(norm_mix_g, w_in, w_out, lru_conv_w, lru_conv_b, lru_w_r, lru_b_r, lru_w_i, lru_b_i, lru_lambda,
             ssd_conv_w, ssd_conv_b, ssd_dt_bias, ssd_a_log, ssd_d, ssd_norm_g,
             gdn_conv_w, gdn_dt_bias, gdn_a_log, gdn_norm_g,
             s5_a_re, s5_a_im, s5_log_dt, s5_b_re, s5_b_im, s5_c_re, s5_c_im, s5_d, s5_w_glu, s5_b_glu,
             norm_mem_g, norm_cross_g, w_cq, w_ck, w_cv, w_co,
             norm_ffn_g, w_ffn_gate, w_ffn_up, w_ffn_down, norm_final_g):
    L = DEPTH
    bf = MXU_DTYPE
    pk = {}
    pk["g_mix"] = norm_mix_g.reshape(L, 1, D_MODEL)
    pk["w_in"] = jnp.concatenate(
        [w_in[:, :, 0:1536], w_in[:, :, 1540:2564], w_in[:, :, 2572:2828], w_in[:, :, 1536:1540],
         w_in[:, :, 2564:2572], jnp.zeros((L, D_MODEL, U_COLS - IN_COLS), F32)], axis=2).astype(bf)
    pk["w_out"] = w_out.astype(bf)
    sp_lam = jax.nn.softplus(-lru_lambda)
    pk["v256"] = _pack_rows(
        [lru_conv_w, lru_conv_b, lru_b_r, lru_b_i, sp_lam, jnp.repeat(ssd_d, SSD_INNER // SSD_HEADS, axis=1),
         ssd_norm_g, jnp.tile(gdn_norm_g, (1, GDN_HEADS)), s5_d, s5_b_glu], 16)
    pk["v768"] = _pack_rows([ssd_conv_w, ssd_conv_b, gdn_conv_w], 16)
    zeros4 = jnp.zeros((L, 4), F32)
    bias = jnp.concatenate([ssd_dt_bias, zeros4, gdn_dt_bias, jnp.zeros((L, LANE - 12), F32)], axis=1)
    avec = jnp.concatenate([-jnp.exp(ssd_a_log.astype(F32)), zeros4, -jnp.exp(gdn_a_log),
                            jnp.zeros((L, LANE - 12), F32)], axis=1)
    pk["v128"] = _pack_rows([bias, avec], 8)
    pk["wri"] = jnp.concatenate([_block_diag(lru_w_r), _block_diag(lru_w_i)], axis=-1).astype(bf)
    a_re = s5_a_re.astype(F32)
    a_im = s5_a_im.astype(F32)
    step = jnp.exp(s5_log_dt.astype(F32))[:, :, None]
    mag = jnp.exp(a_re * step)
    ang = a_im * step
    lb_re, lb_im = mag * jnp.cos(ang), mag * jnp.sin(ang)
    den = a_re * a_re + a_im * a_im
    f_re = ((lb_re - 1.0) * a_re + lb_im * a_im) / den
    f_im = (lb_im * a_re - (lb_re - 1.0) * a_im) / den
    b_re = s5_b_re.astype(F32)
    b_im = s5_b_im.astype(F32)
    bb_re = f_re[..., None] * b_re - f_im[..., None] * b_im
    bb_im = f_re[..., None] * b_im + f_im[..., None] * b_re
    pk["v1024"] = _pack_rows([lb_re.reshape(L, S5_N), lb_im.reshape(L, S5_N)], 8)
    pk["bmat"] = jnp.concatenate([_block_diag(jnp.swapaxes(bb_re, 2, 3)), _block_diag(jnp.swapaxes(bb_im, 2, 3))],
                                 axis=-1).astype(bf)
    pk["cmat"] = jnp.concatenate([_block_diag(jnp.swapaxes(s5_c_re, 2, 3)), -_block_diag(jnp.swapaxes(s5_c_im, 2, 3))],
                                 axis=1).astype(bf)
    pk["wglu"] = s5_w_glu.astype(bf)
    pk["g_mem"] = norm_mem_g.reshape(L, 1, D_MODEL)
    pk["g_cross"] = norm_cross_g.reshape(L, 1, D_MODEL)
    pk["w_cq"], pk["w_ck"], pk["w_cv"], pk["w_co"] = (w.astype(bf) for w in (w_cq, w_ck, w_cv, w_co))
    pk["g_ffn"] = norm_ffn_g.reshape(L, 1, D_MODEL)
    pk["w_gate"], pk["w_up"], pk["w_down"] = (w.astype(bf) for w in (w_ffn_gate, w_ffn_up, w_ffn_down))
    pk["g_final"] = norm_final_g.reshape(1, D_MODEL)
    return pk


def _states_to_kernel(lru_h, lru_buf, ssd, ssd_buf, gdn, gdn_buf, s5_re, s5_im):
    B = lru_h.shape[0]
    eye = jnp.eye(GDN_HEADS, dtype=F32)
    gdn_bd = jnp.einsum('bhkv,hg->bhkgv', gdn, eye).reshape(B, GDN_WIDTH, GDN_WIDTH)
    return (lru_h.reshape(B, 1, LRU_WIDTH), lru_buf, ssd.reshape(B, SSD_GROUPS, LANE, SSD_STATE), ssd_buf,
            gdn_bd, gdn_buf, s5_re.reshape(B, 1, S5_N), s5_im.reshape(B, 1, S5_N))


def _states_from_kernel(st):
    lru_h, lru_buf, ssd, ssd_buf, gdn_bd, gdn_buf, s5_re, s5_im = st
    B = lru_h.shape[0]
    gdn = jnp.einsum('bhkhv->bhkv', gdn_bd.reshape(B, GDN_HEADS, GDN_HEAD_DIM, GDN_HEADS, GDN_HEAD_DIM))
    return (lru_h.reshape(B, LRU_WIDTH), lru_buf, ssd.reshape(B, SSD_HEADS, SSD_INNER // SSD_HEADS, SSD_STATE),
            ssd_buf, gdn, gdn_buf, s5_re.reshape(B, S5_GROUPS, S5_STATE), s5_im.reshape(B, S5_GROUPS, S5_STATE))


def _layer(l, x, mem_k, mem_v, lkv, states, pk, final):
    B, T, _ = x.shape
    if T >= MIXER_BLOCK:
        mix, new_states = _mixer(x, states, pk, l, True)
    else:
        u = _in_proj(x.reshape(B * T, D_MODEL), pk["g_mix"], pk["w_in"], l).reshape(B, T, U_COLS)
        mix, new_states = _mixer(u, states, pk, l, False)
    x = _attn(x, mix, mem_k, mem_v, lkv, pk, l)
    x = _ffn(x.reshape(B * T, D_MODEL), pk, l, final).reshape(B, T, D_MODEL)
    return x, new_states


def kernel(x_prompt, x_sample, mem_prompt, state_lru_h, cache_lru_conv, state_ssd, cache_ssd_conv, state_gdn, cache_gdn_conv, state_s5_re, state_s5_im, cache_mem_k, cache_mem_v, norm_mix_g, w_in, w_out, lru_conv_w, lru_conv_b, lru_w_r, lru_b_r, lru_w_i, lru_b_i, lru_lambda, ssd_conv_w, ssd_conv_b, ssd_dt_bias, ssd_a_log, ssd_d, ssd_norm_g, gdn_conv_w, gdn_dt_bias, gdn_a_log, gdn_norm_g, s5_a_re, s5_a_im, s5_log_dt, s5_b_re, s5_b_im, s5_c_re, s5_c_im, s5_d, s5_w_glu, s5_b_glu, norm_mem_g, norm_cross_g, w_cq, w_ck, w_cv, w_co, norm_ffn_g, w_ffn_gate, w_ffn_up, w_ffn_down, norm_final_g):
    pk = _prepare(norm_mix_g, w_in, w_out, lru_conv_w, lru_conv_b, lru_w_r, lru_b_r, lru_w_i, lru_b_i, lru_lambda,
                  ssd_conv_w, ssd_conv_b, ssd_dt_bias, ssd_a_log, ssd_d, ssd_norm_g,
                  gdn_conv_w, gdn_dt_bias, gdn_a_log, gdn_norm_g,
                  s5_a_re, s5_a_im, s5_log_dt, s5_b_re, s5_b_im, s5_c_re, s5_c_im, s5_d, s5_w_glu, s5_b_glu,
                  norm_mem_g, norm_cross_g, w_cq, w_ck, w_cv, w_co,
                  norm_ffn_g, w_ffn_gate, w_ffn_up, w_ffn_down, norm_final_g)
    bp = x_prompt.shape[0]
    bs = x_sample.shape[0]
    k1 = CONV_WIDTH - 1
    zero_states = _states_to_kernel(
        jnp.zeros((bp, LRU_WIDTH), F32), jnp.zeros((bp, k1, LRU_WIDTH), F32),
        jnp.zeros((bp, SSD_HEADS, SSD_INNER // SSD_HEADS, SSD_STATE), F32), jnp.zeros((bp, k1, SSD_CONV_DIM), F32),
        jnp.zeros((bp, GDN_HEADS, GDN_HEAD_DIM, GDN_HEAD_DIM), F32), jnp.zeros((bp, k1, GDN_CONV_DIM), F32),
        jnp.zeros((bp, S5_GROUPS, S5_STATE), F32), jnp.zeros((bp, S5_GROUPS, S5_STATE), F32))
    cache_k = cache_mem_k.reshape(DEPTH, bs, MEM_TOKENS, D_MODEL)
    cache_v = cache_mem_v.reshape(DEPTH, bs, MEM_TOKENS, D_MODEL)
    xp, xs = x_prompt, x_sample
    p_states, s_states, p_mk, p_mv = [], [], [], []
    for l in range(DEPTH):
        final = l == DEPTH - 1
        mk, mv = _memory_kv(mem_prompt, pk, l)
        xp, sp = _layer(l, xp, mk[None], mv[None], 0, zero_states, pk, final)
        s_in = _states_to_kernel(state_lru_h[l], cache_lru_conv[l], state_ssd[l], cache_ssd_conv[l],
                                 state_gdn[l], cache_gdn_conv[l], state_s5_re[l], state_s5_im[l])
        xs, ss = _layer(l, xs, cache_k, cache_v, l, s_in, pk, final)
        p_states.append(_states_from_kernel(sp))
        s_states.append(_states_from_kernel(ss))
        p_mk.append(mk.reshape(bp, MEM_TOKENS, MEM_HEADS, MEM_HEAD_DIM))
        p_mv.append(mv.reshape(bp, MEM_TOKENS, MEM_HEADS, MEM_HEAD_DIM))

    def stack(states, j):
        return jnp.stack([st[j] for st in states], axis=0)

    return ((xp, xs) + tuple(stack(p_states, j) for j in range(8)) + (jnp.stack(p_mk, axis=0), jnp.stack(p_mv, axis=0))
            + tuple(stack(s_states, j) for j in range(8)))
```

```python
import functools
import math

import jax
import jax.numpy as jnp
import numpy as np
from jax import lax
from jax.experimental import pallas as pl
from jax.experimental.pallas import tpu as pltpu

F32 = jnp.float32
MXU_DTYPE = jnp.bfloat16

D_MODEL = 1024
DEPTH = 4
CONV_WIDTH = 4
RMS_EPS = 1e-6
LRU_WIDTH = 256
LRU_BLOCKS = 4
LRU_C = 8.0
SSD_INNER = 256
SSD_HEADS = 4
SSD_GROUPS = 2
SSD_STATE = 128
SSD_CONV_DIM = 768
GDN_WIDTH = 256
GDN_HEAD_DIM = 64
GDN_HEADS = 4
GDN_CONV_DIM = 768
S5_WIDTH = 256
S5_GROUP_CH = 16
S5_GROUPS = 16
S5_STATE = 64
S5_N = S5_GROUPS * S5_STATE
MEM_TOKENS = 256
MEM_HEADS = 4
MEM_HEAD_DIM = 256
FFN_HIDDEN = 2816
IN_COLS = 2828

U_GATE, U_LRUX, U_SSDZ, U_XBC, U_QKV, U_GDNZ, U_S5, U_SMALL = 0, 256, 512, 768, 1536, 2304, 2560, 2816
U_COLS = 2944
SM_DT, SM_BETA, SM_A = 0, 4, 8

LANE = 128
SUBLANE = 8
VMEM_LIMIT = 56 * 1024 * 1024
HIST = SUBLANE
ROW_TILE = 512
MIXER_BLOCK = 256
SHORT_SEQ_BATCH = 8


def _mm(a, b):
    return jnp.dot(a.astype(MXU_DTYPE), b.astype(MXU_DTYPE), preferred_element_type=F32)


def _mm_nt(a, b):
    return lax.dot_general(a.astype(MXU_DTYPE), b.astype(MXU_DTYPE), (((1,), (1,)), ((), ())),
                           preferred_element_type=F32)


def _mm_tn(a, b):
    return lax.dot_general(a.astype(MXU_DTYPE), b.astype(MXU_DTYPE), (((0,), (0,)), ((), ())),
                           preferred_element_type=F32)


def _rms(x, g):
    ms = jnp.mean(x * x, axis=-1, keepdims=True)
    return x * lax.rsqrt(ms + RMS_EPS) * g


def _silu(x):
    return x * jax.nn.sigmoid(x)


def _gelu(x):
    return jax.nn.gelu(x, approximate=True)


def _shift_rows(x, k, row, fill):
    return jnp.where(row >= k, pltpu.roll(x, k, 0), fill)


def _in_proj_kernel(x_ref, g_ref, w_ref, u_ref):
    h = _rms(x_ref[...], g_ref[...])
    u_ref[...] = _mm(h, w_ref[...])


def _in_proj(x2d, g_all, w_all, l):
    R = x2d.shape[0]
    TM = min(R, MIXER_BLOCK)
    return pl.pallas_call(
        _in_proj_kernel,
        grid=(R // TM,),
        in_specs=[pl.BlockSpec((TM, D_MODEL), lambda i: (i, 0)),
                  pl.BlockSpec((None, 1, D_MODEL), lambda i: (l, 0, 0)),
                  pl.BlockSpec((None, D_MODEL, U_COLS), lambda i: (l, 0, 0))],
        out_specs=pl.BlockSpec((TM, U_COLS), lambda i: (i, 0)),
        out_shape=jax.ShapeDtypeStruct((R, U_COLS), F32),
        compiler_params=pltpu.CompilerParams(dimension_semantics=("parallel",), vmem_limit_bytes=VMEM_LIMIT),
        name="in_proj",
    )(x2d, g_all, w_all)


def _conv_block(xp_ref, x, w_ref, w_row0, TB):
    xp_ref[pl.ds(HIST, TB), :] = x
    y = x * w_ref[w_row0 + CONV_WIDTH - 1:w_row0 + CONV_WIDTH, :]
    for j in range(CONV_WIDTH - 1):
        off = HIST - (CONV_WIDTH - 1) + j
        y = y + xp_ref[pl.ds(off, TB), :] * w_ref[w_row0 + j:w_row0 + j + 1, :]
    return y


def _interleave(streams):
    streams = list(streams)
    while streams:
        for s in list(streams):
            try:
                next(s)
            except StopIteration:
                streams.remove(s)


def _mixer_kernel(TB, QG, NT, NB, fused, *refs):
    n_in = (3 if fused else 1) + 8
    n_par = 8
    n_out = 9
    ins, params = refs[:n_in], refs[n_in:n_in + n_par]
    outs = refs[n_in + n_par:n_in + n_par + n_out]
    scratch = refs[n_in + n_par + n_out:]
    streams, finals = [], []
    for bb in range(NB):
        if fused:
            x_ref, g_ref, win_ref = ins[:3]
            hb = _rms(x_ref[bb], g_ref[...]).astype(MXU_DTYPE)
            states_in = ins[3:]

            def useg(start, width, hb=hb, win_ref=win_ref):
                return _mm(hb, win_ref[:, start:start + width])
        else:
            u_ref = ins[0].at[bb]
            states_in = ins[1:]

            def useg(start, width, u_ref=u_ref):
                return u_ref[:, start:start + width]

        seq_streams, finish = _mixer_sequence(TB, QG, NT, useg, *[r.at[bb] for r in states_in], *params,
                                              *[r.at[bb] for r in outs], *[r.at[bb] for r in scratch])
        streams.append(seq_streams)
        finals.append(finish)
    _interleave([seq[j] for j in range(4) for seq in streams])
    for finish in finals:
        finish()


def _mixer_sequence(TB, QG, NT, useg,
                  lru_h0, lru_b0, ssd_s0, ssd_b0, gdn_s0, gdn_b0, s5r0, s5i0,
                  v256, v768, v128, v1024, wri, bmat, cmat, wglu,
                  mix_ref, lru_h_o, lru_b_o, ssd_s_o, ssd_b_o, gdn_s_o, gdn_b_o, s5r_o, s5i_o,
                  xp_lru, xp_ssd, xp_gdn, h_lru, s_ssd, s_gdn, x_s5, hs_scr, x5_scr):
    t = pl.program_id(1)
    K1 = CONV_WIDTH - 1
    NG = TB // SUBLANE

    @pl.when(t == 0)
    def _init():
        for xp, b0 in ((xp_lru, lru_b0), (xp_ssd, ssd_b0), (xp_gdn, gdn_b0)):
            xp[pl.ds(0, HIST), :] = jnp.zeros((HIST, xp.shape[1]), F32)
            xp[pl.ds(HIST - K1, K1), :] = b0[...]
        h_lru[...] = jnp.zeros(h_lru.shape, F32)
        h_lru[0:1, :] = lru_h0[...]
        s_ssd[...] = ssd_s0[...]
        s_gdn[...] = gdn_s0[...]
        x_s5[...] = jnp.zeros(x_s5.shape, F32)
        x_s5[0, SUBLANE - 1:SUBLANE, :] = s5r0[...]
        x_s5[1, SUBLANE - 1:SUBLANE, :] = s5i0[...]

    row128 = lax.broadcasted_iota(jnp.int32, (TB, LANE), 0)
    lane128 = lax.broadcasted_iota(jnp.int32, (TB, LANE), 1)
    small = useg(U_SMALL, LANE)
    sp = jax.nn.softplus(small + v128[0:1, :])
    sig = jax.nn.sigmoid(small)
    gcum = sp * v128[1:2, :]
    rk = jnp.where(lane128 < SM_BETA, row128, row128 & (QG - 1))
    k = 1
    while k < TB:
        gcum = gcum + jnp.where(rk >= k, pltpu.roll(gcum, k, 0), 0.0)
        k *= 2
    if TB < LANE:
        gcum_t = jnp.concatenate([gcum, jnp.zeros((LANE - TB, LANE), F32)], axis=0).T
    else:
        gcum_t = gcum.T
    rr = lax.broadcasted_iota(jnp.int32, (TB, TB), 0)
    cc = lax.broadcasted_iota(jnp.int32, (TB, TB), 1)

    def lru():
        u_lru = useg(U_GATE, 2 * LRU_WIDTH)
        gate = u_lru[:, 0:LRU_WIDTH]
        xc = _conv_block(xp_lru, u_lru[:, LRU_WIDTH:2 * LRU_WIDTH], v256, 0, TB) + v256[4:5, :]
        ri = _mm(xc, wri[...])
        yield
        r = jax.nn.sigmoid(ri[:, :LRU_WIDTH] + v256[5:6, :])
        ig = jax.nn.sigmoid(ri[:, LRU_WIDTH:] + v256[6:7, :])
        log_a = (-LRU_C) * r * v256[7:8, :]
        a = jnp.exp(log_a)
        bt = jnp.sqrt(-jnp.tanh(log_a) * (a * a + 1.0)) * (ig * xc)
        yield
        row8 = lax.broadcasted_iota(jnp.int32, (SUBLANE, LRU_WIDTH), 0)
        h_prev = h_lru[0:1, :]
        for i in range(NG):
            a_g = a[i * SUBLANE:(i + 1) * SUBLANE]
            b_g = bt[i * SUBLANE:(i + 1) * SUBLANE]
            k = 1
            while k < SUBLANE:
                b_g = b_g + a_g * _shift_rows(b_g, k, row8, 0.0)
                a_g = a_g * _shift_rows(a_g, k, row8, 1.0)
                k *= 2
            h_g = b_g + a_g * h_prev
            h_prev = h_g[SUBLANE - 1:SUBLANE, :]
            hs_scr[pl.ds(i * SUBLANE, SUBLANE), :] = h_g
            if i % 4 == 3:
                yield
        h_lru[0:1, :] = h_prev
        mix_ref[:, 0:LRU_WIDTH] = hs_scr[...] * _gelu(gate)

    def ssd():
        u_ssd = useg(U_SSDZ, SSD_INNER + SSD_CONV_DIM)
        z = u_ssd[:, 0:SSD_INNER]
        xbc = _silu(_conv_block(xp_ssd, u_ssd[:, SSD_INNER:SSD_INNER + SSD_CONV_DIM], v768, 0, TB) + v768[4:5, :])
        yield
        causal = rr >= cc
        lo = lane128 < (LANE // 2)
        row_s = lax.broadcasted_iota(jnp.int32, (LANE, SSD_STATE), 0) < (LANE // 2)
        y_pairs = []
        for p in range(SSD_GROUPS):
            xs_p = xbc[:, p * LANE:(p + 1) * LANE]
            b_p = xbc[:, SSD_INNER + p * SSD_STATE:SSD_INNER + (p + 1) * SSD_STATE]
            c_p = xbc[:, SSD_INNER + SSD_GROUPS * SSD_STATE + p * SSD_STATE:
                      SSD_INNER + SSD_GROUPS * SSD_STATE + (p + 1) * SSD_STATE]
            cb = _mm_nt(c_p, b_p)
            h0, h1 = 2 * p, 2 * p + 1
            xdt = xs_p * jnp.where(lo, sp[:, SM_DT + h0:SM_DT + h0 + 1], sp[:, SM_DT + h1:SM_DT + h1 + 1])
            s_pair = s_ssd[p]
            ys = []
            for h in (h0, h1):
                gcol = gcum[:, SM_DT + h:SM_DT + h + 1]
                grow = gcum_t[SM_DT + h:SM_DT + h + 1, 0:TB]
                seg = jnp.exp(jnp.where(causal, gcol - grow, -jnp.inf))
                y_diag = _mm(cb * seg, xdt)
                y_off = _mm_nt(c_p * jnp.exp(gcol), s_pair)
                ys.append(y_diag + y_off)
                yield
            y_pairs.append(jnp.where(lo, ys[0], ys[1]))
            g0 = gcum[:, SM_DT + h0:SM_DT + h0 + 1]
            g1 = gcum[:, SM_DT + h1:SM_DT + h1 + 1]
            gl0 = g0[TB - 1:TB, :]
            gl1 = g1[TB - 1:TB, :]
            to_end = jnp.where(lo, jnp.exp(gl0 - g0), jnp.exp(gl1 - g1))
            s_ssd[p] = jnp.where(row_s, jnp.exp(gl0), jnp.exp(gl1)) * s_pair + _mm_tn(xdt * to_end, b_p)
            yield
        y = jnp.concatenate(y_pairs, axis=-1) + v256[8:9, :] * xbc[:, 0:SSD_INNER]
        yg = y * _silu(z)
        gs = SSD_INNER // SSD_GROUPS
        for p in range(SSD_GROUPS):
            mix_ref[:, LRU_WIDTH + p * gs:LRU_WIDTH + (p + 1) * gs] = _rms(
                yg[:, p * gs:(p + 1) * gs], v256[9:10, p * gs:(p + 1) * gs])

    def gdn():
        lane = lax.broadcasted_iota(jnp.int32, (TB, GDN_WIDTH), 1)
        hmask = [(lane >= h * GDN_HEAD_DIM) & (lane < (h + 1) * GDN_HEAD_DIM) for h in range(GDN_HEADS)]

        def by_head(vals):
            return jnp.where(hmask[0], vals[0], jnp.where(hmask[1], vals[1], jnp.where(hmask[2], vals[2], vals[3])))

        def head_sum(x):
            return by_head([jnp.sum(jnp.where(m, x, 0.0), axis=-1, keepdims=True) for m in hmask])

        u_gdn = useg(U_QKV, GDN_CONV_DIM + GDN_WIDTH)
        qkv = _silu(_conv_block(xp_gdn, u_gdn[:, 0:GDN_CONV_DIM], v768, 5, TB))
        qf = qkv[:, 0:GDN_WIDTH]
        kf = qkv[:, GDN_WIDTH:2 * GDN_WIDTH]
        vf = qkv[:, 2 * GDN_WIDTH:3 * GDN_WIDTH]
        qf = qf * lax.rsqrt(head_sum(qf * qf) + 1e-6) * (GDN_HEAD_DIM ** -0.5)
        kf = kf * lax.rsqrt(head_sum(kf * kf) + 1e-6)
        beta_f = by_head([sig[:, SM_BETA + h:SM_BETA + h + 1] for h in range(GDN_HEADS)])
        gc_f = by_head([gcum[:, SM_A + h:SM_A + h + 1] for h in range(GDN_HEADS)])
        chunk_shift = int(math.log2(QG))
        same_chunk = lax.shift_right_logical(rr, chunk_shift) == lax.shift_right_logical(cc, chunk_shift)
        incl = same_chunk & (rr >= cc)
        strict = same_chunk & (rr > cc)
        eg = jnp.exp(gc_f)
        kq = _mm_nt(jnp.concatenate([jnp.where(m, kf, 0.0) for m in hmask] + [jnp.where(m, qf, 0.0) for m in hmask],
                                    axis=0), kf)
        rhs = jnp.concatenate([beta_f * vf, beta_f * eg * kf], axis=1)
        yield
        decs, pws, rs = [], [], []
        for h in range(GDN_HEADS):
            gcol = gcum[:, SM_A + h:SM_A + h + 1]
            grow = gcum_t[SM_A + h:SM_A + h + 1, 0:TB]
            dec = jnp.exp(jnp.where(incl, gcol - grow, -jnp.inf))
            low = jnp.where(strict, sig[:, SM_BETA + h:SM_BETA + h + 1] * kq[h * TB:(h + 1) * TB] * dec, 0.0)
            decs.append(dec)
            pws.append(-low)
            rs.append(-low)
        yield
        for _ in range(chunk_shift - 1):
            for h in range(GDN_HEADS):
                pws[h] = _mm(pws[h], pws[h])
                rs[h] = rs[h] + pws[h] + _mm(rs[h], pws[h])
                yield
        sols, qks = [], []
        for h in range(GDN_HEADS):
            sols.append(rhs + _mm(rs[h], rhs))
            qks.append(kq[(GDN_HEADS + h) * TB:(GDN_HEADS + h + 1) * TB] * decs[h])
            yield
        u_all = by_head([s[:, 0:GDN_WIDTH] for s in sols])
        w_all = by_head([s[:, GDN_WIDTH:2 * GDN_WIDTH] for s in sols])
        q_dec = qf * eg
        head_shift = int(math.log2(GDN_HEAD_DIM))
        r2 = lax.shift_right_logical(lax.broadcasted_iota(jnp.int32, (GDN_WIDTH, GDN_WIDTH), 0), head_shift)
        c2 = lax.shift_right_logical(lax.broadcasted_iota(jnp.int32, (GDN_WIDTH, GDN_WIDTH), 1), head_shift)
        blockdiag = r2 == c2
        s_bd = s_gdn[...]
        deltas, o_state = [], []
        for c in range(TB // QG):
            sl = slice(c * QG, (c + 1) * QG)
            ws = _mm(jnp.concatenate([w_all[sl], q_dec[sl]], axis=0), s_bd)
            delta = u_all[sl] - ws[0:QG]
            o_state.append(ws[QG:2 * QG])
            yield
            g_c = gc_f[sl]
            g_last = g_c[QG - 1:QG, :]
            s_bd = s_bd * jnp.exp(g_last) + jnp.where(blockdiag, _mm_tn(kf[sl] * jnp.exp(g_last - g_c), delta), 0.0)
            deltas.append(delta)
            yield
        s_gdn[...] = s_bd
        delta_all = jnp.concatenate(deltas, axis=0)
        o = jnp.concatenate(o_state, axis=0) + by_head([_mm(qks[h], delta_all) for h in range(GDN_HEADS)])
        yield
        ms = head_sum(o * o) * (1.0 / GDN_HEAD_DIM)
        o = o * lax.rsqrt(ms + RMS_EPS) * v256[10:11, :]
        zg = u_gdn[:, GDN_CONV_DIM:GDN_CONV_DIM + GDN_WIDTH]
        mix_ref[:, 2 * LRU_WIDTH:2 * LRU_WIDTH + GDN_WIDTH] = o * _silu(zg)

    def s5():
        row_n = lax.broadcasted_iota(jnp.int32, (SUBLANE, S5_N), 0)
        us5 = useg(U_S5, S5_WIDTH)
        bu = _mm(us5, bmat[...])
        yield
        lam_r, lam_i = v1024[0:1, :], v1024[1:2, :]
        pows = []
        pr, pi = lam_r, lam_i
        k = 1
        while k < SUBLANE:
            pows.append((k, jnp.where(row_n >= k, pr, 0.0), jnp.where(row_n >= k, pi, 0.0)))
            pr, pi = pr * pr - pi * pi, 2.0 * (pr * pi)
            k *= 2
        in_r = jnp.where(row_n == 0, lam_r, 0.0)
        in_i = jnp.where(row_n == 0, lam_i, 0.0)
        g_r = x_s5[0]
        g_i = x_s5[1]
        for i in range(NG):
            p_r = pltpu.roll(g_r, 1, 0)
            p_i = pltpu.roll(g_i, 1, 0)
            g_r = bu[i * SUBLANE:(i + 1) * SUBLANE, 0:S5_N] + (in_r * p_r - in_i * p_i)
            g_i = bu[i * SUBLANE:(i + 1) * SUBLANE, S5_N:2 * S5_N] + (in_r * p_i + in_i * p_r)
            for k, pr, pi in pows:
                s_r = pltpu.roll(g_r, k, 0)
                s_i = pltpu.roll(g_i, k, 0)
                g_r, g_i = g_r + (pr * s_r - pi * s_i), g_i + (pr * s_i + pi * s_r)
            x5_scr[pl.ds(i * SUBLANE, SUBLANE), 0:S5_N] = g_r
            x5_scr[pl.ds(i * SUBLANE, SUBLANE), S5_N:2 * S5_N] = g_i
            yield
        x_s5[0] = g_r
        x_s5[1] = g_i
        y5 = _mm(x5_scr[...], cmat[...])
        y5 = _gelu(y5 + v256[11:12, :] * us5)
        yield
        mix_ref[:, 3 * LRU_WIDTH:3 * LRU_WIDTH + S5_WIDTH] = y5 * jax.nn.sigmoid(_mm(y5, wglu[...]) + v256[12:13, :])

    def finish():
        for xp in (xp_lru, xp_ssd, xp_gdn):
            xp[pl.ds(0, HIST), :] = xp[pl.ds(TB, HIST), :]

        pl.when(t == NT - 1)(_emit)

    def _emit():
        lru_h_o[...] = h_lru[0:1, :]
        lru_b_o[...] = xp_lru[pl.ds(HIST - K1, K1), :]
        ssd_b_o[...] = xp_ssd[pl.ds(HIST - K1, K1), :]
        gdn_b_o[...] = xp_gdn[pl.ds(HIST - K1, K1), :]
        ssd_s_o[...] = s_ssd[...]
        gdn_s_o[...] = s_gdn[...]
        s5r_o[...] = x_s5[0, SUBLANE - 1:SUBLANE, :]
        s5i_o[...] = x_s5[1, SUBLANE - 1:SUBLANE, :]

    return [gdn(), s5(), ssd(), lru()], finish


def _mixer(src, states, ls, pk, l, fused):
    B, T, _ = src.shape
    TB = min(T, MIXER_BLOCK)
    QG = min(TB, 64)
    NT = T // TB
    NB = SHORT_SEQ_BATCH if (B % SHORT_SEQ_BATCH == 0 and TB < MIXER_BLOCK) else 1

    def per_b(shape):
        nd = len(shape)
        return pl.BlockSpec((NB,) + tuple(shape[1:]), lambda b, t: (b,) + (0,) * (nd - 1))

    def per_b_of_layer(shape):
        nd = len(shape)
        return pl.BlockSpec((None, NB) + tuple(shape[2:]), lambda b, t: (ls, b) + (0,) * (nd - 2))

    def per_l(arr):
        nd = arr.ndim
        return pl.BlockSpec((None,) + tuple(arr.shape[1:]), lambda b, t: (l,) + (0,) * (nd - 1))

    proj = (pk["g_mix"], pk["w_in"]) if fused else ()
    params = (pk["v256"], pk["v768"], pk["v128"], pk["v1024"], pk["wri"], pk["bmat"], pk["cmat"], pk["wglu"])
    state_shapes = [s.shape[1:] for s in states]
    out_shape = [jax.ShapeDtypeStruct((B, T, D_MODEL), F32)] + [jax.ShapeDtypeStruct(s, F32) for s in state_shapes]
    outs = pl.pallas_call(
        functools.partial(_mixer_kernel, TB, QG, NT, NB, fused),
        grid=(B // NB, NT),
        in_specs=[pl.BlockSpec((NB, TB, src.shape[2]), lambda b, t: (b, t, 0))] + [per_l(p) for p in proj]
                 + [per_b_of_layer(s.shape) for s in states] + [per_l(p) for p in params],
        out_specs=[pl.BlockSpec((NB, TB, D_MODEL), lambda b, t: (b, t, 0))] + [per_b(s) for s in state_shapes],
        out_shape=out_shape,
        scratch_shapes=[pltpu.VMEM((NB, TB + HIST, LRU_WIDTH), F32),
                        pltpu.VMEM((NB, TB + HIST, SSD_CONV_DIM), F32),
                        pltpu.VMEM((NB, TB + HIST, GDN_CONV_DIM), F32),
                        pltpu.VMEM((NB, SUBLANE, LRU_WIDTH), F32),
                        pltpu.VMEM((NB, SSD_GROUPS, LANE, SSD_STATE), F32),
                        pltpu.VMEM((NB, GDN_WIDTH, GDN_WIDTH), F32),
                        pltpu.VMEM((NB, 2, SUBLANE, S5_N), F32),
                        pltpu.VMEM((NB, TB, LRU_WIDTH), F32),
                        pltpu.VMEM((NB, TB, 2 * S5_N), F32)],
        compiler_params=pltpu.CompilerParams(dimension_semantics=("arbitrary", "arbitrary"),
                                             vmem_limit_bytes=VMEM_LIMIT),
        name="mixer",
    )(src, *proj, *states, *params)
    return outs[0], tuple(outs[1:])


def _attend(q, k_ref, v_ref):
    heads = []
    for h in range(MEM_HEADS):
        sl = slice(h * MEM_HEAD_DIM, (h + 1) * MEM_HEAD_DIM)
        s = _mm_nt(q[:, sl], k_ref[:, sl]) * (MEM_HEAD_DIM ** -0.5)
        e = jnp.exp(s - jnp.max(s, axis=-1, keepdims=True))
        p = e / jnp.sum(e, axis=-1, keepdims=True)
        heads.append(_mm(p, v_ref[:, sl]))
    return jnp.concatenate(heads, axis=-1)


def _attn_kernel(x_ref, mix_ref, wout, g_ref, wq, k_ref, v_ref, wo, o_ref):
    x1 = x_ref[0] + _mm(mix_ref[0], wout[...])
    q = _mm(_rms(x1, g_ref[...]), wq[...])
    o_ref[0] = x1 + _mm(_attend(q, k_ref, v_ref), wo[...])


def _attn_short_kernel(T, x_ref, mix_ref, wout, g_ref, wq, k_ref, v_ref, wo, o_ref, x1_scr, q_scr, a_scr):
    b = pl.program_id(0)

    @pl.when(b == 0)
    def _project_in():
        x1 = x_ref[...] + _mm(mix_ref[...], wout[...])
        x1_scr[...] = x1
        q_scr[...] = _mm(_rms(x1, g_ref[...]), wq[...])

    rows = pl.ds(pl.multiple_of(b * T, T), T)
    a_scr[rows, :] = _attend(q_scr[rows, :], k_ref, v_ref)

    @pl.when(b == pl.num_programs(0) - 1)
    def _project_out():
        o_ref[...] = x1_scr[...] + _mm(a_scr[...], wo[...])


def _attn(x, mix, mem_k, mem_v, lkv, pk, l):
    B, T, _ = x.shape

    def per_l(arr):
        nd = arr.ndim
        return pl.BlockSpec((None,) + tuple(arr.shape[1:]), lambda *i: (l,) + (0,) * (nd - 1),
                            pipeline_mode=pl.Buffered(1))

    weights = (pk["w_out"], pk["g_cross"], pk["w_cq"])
    if T >= MIXER_BLOCK:
        TB = min(T, ROW_TILE)
        blk = pl.BlockSpec((1, TB, D_MODEL), lambda b, t: (b, t, 0))
        kv = pl.BlockSpec((None, None, MEM_TOKENS, D_MODEL), lambda b, t: (lkv, b, 0, 0))
        return pl.pallas_call(
            _attn_kernel,
            grid=(B, T // TB),
            in_specs=[blk, blk] + [per_l(w) for w in weights] + [kv, kv, per_l(pk["w_co"])],
            out_specs=blk,
            out_shape=jax.ShapeDtypeStruct((B, T, D_MODEL), F32),
            compiler_params=pltpu.CompilerParams(dimension_semantics=("parallel", "parallel"),
                                                 vmem_limit_bytes=VMEM_LIMIT),
            name="attn",
        )(x, mix, *weights, mem_k, mem_v, pk["w_co"])
    R = B * T
    rows = pl.BlockSpec((R, D_MODEL), lambda b: (0, 0))
    kv = pl.BlockSpec((None, None, MEM_TOKENS, D_MODEL), lambda b: (lkv, b, 0, 0))
    out = pl.pallas_call(
        functools.partial(_attn_short_kernel, T),
        grid=(B,),
        in_specs=[rows, rows] + [per_l(w) for w in weights] + [kv, kv, per_l(pk["w_co"])],
        out_specs=rows,
        out_shape=jax.ShapeDtypeStruct((R, D_MODEL), F32),
        scratch_shapes=[pltpu.VMEM((R, D_MODEL), F32)] * 3,
        compiler_params=pltpu.CompilerParams(dimension_semantics=("arbitrary",), vmem_limit_bytes=VMEM_LIMIT),
        name="attn_short",
    )(x.reshape(R, D_MODEL), mix.reshape(R, D_MODEL), *weights, mem_k, mem_v, pk["w_co"])
    return out.reshape(B, T, D_MODEL)


def _ffn_kernel(final, x_ref, g_ref, wg, wu, wd, gf_ref, o_ref):
    x = x_ref[...]
    h = _rms(x, g_ref[...]).astype(MXU_DTYPE)
    y = x + _mm(_silu(_mm(h, wg[...])) * _mm(h, wu[...]), wd[...])
    if final:
        y = _rms(y, gf_ref[...])
    o_ref[...] = y


def _ffn(x2d, pk, l, final):
    R = x2d.shape[0]
    TM = min(R, ROW_TILE)

    def per_l(arr):
        nd = arr.ndim
        return pl.BlockSpec((None,) + tuple(arr.shape[1:]), lambda i: (l,) + (0,) * (nd - 1),
                            pipeline_mode=pl.Buffered(1))

    blk = pl.BlockSpec((TM, D_MODEL), lambda i: (i, 0))
    return pl.pallas_call(
        functools.partial(_ffn_kernel, final),
        grid=(R // TM,),
        in_specs=[blk, per_l(pk["g_ffn"]), per_l(pk["w_gate"]), per_l(pk["w_up"]), per_l(pk["w_down"]),
                  pl.BlockSpec((1, D_MODEL), lambda i: (0, 0))],
        out_specs=blk,
        out_shape=jax.ShapeDtypeStruct((R, D_MODEL), F32),
        compiler_params=pltpu.CompilerParams(dimension_semantics=("parallel",), vmem_limit_bytes=VMEM_LIMIT),
        name="ffn",
    )(x2d, pk["g_ffn"], pk["w_gate"], pk["w_up"], pk["w_down"], pk["g_final"])


def _memkv_kernel(m_ref, g_ref, wk, wv, k_ref, v_ref):
    h = _rms(m_ref[0], g_ref[...]).astype(MXU_DTYPE)
    k_ref[0] = _mm(h, wk[...])
    v_ref[0] = _mm(h, wv[...])


def _memory_kv(mem, pk, l):
    B = mem.shape[0]

    def per_l(arr):
        nd = arr.ndim
        return pl.BlockSpec((None,) + tuple(arr.shape[1:]), lambda b: (l,) + (0,) * (nd - 1))

    blk = pl.BlockSpec((1, MEM_TOKENS, D_MODEL), lambda b: (b, 0, 0))
    return pl.pallas_call(
        _memkv_kernel,
        grid=(B,),
        in_specs=[blk, per_l(pk["g_mem"]), per_l(pk["w_ck"]), per_l(pk["w_cv"])],
        out_specs=[blk, blk],
        out_shape=[jax.ShapeDtypeStruct((B, MEM_TOKENS, D_MODEL), F32)] * 2,
        compiler_params=pltpu.CompilerParams(dimension_semantics=("parallel",), vmem_limit_bytes=VMEM_LIMIT),
        name="memory_kv",
    )(mem, pk["g_mem"], pk["w_ck"], pk["w_cv"])


def _block_diag(w):
    L, n, a, b = w.shape
    return jnp.einsum('lnab,nm->lnamb', w, jnp.eye(n, dtype=w.dtype)).reshape(L, n * a, n * b)


def _pack_rows(rows, n_rows):
    L, width = rows[0].shape[0], rows[0].shape[-1]
    rows = [r.reshape(L, -1, width) for r in rows]
    used = sum(r.shape[1] for r in rows)
    return jnp.concatenate(rows + [jnp.zeros((L, n_rows - used, width), F32)], axis=1)


def _prepare(norm_mix_g, w_in, w_out, lru_conv_w, lru_conv_b, lru_w_r, lru_b_r, lru_w_i, lru_b_i, lru_lambda,
             ssd_conv_w, ssd_conv_b, ssd_dt_bias, ssd_a_log, ssd_d, ssd_norm_g,
             gdn_conv_w, gdn_dt_bias, gdn_a_log, gdn_norm_g,
             s5_a_re, s5_a_im, s5_log_dt, s5_b_re, s5_b_im, s5_c_re, s5_c_im, s5_d, s5_w_glu, s5_b_glu,
             norm_mem_g, norm_cross_g, w_cq, w_ck, w_cv, w_co,
             norm_ffn_g, w_ffn_gate, w_ffn_up, w_ffn_down, norm_final_g):
    L = DEPTH
    bf = MXU_DTYPE
    pk = {}
    pk["g_mix"] = norm_mix_g.reshape(L, 1, D_MODEL)
    pk["w_in"] = jnp.concatenate(
        [w_in[:, :, 0:1536], w_in[:, :, 1540:2564], w_in[:, :, 2572:2828], w_in[:, :, 1536:1540],
         w_in[:, :, 2564:2572], jnp.zeros((L, D_MODEL, U_COLS - IN_COLS), F32)], axis=2).astype(bf)
    pk["w_out"] = w_out.astype(bf)
    sp_lam = jax.nn.softplus(-lru_lambda)
    pk["v256"] = _pack_rows(
        [lru_conv_w, lru_conv_b, lru_b_r, lru_b_i, sp_lam, jnp.repeat(ssd_d, SSD_INNER // SSD_HEADS, axis=1),
         ssd_norm_g, jnp.tile(gdn_norm_g, (1, GDN_HEADS)), s5_d, s5_b_glu], 16)
    pk["v768"] = _pack_rows([ssd_conv_w, ssd_conv_b, gdn_conv_w], 16)
    zeros4 = jnp.zeros((L, 4), F32)
    bias = jnp.concatenate([ssd_dt_bias, zeros4, gdn_dt_bias, jnp.zeros((L, LANE - 12), F32)], axis=1)
    avec = jnp.concatenate([-jnp.exp(ssd_a_log.astype(F32)), zeros4, -jnp.exp(gdn_a_log),
                            jnp.zeros((L, LANE - 12), F32)], axis=1)
    pk["v128"] = _pack_rows([bias, avec], 8)
    pk["wri"] = jnp.concatenate([_block_diag(lru_w_r), _block_diag(lru_w_i)], axis=-1).astype(bf)
    a_re = s5_a_re.astype(F32)
    a_im = s5_a_im.astype(F32)
    step = jnp.exp(s5_log_dt.astype(F32))[:, :, None]
    mag = jnp.exp(a_re * step)
    ang = a_im * step
    lb_re, lb_im = mag * jnp.cos(ang), mag * jnp.sin(ang)
    den = a_re * a_re + a_im * a_im
    f_re = ((lb_re - 1.0) * a_re + lb_im * a_im) / den
    f_im = (lb_im * a_re - (lb_re - 1.0) * a_im) / den
    b_re = s5_b_re.astype(F32)
    b_im = s5_b_im.astype(F32)
    bb_re = f_re[..., None] * b_re - f_im[..., None] * b_im
    bb_im = f_re[..., None] * b_im + f_im[..., None] * b_re
    pk["v1024"] = _pack_rows([lb_re.reshape(L, S5_N), lb_im.reshape(L, S5_N)], 8)
    pk["bmat"] = jnp.concatenate([_block_diag(jnp.swapaxes(bb_re, 2, 3)), _block_diag(jnp.swapaxes(bb_im, 2, 3))],
                                 axis=-1).astype(bf)
    pk["cmat"] = jnp.concatenate([_block_diag(jnp.swapaxes(s5_c_re, 2, 3)), -_block_diag(jnp.swapaxes(s5_c_im, 2, 3))],
                                 axis=1).astype(bf)
    pk["wglu"] = s5_w_glu.astype(bf)
    pk["g_mem"] = norm_mem_g.reshape(L, 1, D_MODEL)
    pk["g_cross"] = norm_cross_g.reshape(L, 1, D_MODEL)
    pk["w_cq"], pk["w_ck"], pk["w_cv"], pk["w_co"] = (w.astype(bf) for w in (w_cq, w_ck, w_cv, w_co))
    pk["g_ffn"] = norm_ffn_g.reshape(L, 1, D_MODEL)
    pk["w_gate"], pk["w_up"], pk["w_down"] = (w.astype(bf) for w in (w_ffn_gate, w_ffn_up, w_ffn_down))
    pk["g_final"] = norm_final_g.reshape(1, D_MODEL)
    return pk


def _states_to_kernel(lru_h, lru_buf, ssd, ssd_buf, gdn, gdn_buf, s5_re, s5_im):
    L, B = lru_h.shape[:2]
    eye = jnp.eye(GDN_HEADS, dtype=F32)
    gdn_bd = jnp.einsum('lbhkv,hg->lbhkgv', gdn, eye).reshape(L, B, GDN_WIDTH, GDN_WIDTH)
    return (lru_h.reshape(L, B, 1, LRU_WIDTH), lru_buf, ssd.reshape(L, B, SSD_GROUPS, LANE, SSD_STATE), ssd_buf,
            gdn_bd, gdn_buf, s5_re.reshape(L, B, 1, S5_N), s5_im.reshape(L, B, 1, S5_N))


def _states_from_kernel(per_layer):
    lru_h, lru_buf, ssd, ssd_buf, gdn_bd, gdn_buf, s5_re, s5_im = (
        jnp.stack([st[j] for st in per_layer], axis=0) for j in range(8))
    L, B = lru_h.shape[:2]
    gdn = jnp.einsum('lbhkhv->lbhkv', gdn_bd.reshape(L, B, GDN_HEADS, GDN_HEAD_DIM, GDN_HEADS, GDN_HEAD_DIM))
    return (lru_h.reshape(L, B, LRU_WIDTH), lru_buf, ssd.reshape(L, B, SSD_HEADS, SSD_INNER // SSD_HEADS, SSD_STATE),
            ssd_buf, gdn, gdn_buf, s5_re.reshape(L, B, S5_GROUPS, S5_STATE), s5_im.reshape(L, B, S5_GROUPS, S5_STATE))


def _layer(l, x, mem_k, mem_v, lkv, states, ls, pk, final):
    B, T, _ = x.shape
    if T >= MIXER_BLOCK:
        mix, new_states = _mixer(x, states, ls, pk, l, True)
    else:
        u = _in_proj(x.reshape(B * T, D_MODEL), pk["g_mix"], pk["w_in"], l).reshape(B, T, U_COLS)
        mix, new_states = _mixer(u, states, ls, pk, l, False)
    x = _attn(x, mix, mem_k, mem_v, lkv, pk, l)
    x = _ffn(x.reshape(B * T, D_MODEL), pk, l, final).reshape(B, T, D_MODEL)
    return x, new_states


def kernel(x_prompt, x_sample, mem_prompt, state_lru_h, cache_lru_conv, state_ssd, cache_ssd_conv, state_gdn, cache_gdn_conv, state_s5_re, state_s5_im, cache_mem_k, cache_mem_v, norm_mix_g, w_in, w_out, lru_conv_w, lru_conv_b, lru_w_r, lru_b_r, lru_w_i, lru_b_i, lru_lambda, ssd_conv_w, ssd_conv_b, ssd_dt_bias, ssd_a_log, ssd_d, ssd_norm_g, gdn_conv_w, gdn_dt_bias, gdn_a_log, gdn_norm_g, s5_a_re, s5_a_im, s5_log_dt, s5_b_re, s5_b_im, s5_c_re, s5_c_im, s5_d, s5_w_glu, s5_b_glu, norm_mem_g, norm_cross_g, w_cq, w_ck, w_cv, w_co, norm_ffn_g, w_ffn_gate, w_ffn_up, w_ffn_down, norm_final_g):
    pk = _prepare(norm_mix_g, w_in, w_out, lru_conv_w, lru_conv_b, lru_w_r, lru_b_r, lru_w_i, lru_b_i, lru_lambda,
                  ssd_conv_w, ssd_conv_b, ssd_dt_bias, ssd_a_log, ssd_d, ssd_norm_g,
                  gdn_conv_w, gdn_dt_bias, gdn_a_log, gdn_norm_g,
                  s5_a_re, s5_a_im, s5_log_dt, s5_b_re, s5_b_im, s5_c_re, s5_c_im, s5_d, s5_w_glu, s5_b_glu,
                  norm_mem_g, norm_cross_g, w_cq, w_ck, w_cv, w_co,
                  norm_ffn_g, w_ffn_gate, w_ffn_up, w_ffn_down, norm_final_g)
    bp = x_prompt.shape[0]
    bs = x_sample.shape[0]
    k1 = CONV_WIDTH - 1
    zero_states = _states_to_kernel(
        jnp.zeros((1, bp, LRU_WIDTH), F32), jnp.zeros((1, bp, k1, LRU_WIDTH), F32),
        jnp.zeros((1, bp, SSD_HEADS, SSD_INNER // SSD_HEADS, SSD_STATE), F32), jnp.zeros((1, bp, k1, SSD_CONV_DIM), F32),
        jnp.zeros((1, bp, GDN_HEADS, GDN_HEAD_DIM, GDN_HEAD_DIM), F32), jnp.zeros((1, bp, k1, GDN_CONV_DIM), F32),
        jnp.zeros((1, bp, S5_GROUPS, S5_STATE), F32), jnp.zeros((1, bp, S5_GROUPS, S5_STATE), F32))
    sample_states = _states_to_kernel(state_lru_h, cache_lru_conv, state_ssd, cache_ssd_conv,
                                      state_gdn, cache_gdn_conv, state_s5_re, state_s5_im)
    cache_k = cache_mem_k.reshape(DEPTH, bs, MEM_TOKENS, D_MODEL)
    cache_v = cache_mem_v.reshape(DEPTH, bs, MEM_TOKENS, D_MODEL)
    xp, xs = x_prompt, x_sample
    p_states, s_states, p_mk, p_mv = [], [], [], []
    for l in range(DEPTH):
        final = l == DEPTH - 1
        mk, mv = _memory_kv(mem_prompt, pk, l)
        xp, sp = _layer(l, xp, mk[None], mv[None], 0, zero_states, 0, pk, final)
        xs, ss = _layer(l, xs, cache_k, cache_v, l, sample_states, l, pk, final)
        p_states.append(sp)
        s_states.append(ss)
        p_mk.append(mk.reshape(bp, MEM_TOKENS, MEM_HEADS, MEM_HEAD_DIM))
        p_mv.append(mv.reshape(bp, MEM_TOKENS, MEM_HEADS, MEM_HEAD_DIM))

    return ((xp, xs) + _states_from_kernel(p_states) + (jnp.stack(p_mk, axis=0), jnp.stack(p_mv, axis=0))
            + _states_from_kernel(s_states))
```

```python
import functools
import math

import jax
import jax.numpy as jnp
import numpy as np
from jax import lax
from jax.experimental import pallas as pl
from jax.experimental.pallas import tpu as pltpu

F32 = jnp.float32
MXU_DTYPE = jnp.bfloat16

D_MODEL = 1024
DEPTH = 4
CONV_WIDTH = 4
RMS_EPS = 1e-6
LRU_WIDTH = 256
LRU_BLOCKS = 4
LRU_C = 8.0
SSD_INNER = 256
SSD_HEADS = 4
SSD_GROUPS = 2
SSD_STATE = 128
SSD_CONV_DIM = 768
GDN_WIDTH = 256
GDN_HEAD_DIM = 64
GDN_HEADS = 4
GDN_CONV_DIM = 768
S5_WIDTH = 256
S5_GROUP_CH = 16
S5_GROUPS = 16
S5_STATE = 64
S5_N = S5_GROUPS * S5_STATE
MEM_TOKENS = 256
MEM_HEADS = 4
MEM_HEAD_DIM = 256
FFN_HIDDEN = 2816
IN_COLS = 2828

U_GATE, U_LRUX, U_SSDZ, U_XBC, U_QKV, U_GDNZ, U_S5, U_SMALL = 0, 256, 512, 768, 1536, 2304, 2560, 2816
U_COLS = 2944
SM_DT, SM_BETA, SM_A = 0, 4, 8

LANE = 128
SUBLANE = 8
VMEM_LIMIT = 56 * 1024 * 1024
HIST = SUBLANE
ROW_TILE = 512
MIXER_BLOCK = 256
SHORT_SEQ_BATCH = 8


def _mm(a, b):
    return jnp.dot(a.astype(MXU_DTYPE), b.astype(MXU_DTYPE), preferred_element_type=F32)


def _mm_nt(a, b):
    return lax.dot_general(a.astype(MXU_DTYPE), b.astype(MXU_DTYPE), (((1,), (1,)), ((), ())),
                           preferred_element_type=F32)


def _mm_tn(a, b):
    return lax.dot_general(a.astype(MXU_DTYPE), b.astype(MXU_DTYPE), (((0,), (0,)), ((), ())),
                           preferred_element_type=F32)


def _rms(x, g):
    ms = jnp.mean(x * x, axis=-1, keepdims=True)
    return x * lax.rsqrt(ms + RMS_EPS) * g


def _silu(x):
    return x * jax.nn.sigmoid(x)


def _gelu(x):
    return jax.nn.gelu(x, approximate=True)


def _shift_rows(x, k, row, fill):
    return jnp.where(row >= k, pltpu.roll(x, k, 0), fill)


def _in_proj_kernel(x_ref, g_ref, w_ref, u_ref):
    h = _rms(x_ref[...], g_ref[...])
    u_ref[...] = _mm(h, w_ref[...])


def _in_proj(x2d, g_all, w_all, l):
    R = x2d.shape[0]
    TM = min(R, MIXER_BLOCK)
    return pl.pallas_call(
        _in_proj_kernel,
        grid=(R // TM,),
        in_specs=[pl.BlockSpec((TM, D_MODEL), lambda i: (i, 0)),
                  pl.BlockSpec((None, 1, D_MODEL), lambda i: (l, 0, 0)),
                  pl.BlockSpec((None, D_MODEL, U_COLS), lambda i: (l, 0, 0))],
        out_specs=pl.BlockSpec((TM, U_COLS), lambda i: (i, 0)),
        out_shape=jax.ShapeDtypeStruct((R, U_COLS), F32),
        compiler_params=pltpu.CompilerParams(dimension_semantics=("parallel",), vmem_limit_bytes=VMEM_LIMIT),
        name="in_proj",
    )(x2d, g_all, w_all)


def _conv_block(xp_ref, x, w_ref, w_row0, TB):
    xp_ref[pl.ds(HIST, TB), :] = x
    y = x * w_ref[w_row0 + CONV_WIDTH - 1:w_row0 + CONV_WIDTH, :]
    for j in range(CONV_WIDTH - 1):
        off = HIST - (CONV_WIDTH - 1) + j
        y = y + xp_ref[pl.ds(off, TB), :] * w_ref[w_row0 + j:w_row0 + j + 1, :]
    return y


def _interleave(streams):
    streams = list(streams)
    while streams:
        for s in list(streams):
            try:
                next(s)
            except StopIteration:
                streams.remove(s)


def _after(x, dep):
    if dep is None:
        return x
    z = dep[0:SUBLANE, 0:LANE] * 0.0
    return x + jnp.tile(z, (x.shape[0] // SUBLANE, x.shape[1] // LANE))


def _interleave_paced(lead, others):
    lead_gen, lead_steps = lead
    done = [0] * len(others)
    for r in range(1, lead_steps + 1):
        next(lead_gen, None)
        for j, (gen, steps) in enumerate(others):
            target = -(-steps * r // lead_steps)
            while done[j] < target:
                next(gen, None)
                done[j] += 1
    for gen in [lead_gen] + [g for g, _ in others]:
        for _ in gen:
            pass


def _mixer_kernel(TB, QG, NT, NB, fused, *refs):
    n_in = (3 if fused else 1) + 8
    n_par = 8
    n_out = 9
    ins, params = refs[:n_in], refs[n_in:n_in + n_par]
    outs = refs[n_in + n_par:n_in + n_par + n_out]
    scratch = refs[n_in + n_par + n_out:]
    streams, finals = [], []
    for bb in range(NB):
        if fused:
            x_ref, g_ref, win_ref = ins[:3]
            hb = _rms(x_ref[bb], g_ref[...]).astype(MXU_DTYPE)
            states_in = ins[3:]

            def useg(start, width, hb=hb, win_ref=win_ref):
                return _mm(hb, win_ref[:, start:start + width])
        else:
            u_ref = ins[0].at[bb]
            states_in = ins[1:]

            def useg(start, width, u_ref=u_ref):
                return u_ref[:, start:start + width]

        seq_streams, finish = _mixer_sequence(TB, QG, NT, NB == 1, useg, *[r.at[bb] for r in states_in], *params,
                                              *[r.at[bb] for r in outs], *[r.at[bb] for r in scratch])
        streams.append(seq_streams)
        finals.append(finish)
    if NB == 1:
        gdn_stream, s5_stream, ssd_stream, lru_stream = streams[0]
        next(gdn_stream)
        next(gdn_stream)
        n_lead = 2 * (int(math.log2(QG)) - 1) * GDN_HEADS + GDN_HEADS + 2 * (TB // QG) + 2
        n_groups = TB // SUBLANE + 3
        _interleave_paced((gdn_stream, n_lead), [(s5_stream, n_groups), (lru_stream, n_groups), (ssd_stream, 8)])
    else:
        _interleave([seq[j] for j in range(4) for seq in streams])
    for finish in finals:
        finish()


def _mixer_sequence(TB, QG, NT, paced, useg,
                  lru_h0, lru_b0, ssd_s0, ssd_b0, gdn_s0, gdn_b0, s5r0, s5i0,
                  v256, v768, v128, v1024, wri, bmat, cmat, wglu,
                  mix_ref, lru_h_o, lru_b_o, ssd_s_o, ssd_b_o, gdn_s_o, gdn_b_o, s5r_o, s5i_o,
                  xp_lru, xp_ssd, xp_gdn, h_lru, s_ssd, s_gdn, x_s5, hs_scr, x5_scr):
    t = pl.program_id(1)
    K1 = CONV_WIDTH - 1
    NG = TB // SUBLANE

    @pl.when(t == 0)
    def _init():
        for xp, b0 in ((xp_lru, lru_b0), (xp_ssd, ssd_b0), (xp_gdn, gdn_b0)):
            xp[pl.ds(0, HIST), :] = jnp.zeros((HIST, xp.shape[1]), F32)
            xp[pl.ds(HIST - K1, K1), :] = b0[...]
        h_lru[...] = jnp.zeros(h_lru.shape, F32)
        h_lru[0:1, :] = lru_h0[...]
        s_ssd[...] = ssd_s0[...]
        s_gdn[...] = gdn_s0[...]
        x_s5[...] = jnp.zeros(x_s5.shape, F32)
        x_s5[0, SUBLANE - 1:SUBLANE, :] = s5r0[...]
        x_s5[1, SUBLANE - 1:SUBLANE, :] = s5i0[...]

    row128 = lax.broadcasted_iota(jnp.int32, (TB, LANE), 0)
    lane128 = lax.broadcasted_iota(jnp.int32, (TB, LANE), 1)
    small = useg(U_SMALL, LANE)
    sp = jax.nn.softplus(small + v128[0:1, :])
    sig = jax.nn.sigmoid(small)
    gcum = sp * v128[1:2, :]
    rk = jnp.where(lane128 < SM_BETA, row128, row128 & (QG - 1))
    k = 1
    while k < TB:
        gcum = gcum + jnp.where(rk >= k, pltpu.roll(gcum, k, 0), 0.0)
        k *= 2
    if TB < LANE:
        gcum_t = jnp.concatenate([gcum, jnp.zeros((LANE - TB, LANE), F32)], axis=0).T
    else:
        gcum_t = gcum.T
    rr = lax.broadcasted_iota(jnp.int32, (TB, TB), 0)
    cc = lax.broadcasted_iota(jnp.int32, (TB, TB), 1)
    sync = [None]

    def mark(x):
        if paced:
            sync[0] = x

    def lru():
        u_lru = useg(U_GATE, 2 * LRU_WIDTH)
        gate = u_lru[:, 0:LRU_WIDTH]
        xc = _conv_block(xp_lru, u_lru[:, LRU_WIDTH:2 * LRU_WIDTH], v256, 0, TB) + v256[4:5, :]
        ri = _mm(xc, wri[...])
        yield
        r = jax.nn.sigmoid(ri[:, :LRU_WIDTH] + v256[5:6, :])
        ig = jax.nn.sigmoid(ri[:, LRU_WIDTH:] + v256[6:7, :])
        log_a = (-LRU_C) * r * v256[7:8, :]
        a = jnp.exp(log_a)
        bt = jnp.sqrt(-jnp.tanh(log_a) * (a * a + 1.0)) * (ig * xc)
        yield
        row8 = lax.broadcasted_iota(jnp.int32, (SUBLANE, LRU_WIDTH), 0)
        h_prev = h_lru[0:1, :]
        for i in range(NG):
            a_g = a[i * SUBLANE:(i + 1) * SUBLANE]
            b_g = bt[i * SUBLANE:(i + 1) * SUBLANE]
            k = 1
            while k < SUBLANE:
                b_g = b_g + a_g * _shift_rows(b_g, k, row8, 0.0)
                a_g = a_g * _shift_rows(a_g, k, row8, 1.0)
                k *= 2
            h_g = _after(b_g, sync[0]) + a_g * h_prev
            h_prev = h_g[SUBLANE - 1:SUBLANE, :]
            hs_scr[pl.ds(i * SUBLANE, SUBLANE), :] = h_g
            if paced or i % 4 == 3:
                yield
        h_lru[0:1, :] = h_prev
        mix_ref[:, 0:LRU_WIDTH] = hs_scr[...] * _gelu(gate)

    def ssd():
        u_ssd = useg(U_SSDZ, SSD_INNER + SSD_CONV_DIM)
        z = u_ssd[:, 0:SSD_INNER]
        xbc = _silu(_conv_block(xp_ssd, u_ssd[:, SSD_INNER:SSD_INNER + SSD_CONV_DIM], v768, 0, TB) + v768[4:5, :])
        yield
        causal = rr >= cc
        lo = lane128 < (LANE // 2)
        row_s = lax.broadcasted_iota(jnp.int32, (LANE, SSD_STATE), 0) < (LANE // 2)
        y_pairs = []
        for p in range(SSD_GROUPS):
            xs_p = xbc[:, p * LANE:(p + 1) * LANE]
            b_p = xbc[:, SSD_INNER + p * SSD_STATE:SSD_INNER + (p + 1) * SSD_STATE]
            c_p = xbc[:, SSD_INNER + SSD_GROUPS * SSD_STATE + p * SSD_STATE:
                      SSD_INNER + SSD_GROUPS * SSD_STATE + (p + 1) * SSD_STATE]
            cb = _mm_nt(c_p, b_p)
            h0, h1 = 2 * p, 2 * p + 1
            xdt = xs_p * jnp.where(lo, sp[:, SM_DT + h0:SM_DT + h0 + 1], sp[:, SM_DT + h1:SM_DT + h1 + 1])
            s_pair = s_ssd[p]
            ys = []
            for h in (h0, h1):
                gcol = gcum[:, SM_DT + h:SM_DT + h + 1]
                grow = gcum_t[SM_DT + h:SM_DT + h + 1, 0:TB]
                seg = jnp.exp(jnp.where(causal, gcol - grow, -jnp.inf))
                y_diag = _mm(cb * seg, xdt)
                y_off = _mm_nt(c_p * jnp.exp(gcol), s_pair)
                ys.append(y_diag + y_off)
                yield
            y_pairs.append(jnp.where(lo, ys[0], ys[1]))
            g0 = gcum[:, SM_DT + h0:SM_DT + h0 + 1]
            g1 = gcum[:, SM_DT + h1:SM_DT + h1 + 1]
            gl0 = g0[TB - 1:TB, :]
            gl1 = g1[TB - 1:TB, :]
            to_end = jnp.where(lo, jnp.exp(gl0 - g0), jnp.exp(gl1 - g1))
            s_ssd[p] = jnp.where(row_s, jnp.exp(gl0), jnp.exp(gl1)) * s_pair + _mm_tn(xdt * to_end, b_p)
            yield
        y = jnp.concatenate(y_pairs, axis=-1) + v256[8:9, :] * xbc[:, 0:SSD_INNER]
        yg = y * _silu(z)
        gs = SSD_INNER // SSD_GROUPS
        for p in range(SSD_GROUPS):
            mix_ref[:, LRU_WIDTH + p * gs:LRU_WIDTH + (p + 1) * gs] = _rms(
                yg[:, p * gs:(p + 1) * gs], v256[9:10, p * gs:(p + 1) * gs])

    def gdn():
        lane = lax.broadcasted_iota(jnp.int32, (TB, GDN_WIDTH), 1)
        hmask = [(lane >= h * GDN_HEAD_DIM) & (lane < (h + 1) * GDN_HEAD_DIM) for h in range(GDN_HEADS)]

        def by_head(vals):
            return jnp.where(hmask[0], vals[0], jnp.where(hmask[1], vals[1], jnp.where(hmask[2], vals[2], vals[3])))

        def head_sum(x):
            return by_head([jnp.sum(jnp.where(m, x, 0.0), axis=-1, keepdims=True) for m in hmask])

        u_gdn = useg(U_QKV, GDN_CONV_DIM + GDN_WIDTH)
        qkv = _silu(_conv_block(xp_gdn, u_gdn[:, 0:GDN_CONV_DIM], v768, 5, TB))
        qf = qkv[:, 0:GDN_WIDTH]
        kf = qkv[:, GDN_WIDTH:2 * GDN_WIDTH]
        vf = qkv[:, 2 * GDN_WIDTH:3 * GDN_WIDTH]
        qf = qf * lax.rsqrt(head_sum(qf * qf) + 1e-6) * (GDN_HEAD_DIM ** -0.5)
        kf = kf * lax.rsqrt(head_sum(kf * kf) + 1e-6)
        beta_f = by_head([sig[:, SM_BETA + h:SM_BETA + h + 1] for h in range(GDN_HEADS)])
        gc_f = by_head([gcum[:, SM_A + h:SM_A + h + 1] for h in range(GDN_HEADS)])
        chunk_shift = int(math.log2(QG))
        same_chunk = lax.shift_right_logical(rr, chunk_shift) == lax.shift_right_logical(cc, chunk_shift)
        incl = same_chunk & (rr >= cc)
        strict = same_chunk & (rr > cc)
        eg = jnp.exp(gc_f)
        kq = _mm_nt(jnp.concatenate([jnp.where(m, kf, 0.0) for m in hmask] + [jnp.where(m, qf, 0.0) for m in hmask],
                                    axis=0), kf)
        rhs = jnp.concatenate([beta_f * vf, beta_f * eg * kf], axis=1)
        yield
        decs, pws, rs = [], [], []
        for h in range(GDN_HEADS):
            gcol = gcum[:, SM_A + h:SM_A + h + 1]
            grow = gcum_t[SM_A + h:SM_A + h + 1, 0:TB]
            dec = jnp.exp(jnp.where(incl, gcol - grow, -jnp.inf))
            low = jnp.where(strict, sig[:, SM_BETA + h:SM_BETA + h + 1] * kq[h * TB:(h + 1) * TB] * dec, 0.0)
            decs.append(dec)
            pws.append(-low)
            rs.append(-low)
        mark(pws[GDN_HEADS - 1])
        yield
        for _ in range(chunk_shift - 1):
            for h in range(GDN_HEADS):
                pws[h] = _mm(pws[h], pws[h])
                mark(pws[h])
                if paced:
                    yield
                rs[h] = rs[h] + pws[h] + _mm(rs[h], pws[h])
                mark(rs[h])
                yield
        sols, qks = [], []
        for h in range(GDN_HEADS):
            sols.append(rhs + _mm(rs[h], rhs))
            qks.append(kq[(GDN_HEADS + h) * TB:(GDN_HEADS + h + 1) * TB] * decs[h])
            yield
        u_all = by_head([s[:, 0:GDN_WIDTH] for s in sols])
        w_all = by_head([s[:, GDN_WIDTH:2 * GDN_WIDTH] for s in sols])
        q_dec = qf * eg
        head_shift = int(math.log2(GDN_HEAD_DIM))
        r2 = lax.shift_right_logical(lax.broadcasted_iota(jnp.int32, (GDN_WIDTH, GDN_WIDTH), 0), head_shift)
        c2 = lax.shift_right_logical(lax.broadcasted_iota(jnp.int32, (GDN_WIDTH, GDN_WIDTH), 1), head_shift)
        blockdiag = r2 == c2
        s_bd = s_gdn[...]
        deltas, o_state = [], []
        for c in range(TB // QG):
            sl = slice(c * QG, (c + 1) * QG)
            ws = _mm(jnp.concatenate([w_all[sl], q_dec[sl]], axis=0), s_bd)
            delta = u_all[sl] - ws[0:QG]
            o_state.append(ws[QG:2 * QG])
            mark(delta)
            yield
            g_c = gc_f[sl]
            g_last = g_c[QG - 1:QG, :]
            s_bd = s_bd * jnp.exp(g_last) + jnp.where(blockdiag, _mm_tn(kf[sl] * jnp.exp(g_last - g_c), delta), 0.0)
            deltas.append(delta)
            yield
        s_gdn[...] = s_bd
        delta_all = jnp.concatenate(deltas, axis=0)
        o = jnp.concatenate(o_state, axis=0) + by_head([_mm(qks[h], delta_all) for h in range(GDN_HEADS)])
        yield
        ms = head_sum(o * o) * (1.0 / GDN_HEAD_DIM)
        o = o * lax.rsqrt(ms + RMS_EPS) * v256[10:11, :]
        zg = u_gdn[:, GDN_CONV_DIM:GDN_CONV_DIM + GDN_WIDTH]
        mix_ref[:, 2 * LRU_WIDTH:2 * LRU_WIDTH + GDN_WIDTH] = o * _silu(zg)

    def s5():
        row_n = lax.broadcasted_iota(jnp.int32, (SUBLANE, S5_N), 0)
        us5 = useg(U_S5, S5_WIDTH)
        bu = _mm(us5, bmat[...])
        yield
        lam_r, lam_i = v1024[0:1, :], v1024[1:2, :]
        pows = []
        pr, pi = lam_r, lam_i
        k = 1
        while k < SUBLANE:
            pows.append((k, jnp.where(row_n >= k, pr, 0.0), jnp.where(row_n >= k, pi, 0.0)))
            pr, pi = pr * pr - pi * pi, 2.0 * (pr * pi)
            k *= 2
        in_r = jnp.where(row_n == 0, lam_r, 0.0)
        in_i = jnp.where(row_n == 0, lam_i, 0.0)
        g_r = x_s5[0]
        g_i = x_s5[1]
        for i in range(NG):
            p_r = _after(pltpu.roll(g_r, 1, 0), sync[0])
            p_i = _after(pltpu.roll(g_i, 1, 0), sync[0])
            g_r = bu[i * SUBLANE:(i + 1) * SUBLANE, 0:S5_N] + (in_r * p_r - in_i * p_i)
            g_i = bu[i * SUBLANE:(i + 1) * SUBLANE, S5_N:2 * S5_N] + (in_r * p_i + in_i * p_r)
            for k, pr, pi in pows:
                s_r = pltpu.roll(g_r, k, 0)
                s_i = pltpu.roll(g_i, k, 0)
                g_r, g_i = g_r + (pr * s_r - pi * s_i), g_i + (pr * s_i + pi * s_r)
            x5_scr[pl.ds(i * SUBLANE, SUBLANE), 0:S5_N] = g_r
            x5_scr[pl.ds(i * SUBLANE, SUBLANE), S5_N:2 * S5_N] = g_i
            yield
        x_s5[0] = g_r
        x_s5[1] = g_i
        y5 = _mm(x5_scr[...], cmat[...])
        y5 = _gelu(y5 + v256[11:12, :] * us5)
        yield
        mix_ref[:, 3 * LRU_WIDTH:3 * LRU_WIDTH + S5_WIDTH] = y5 * jax.nn.sigmoid(_mm(y5, wglu[...]) + v256[12:13, :])

    def finish():
        for xp in (xp_lru, xp_ssd, xp_gdn):
            xp[pl.ds(0, HIST), :] = xp[pl.ds(TB, HIST), :]

        pl.when(t == NT - 1)(_emit)

    def _emit():
        lru_h_o[...] = h_lru[0:1, :]
        lru_b_o[...] = xp_lru[pl.ds(HIST - K1, K1), :]
        ssd_b_o[...] = xp_ssd[pl.ds(HIST - K1, K1), :]
        gdn_b_o[...] = xp_gdn[pl.ds(HIST - K1, K1), :]
        ssd_s_o[...] = s_ssd[...]
        gdn_s_o[...] = s_gdn[...]
        s5r_o[...] = x_s5[0, SUBLANE - 1:SUBLANE, :]
        s5i_o[...] = x_s5[1, SUBLANE - 1:SUBLANE, :]

    return [gdn(), s5(), ssd(), lru()], finish


def _mixer(src, states, ls, pk, l, fused):
    B, T, _ = src.shape
    TB = min(T, MIXER_BLOCK)
    QG = min(TB, 64)
    NT = T // TB
    NB = SHORT_SEQ_BATCH if (B % SHORT_SEQ_BATCH == 0 and TB < MIXER_BLOCK) else 1

    def per_b(shape):
        nd = len(shape)
        return pl.BlockSpec((NB,) + tuple(shape[1:]), lambda b, t: (b,) + (0,) * (nd - 1))

    def per_b_of_layer(shape):
        nd = len(shape)
        return pl.BlockSpec((None, NB) + tuple(shape[2:]), lambda b, t: (ls, b) + (0,) * (nd - 2))

    def per_l(arr):
        nd = arr.ndim
        return pl.BlockSpec((None,) + tuple(arr.shape[1:]), lambda b, t: (l,) + (0,) * (nd - 1))

    proj = (pk["g_mix"], pk["w_in"]) if fused else ()
    params = (pk["v256"], pk["v768"], pk["v128"], pk["v1024"], pk["wri"], pk["bmat"], pk["cmat"], pk["wglu"])
    state_shapes = [s.shape[1:] for s in states]
    out_shape = [jax.ShapeDtypeStruct((B, T, D_MODEL), F32)] + [jax.ShapeDtypeStruct(s, F32) for s in state_shapes]
    outs = pl.pallas_call(
        functools.partial(_mixer_kernel, TB, QG, NT, NB, fused),
        grid=(B // NB, NT),
        in_specs=[pl.BlockSpec((NB, TB, src.shape[2]), lambda b, t: (b, t, 0))] + [per_l(p) for p in proj]
                 + [per_b_of_layer(s.shape) for s in states] + [per_l(p) for p in params],
        out_specs=[pl.BlockSpec((NB, TB, D_MODEL), lambda b, t: (b, t, 0))] + [per_b(s) for s in state_shapes],
        out_shape=out_shape,
        scratch_shapes=[pltpu.VMEM((NB, TB + HIST, LRU_WIDTH), F32),
                        pltpu.VMEM((NB, TB + HIST, SSD_CONV_DIM), F32),
                        pltpu.VMEM((NB, TB + HIST, GDN_CONV_DIM), F32),
                        pltpu.VMEM((NB, SUBLANE, LRU_WIDTH), F32),
                        pltpu.VMEM((NB, SSD_GROUPS, LANE, SSD_STATE), F32),
                        pltpu.VMEM((NB, GDN_WIDTH, GDN_WIDTH), F32),
                        pltpu.VMEM((NB, 2, SUBLANE, S5_N), F32),
                        pltpu.VMEM((NB, TB, LRU_WIDTH), F32),
                        pltpu.VMEM((NB, TB, 2 * S5_N), F32)],
        compiler_params=pltpu.CompilerParams(dimension_semantics=("arbitrary", "arbitrary"),
                                             vmem_limit_bytes=VMEM_LIMIT),
        name="mixer",
    )(src, *proj, *states, *params)
    return outs[0], tuple(outs[1:])


def _attend(q, k_ref, v_ref):
    heads = []
    for h in range(MEM_HEADS):
        sl = slice(h * MEM_HEAD_DIM, (h + 1) * MEM_HEAD_DIM)
        s = _mm_nt(q[:, sl], k_ref[:, sl]) * (MEM_HEAD_DIM ** -0.5)
        e = jnp.exp(s - jnp.max(s, axis=-1, keepdims=True))
        p = e / jnp.sum(e, axis=-1, keepdims=True)
        heads.append(_mm(p, v_ref[:, sl]))
    return jnp.concatenate(heads, axis=-1)


def _attn_kernel(x_ref, mix_ref, wout, g_ref, wq, k_ref, v_ref, wo, o_ref):
    x1 = x_ref[0] + _mm(mix_ref[0], wout[...])
    q = _mm(_rms(x1, g_ref[...]), wq[...])
    o_ref[0] = x1 + _mm(_attend(q, k_ref, v_ref), wo[...])


def _attn_short_kernel(T, x_ref, mix_ref, wout, g_ref, wq, k_ref, v_ref, wo, o_ref, x1_scr, q_scr, a_scr):
    b = pl.program_id(0)

    @pl.when(b == 0)
    def _project_in():
        x1 = x_ref[...] + _mm(mix_ref[...], wout[...])
        x1_scr[...] = x1
        q_scr[...] = _mm(_rms(x1, g_ref[...]), wq[...])

    rows = pl.ds(pl.multiple_of(b * T, T), T)
    a_scr[rows, :] = _attend(q_scr[rows, :], k_ref, v_ref)

    @pl.when(b == pl.num_programs(0) - 1)
    def _project_out():
        o_ref[...] = x1_scr[...] + _mm(a_scr[...], wo[...])


def _attn(x, mix, mem_k, mem_v, lkv, pk, l):
    B, T, _ = x.shape

    def per_l(arr):
        nd = arr.ndim
        return pl.BlockSpec((None,) + tuple(arr.shape[1:]), lambda *i: (l,) + (0,) * (nd - 1),
                            pipeline_mode=pl.Buffered(1))

    weights = (pk["w_out"], pk["g_cross"], pk["w_cq"])
    if T >= MIXER_BLOCK:
        TB = min(T, ROW_TILE)
        blk = pl.BlockSpec((1, TB, D_MODEL), lambda b, t: (b, t, 0))
        kv = pl.BlockSpec((None, None, MEM_TOKENS, D_MODEL), lambda b, t: (lkv, b, 0, 0))
        return pl.pallas_call(
            _attn_kernel,
            grid=(B, T // TB),
            in_specs=[blk, blk] + [per_l(w) for w in weights] + [kv, kv, per_l(pk["w_co"])],
            out_specs=blk,
            out_shape=jax.ShapeDtypeStruct((B, T, D_MODEL), F32),
            compiler_params=pltpu.CompilerParams(dimension_semantics=("parallel", "parallel"),
                                                 vmem_limit_bytes=VMEM_LIMIT),
            name="attn",
        )(x, mix, *weights, mem_k, mem_v, pk["w_co"])
    R = B * T
    rows = pl.BlockSpec((R, D_MODEL), lambda b: (0, 0))
    kv = pl.BlockSpec((None, None, MEM_TOKENS, D_MODEL), lambda b: (lkv, b, 0, 0))
    out = pl.pallas_call(
        functools.partial(_attn_short_kernel, T),
        grid=(B,),
        in_specs=[rows, rows] + [per_l(w) for w in weights] + [kv, kv, per_l(pk["w_co"])],
        out_specs=rows,
        out_shape=jax.ShapeDtypeStruct((R, D_MODEL), F32),
        scratch_shapes=[pltpu.VMEM((R, D_MODEL), F32)] * 3,
        compiler_params=pltpu.CompilerParams(dimension_semantics=("arbitrary",), vmem_limit_bytes=VMEM_LIMIT),
        name="attn_short",
    )(x.reshape(R, D_MODEL), mix.reshape(R, D_MODEL), *weights, mem_k, mem_v, pk["w_co"])
    return out.reshape(B, T, D_MODEL)


def _ffn_kernel(final, x_ref, g_ref, wg, wu, wd, gf_ref, o_ref):
    x = x_ref[...]
    h = _rms(x, g_ref[...]).astype(MXU_DTYPE)
    y = x + _mm(_silu(_mm(h, wg[...])) * _mm(h, wu[...]), wd[...])
    if final:
        y = _rms(y, gf_ref[...])
    o_ref[...] = y


def _ffn(x2d, pk, l, final):
    R = x2d.shape[0]
    TM = min(R, ROW_TILE)

    def per_l(arr):
        nd = arr.ndim
        return pl.BlockSpec((None,) + tuple(arr.shape[1:]), lambda i: (l,) + (0,) * (nd - 1),
                            pipeline_mode=pl.Buffered(1))

    blk = pl.BlockSpec((TM, D_MODEL), lambda i: (i, 0))
    return pl.pallas_call(
        functools.partial(_ffn_kernel, final),
        grid=(R // TM,),
        in_specs=[blk, per_l(pk["g_ffn"]), per_l(pk["w_gate"]), per_l(pk["w_up"]), per_l(pk["w_down"]),
                  pl.BlockSpec((1, D_MODEL), lambda i: (0, 0))],
        out_specs=blk,
        out_shape=jax.ShapeDtypeStruct((R, D_MODEL), F32),
        compiler_params=pltpu.CompilerParams(dimension_semantics=("parallel",), vmem_limit_bytes=VMEM_LIMIT),
        name="ffn",
    )(x2d, pk["g_ffn"], pk["w_gate"], pk["w_up"], pk["w_down"], pk["g_final"])


def _memkv_kernel(m_ref, g_ref, wk, wv, k_ref, v_ref):
    h = _rms(m_ref[0], g_ref[...]).astype(MXU_DTYPE)
    k_ref[0] = _mm(h, wk[...])
    v_ref[0] = _mm(h, wv[...])


def _memory_kv(mem, pk, l):
    B = mem.shape[0]

    def per_l(arr):
        nd = arr.ndim
        return pl.BlockSpec((None,) + tuple(arr.shape[1:]), lambda b: (l,) + (0,) * (nd - 1))

    blk = pl.BlockSpec((1, MEM_TOKENS, D_MODEL), lambda b: (b, 0, 0))
    return pl.pallas_call(
        _memkv_kernel,
        grid=(B,),
        in_specs=[blk, per_l(pk["g_mem"]), per_l(pk["w_ck"]), per_l(pk["w_cv"])],
        out_specs=[blk, blk],
        out_shape=[jax.ShapeDtypeStruct((B, MEM_TOKENS, D_MODEL), F32)] * 2,
        compiler_params=pltpu.CompilerParams(dimension_semantics=("parallel",), vmem_limit_bytes=VMEM_LIMIT),
        name="memory_kv",
    )(mem, pk["g_mem"], pk["w_ck"], pk["w_cv"])


def _block_diag(w):
    L, n, a, b = w.shape
    return jnp.einsum('lnab,nm->lnamb', w, jnp.eye(n, dtype=w.dtype)).reshape(L, n * a, n * b)


def _pack_rows(rows, n_rows):
    L, width = rows[0].shape[0], rows[0].shape[-1]
    rows = [r.reshape(L, -1, width) for r in rows]
    used = sum(r.shape[1] for r in rows)
    return jnp.concatenate(rows + [jnp.zeros((L, n_rows - used, width), F32)], axis=1)


def _prepare(norm_mix_g, w_in, w_out, lru_conv_w, lru_conv_b, lru_w_r, lru_b_r, lru_w_i, lru_b_i, lru_lambda,
             ssd_conv_w, ssd_conv_b, ssd_dt_bias, ssd_a_log, ssd_d, ssd_norm_g,
             gdn_conv_w, gdn_dt_bias, gdn_a_log, gdn_norm_g,
             s5_a_re, s5_a_im, s5_log_dt, s5_b_re, s5_b_im, s5_c_re, s5_c_im, s5_d, s5_w_glu, s5_b_glu,
             norm_mem_g, norm_cross_g, w_cq, w_ck, w_cv, w_co,
             norm_ffn_g, w_ffn_gate, w_ffn_up, w_ffn_down, norm_final_g):
    L = DEPTH
    bf = MXU_DTYPE
    pk = {}
    pk["g_mix"] = norm_mix_g.reshape(L, 1, D_MODEL)
    pk["w_in"] = jnp.concatenate(
        [w_in[:, :, 0:1536], w_in[:, :, 1540:2564], w_in[:, :, 2572:2828], w_in[:, :, 1536:1540],
         w_in[:, :, 2564:2572], jnp.zeros((L, D_MODEL, U_COLS - IN_COLS), F32)], axis=2).astype(bf)
    pk["w_out"] = w_out.astype(bf)
    sp_lam = jax.nn.softplus(-lru_lambda)
    pk["v256"] = _pack_rows(
        [lru_conv_w, lru_conv_b, lru_b_r, lru_b_i, sp_lam, jnp.repeat(ssd_d, SSD_INNER // SSD_HEADS, axis=1),
         ssd_norm_g, jnp.tile(gdn_norm_g, (1, GDN_HEADS)), s5_d, s5_b_glu], 16)
    pk["v768"] = _pack_rows([ssd_conv_w, ssd_conv_b, gdn_conv_w], 16)
    zeros4 = jnp.zeros((L, 4), F32)
    bias = jnp.concatenate([ssd_dt_bias, zeros4, gdn_dt_bias, jnp.zeros((L, LANE - 12), F32)], axis=1)
    avec = jnp.concatenate([-jnp.exp(ssd_a_log.astype(F32)), zeros4, -jnp.exp(gdn_a_log),
                            jnp.zeros((L, LANE - 12), F32)], axis=1)
    pk["v128"] = _pack_rows([bias, avec], 8)
    pk["wri"] = jnp.concatenate([_block_diag(lru_w_r), _block_diag(lru_w_i)], axis=-1).astype(bf)
    a_re = s5_a_re.astype(F32)
    a_im = s5_a_im.astype(F32)
    step = jnp.exp(s5_log_dt.astype(F32))[:, :, None]
    mag = jnp.exp(a_re * step)
    ang = a_im * step
    lb_re, lb_im = mag * jnp.cos(ang), mag * jnp.sin(ang)
    den = a_re * a_re + a_im * a_im
    f_re = ((lb_re - 1.0) * a_re + lb_im * a_im) / den
    f_im = (lb_im * a_re - (lb_re - 1.0) * a_im) / den
    b_re = s5_b_re.astype(F32)
    b_im = s5_b_im.astype(F32)
    bb_re = f_re[..., None] * b_re - f_im[..., None] * b_im
    bb_im = f_re[..., None] * b_im + f_im[..., None] * b_re
    pk["v1024"] = _pack_rows([lb_re.reshape(L, S5_N), lb_im.reshape(L, S5_N)], 8)
    pk["bmat"] = jnp.concatenate([_block_diag(jnp.swapaxes(bb_re, 2, 3)), _block_diag(jnp.swapaxes(bb_im, 2, 3))],
                                 axis=-1).astype(bf)
    pk["cmat"] = jnp.concatenate([_block_diag(jnp.swapaxes(s5_c_re, 2, 3)), -_block_diag(jnp.swapaxes(s5_c_im, 2, 3))],
                                 axis=1).astype(bf)
    pk["wglu"] = s5_w_glu.astype(bf)
    pk["g_mem"] = norm_mem_g.reshape(L, 1, D_MODEL)
    pk["g_cross"] = norm_cross_g.reshape(L, 1, D_MODEL)
    pk["w_cq"], pk["w_ck"], pk["w_cv"], pk["w_co"] = (w.astype(bf) for w in (w_cq, w_ck, w_cv, w_co))
    pk["g_ffn"] = norm_ffn_g.reshape(L, 1, D_MODEL)
    pk["w_gate"], pk["w_up"], pk["w_down"] = (w.astype(bf) for w in (w_ffn_gate, w_ffn_up, w_ffn_down))
    pk["g_final"] = norm_final_g.reshape(1, D_MODEL)
    return pk


def _states_to_kernel(lru_h, lru_buf, ssd, ssd_buf, gdn, gdn_buf, s5_re, s5_im):
    L, B = lru_h.shape[:2]
    eye = jnp.eye(GDN_HEADS, dtype=F32)
    gdn_bd = jnp.einsum('lbhkv,hg->lbhkgv', gdn, eye).reshape(L, B, GDN_WIDTH, GDN_WIDTH)
    return (lru_h.reshape(L, B, 1, LRU_WIDTH), lru_buf, ssd.reshape(L, B, SSD_GROUPS, LANE, SSD_STATE), ssd_buf,
            gdn_bd, gdn_buf, s5_re.reshape(L, B, 1, S5_N), s5_im.reshape(L, B, 1, S5_N))


def _states_from_kernel(per_layer):
    lru_h, lru_buf, ssd, ssd_buf, gdn_bd, gdn_buf, s5_re, s5_im = (
        jnp.stack([st[j] for st in per_layer], axis=0) for j in range(8))
    L, B = lru_h.shape[:2]
    gdn = jnp.einsum('lbhkhv->lbhkv', gdn_bd.reshape(L, B, GDN_HEADS, GDN_HEAD_DIM, GDN_HEADS, GDN_HEAD_DIM))
    return (lru_h.reshape(L, B, LRU_WIDTH), lru_buf, ssd.reshape(L, B, SSD_HEADS, SSD_INNER // SSD_HEADS, SSD_STATE),
            ssd_buf, gdn, gdn_buf, s5_re.reshape(L, B, S5_GROUPS, S5_STATE), s5_im.reshape(L, B, S5_GROUPS, S5_STATE))


def _layer(l, x, mem_k, mem_v, lkv, states, ls, pk, final):
    B, T, _ = x.shape
    if T >= MIXER_BLOCK:
        mix, new_states = _mixer(x, states, ls, pk, l, True)
    else:
        u = _in_proj(x.reshape(B * T, D_MODEL), pk["g_mix"], pk["w_in"], l).reshape(B, T, U_COLS)
        mix, new_states = _mixer(u, states, ls, pk, l, False)
    x = _attn(x, mix, mem_k, mem_v, lkv, pk, l)
    x = _ffn(x.reshape(B * T, D_MODEL), pk, l, final).reshape(B, T, D_MODEL)
    return x, new_states


def kernel(x_prompt, x_sample, mem_prompt, state_lru_h, cache_lru_conv, state_ssd, cache_ssd_conv, state_gdn, cache_gdn_conv, state_s5_re, state_s5_im, cache_mem_k, cache_mem_v, norm_mix_g, w_in, w_out, lru_conv_w, lru_conv_b, lru_w_r, lru_b_r, lru_w_i, lru_b_i, lru_lambda, ssd_conv_w, ssd_conv_b, ssd_dt_bias, ssd_a_log, ssd_d, ssd_norm_g, gdn_conv_w, gdn_dt_bias, gdn_a_log, gdn_norm_g, s5_a_re, s5_a_im, s5_log_dt, s5_b_re, s5_b_im, s5_c_re, s5_c_im, s5_d, s5_w_glu, s5_b_glu, norm_mem_g, norm_cross_g, w_cq, w_ck, w_cv, w_co, norm_ffn_g, w_ffn_gate, w_ffn_up, w_ffn_down, norm_final_g):
    pk = _prepare(norm_mix_g, w_in, w_out, lru_conv_w, lru_conv_b, lru_w_r, lru_b_r, lru_w_i, lru_b_i, lru_lambda,
                  ssd_conv_w, ssd_conv_b, ssd_dt_bias, ssd_a_log, ssd_d, ssd_norm_g,
                  gdn_conv_w, gdn_dt_bias, gdn_a_log, gdn_norm_g,
                  s5_a_re, s5_a_im, s5_log_dt, s5_b_re, s5_b_im, s5_c_re, s5_c_im, s5_d, s5_w_glu, s5_b_glu,
                  norm_mem_g, norm_cross_g, w_cq, w_ck, w_cv, w_co,
                  norm_ffn_g, w_ffn_gate, w_ffn_up, w_ffn_down, norm_final_g)
    bp = x_prompt.shape[0]
    bs = x_sample.shape[0]
    k1 = CONV_WIDTH - 1
    zero_states = _states_to_kernel(
        jnp.zeros((1, bp, LRU_WIDTH), F32), jnp.zeros((1, bp, k1, LRU_WIDTH), F32),
        jnp.zeros((1, bp, SSD_HEADS, SSD_INNER // SSD_HEADS, SSD_STATE), F32), jnp.zeros((1, bp, k1, SSD_CONV_DIM), F32),
        jnp.zeros((1, bp, GDN_HEADS, GDN_HEAD_DIM, GDN_HEAD_DIM), F32), jnp.zeros((1, bp, k1, GDN_CONV_DIM), F32),
        jnp.zeros((1, bp, S5_GROUPS, S5_STATE), F32), jnp.zeros((1, bp, S5_GROUPS, S5_STATE), F32))
    sample_states = _states_to_kernel(state_lru_h, cache_lru_conv, state_ssd, cache_ssd_conv,
                                      state_gdn, cache_gdn_conv, state_s5_re, state_s5_im)
    cache_k = cache_mem_k.reshape(DEPTH, bs, MEM_TOKENS, D_MODEL)
    cache_v = cache_mem_v.reshape(DEPTH, bs, MEM_TOKENS, D_MODEL)
    xp, xs = x_prompt, x_sample
    p_states, s_states, p_mk, p_mv = [], [], [], []
    for l in range(DEPTH):
        final = l == DEPTH - 1
        mk, mv = _memory_kv(mem_prompt, pk, l)
        xp, sp = _layer(l, xp, mk[None], mv[None], 0, zero_states, 0, pk, final)
        xs, ss = _layer(l, xs, cache_k, cache_v, l, sample_states, l, pk, final)
        p_states.append(sp)
        s_states.append(ss)
        p_mk.append(mk.reshape(bp, MEM_TOKENS, MEM_HEADS, MEM_HEAD_DIM))
        p_mv.append(mv.reshape(bp, MEM_TOKENS, MEM_HEADS, MEM_HEAD_DIM))

    return ((xp, xs) + _states_from_kernel(p_states) + (jnp.stack(p_mk, axis=0), jnp.stack(p_mv, axis=0))
            + _states_from_kernel(s_states))
```

```python
import functools
import math

import jax
import jax.numpy as jnp
import numpy as np
from jax import lax
from jax.experimental import pallas as pl
from jax.experimental.pallas import tpu as pltpu

F32 = jnp.float32
MXU_DTYPE = jnp.bfloat16

D_MODEL = 1024
DEPTH = 4
CONV_WIDTH = 4
RMS_EPS = 1e-6
LRU_WIDTH = 256
LRU_BLOCKS = 4
LRU_C = 8.0
SSD_INNER = 256
SSD_HEADS = 4
SSD_GROUPS = 2
SSD_STATE = 128
SSD_CONV_DIM = 768
GDN_WIDTH = 256
GDN_HEAD_DIM = 64
GDN_HEADS = 4
GDN_CONV_DIM = 768
S5_WIDTH = 256
S5_GROUP_CH = 16
S5_GROUPS = 16
S5_STATE = 64
S5_N = S5_GROUPS * S5_STATE
MEM_TOKENS = 256
MEM_HEADS = 4
MEM_HEAD_DIM = 256
FFN_HIDDEN = 2816
IN_COLS = 2828

U_GATE, U_LRUX, U_SSDZ, U_XBC, U_QKV, U_GDNZ, U_S5, U_SMALL = 0, 256, 512, 768, 1536, 2304, 2560, 2816
U_COLS = 2944
SM_DT, SM_BETA, SM_A = 0, 4, 8

LANE = 128
SUBLANE = 8
VMEM_LIMIT = 56 * 1024 * 1024
HIST = SUBLANE
ROW_TILE = 1024
MIXER_BLOCK = 256
SHORT_SEQ_BATCH = 8


def _mm(a, b):
    return jnp.dot(a.astype(MXU_DTYPE), b.astype(MXU_DTYPE), preferred_element_type=F32)


def _mm_nt(a, b):
    return lax.dot_general(a.astype(MXU_DTYPE), b.astype(MXU_DTYPE), (((1,), (1,)), ((), ())),
                           preferred_element_type=F32)


def _mm_tn(a, b):
    return lax.dot_general(a.astype(MXU_DTYPE), b.astype(MXU_DTYPE), (((0,), (0,)), ((), ())),
                           preferred_element_type=F32)


def _rms(x, g):
    ms = jnp.mean(x * x, axis=-1, keepdims=True)
    return x * lax.rsqrt(ms + RMS_EPS) * g


def _silu(x):
    return x * jax.nn.sigmoid(x)


def _gelu(x):
    return jax.nn.gelu(x, approximate=True)


def _shift_rows(x, k, row, fill):
    return jnp.where(row >= k, pltpu.roll(x, k, 0), fill)


def _in_proj_kernel(x_ref, g_ref, w_ref, u_ref):
    h = _rms(x_ref[...], g_ref[...])
    u_ref[...] = _mm(h, w_ref[...])


def _in_proj(x2d, g_all, w_all, l):
    R = x2d.shape[0]
    TM = min(R, MIXER_BLOCK)
    return pl.pallas_call(
        _in_proj_kernel,
        grid=(R // TM,),
        in_specs=[pl.BlockSpec((TM, D_MODEL), lambda i: (i, 0)),
                  pl.BlockSpec((None, 1, D_MODEL), lambda i: (l, 0, 0)),
                  pl.BlockSpec((None, D_MODEL, U_COLS), lambda i: (l, 0, 0))],
        out_specs=pl.BlockSpec((TM, U_COLS), lambda i: (i, 0)),
        out_shape=jax.ShapeDtypeStruct((R, U_COLS), F32),
        compiler_params=pltpu.CompilerParams(dimension_semantics=("parallel",), vmem_limit_bytes=VMEM_LIMIT),
        name="in_proj",
    )(x2d, g_all, w_all)


def _conv_block(xp_ref, x, w_ref, w_row0, TB):
    xp_ref[pl.ds(HIST, TB), :] = x
    y = x * w_ref[w_row0 + CONV_WIDTH - 1:w_row0 + CONV_WIDTH, :]
    for j in range(CONV_WIDTH - 1):
        off = HIST - (CONV_WIDTH - 1) + j
        y = y + xp_ref[pl.ds(off, TB), :] * w_ref[w_row0 + j:w_row0 + j + 1, :]
    return y


def _interleave(streams):
    streams = list(streams)
    while streams:
        for s in list(streams):
            try:
                next(s)
            except StopIteration:
                streams.remove(s)


def _after(x, dep):
    if dep is None:
        return x
    z = dep[0:SUBLANE, 0:LANE] * 0.0
    return x + jnp.tile(z, (x.shape[0] // SUBLANE, x.shape[1] // LANE))


def _interleave_paced(lead, others):
    lead_gen, lead_steps = lead
    done = [0] * len(others)
    for r in range(1, lead_steps + 1):
        next(lead_gen, None)
        for j, (gen, steps) in enumerate(others):
            target = -(-steps * r // lead_steps)
            while done[j] < target:
                next(gen, None)
                done[j] += 1
    for gen in [lead_gen] + [g for g, _ in others]:
        for _ in gen:
            pass


def _mixer_kernel(TB, QG, NT, NB, fused, *refs):
    n_in = (3 if fused else 1) + 8
    n_par = 8
    n_out = 9
    ins, params = refs[:n_in], refs[n_in:n_in + n_par]
    outs = refs[n_in + n_par:n_in + n_par + n_out]
    scratch = refs[n_in + n_par + n_out:]
    streams, finals = [], []
    for bb in range(NB):
        if fused:
            x_ref, g_ref, win_ref = ins[:3]
            hb = _rms(x_ref[bb], g_ref[...]).astype(MXU_DTYPE)
            states_in = ins[3:]

            def useg(start, width, hb=hb, win_ref=win_ref):
                return _mm(hb, win_ref[:, start:start + width])
        else:
            u_ref = ins[0].at[bb]
            states_in = ins[1:]

            def useg(start, width, u_ref=u_ref):
                return u_ref[:, start:start + width]

        seq_streams, finish = _mixer_sequence(TB, QG, NT, NB == 1, useg, *[r.at[bb] for r in states_in], *params,
                                              *[r.at[bb] for r in outs], *[r.at[bb] for r in scratch])
        streams.append(seq_streams)
        finals.append(finish)
    if NB == 1:
        gdn_stream, s5_stream, ssd_stream, lru_stream = streams[0]
        next(gdn_stream)
        next(gdn_stream)
        n_lead = 2 * (int(math.log2(QG)) - 1) * GDN_HEADS + GDN_HEADS + 2 * (TB // QG) + 2
        n_groups = TB // SUBLANE + 3
        _interleave_paced((gdn_stream, n_lead), [(s5_stream, n_groups), (lru_stream, n_groups), (ssd_stream, 8)])
    else:
        _interleave([seq[j] for j in range(4) for seq in streams])
    for finish in finals:
        finish()


def _mixer_sequence(TB, QG, NT, paced, useg,
                  lru_h0, lru_b0, ssd_s0, ssd_b0, gdn_s0, gdn_b0, s5r0, s5i0,
                  v256, v768, v128, v1024, wri, bmat, cmat, wglu,
                  mix_ref, lru_h_o, lru_b_o, ssd_s_o, ssd_b_o, gdn_s_o, gdn_b_o, s5r_o, s5i_o,
                  xp_lru, xp_ssd, xp_gdn, h_lru, s_ssd, s_gdn, x_s5, hs_scr, x5_scr):
    t = pl.program_id(1)
    K1 = CONV_WIDTH - 1
    NG = TB // SUBLANE

    @pl.when(t == 0)
    def _init():
        for xp, b0 in ((xp_lru, lru_b0), (xp_ssd, ssd_b0), (xp_gdn, gdn_b0)):
            xp[pl.ds(0, HIST), :] = jnp.zeros((HIST, xp.shape[1]), F32)
            xp[pl.ds(HIST - K1, K1), :] = b0[...]
        h_lru[...] = jnp.zeros(h_lru.shape, F32)
        h_lru[0:1, :] = lru_h0[...]
        s_ssd[...] = ssd_s0[...]
        s_gdn[...] = gdn_s0[...]
        x_s5[...] = jnp.zeros(x_s5.shape, F32)
        x_s5[0, SUBLANE - 1:SUBLANE, :] = s5r0[...]
        x_s5[1, SUBLANE - 1:SUBLANE, :] = s5i0[...]

    row128 = lax.broadcasted_iota(jnp.int32, (TB, LANE), 0)
    lane128 = lax.broadcasted_iota(jnp.int32, (TB, LANE), 1)
    small = useg(U_SMALL, LANE)
    sp = jax.nn.softplus(small + v128[0:1, :])
    sig = jax.nn.sigmoid(small)
    gcum = sp * v128[1:2, :]
    rk = jnp.where(lane128 < SM_BETA, row128, row128 & (QG - 1))
    k = 1
    while k < TB:
        gcum = gcum + jnp.where(rk >= k, pltpu.roll(gcum, k, 0), 0.0)
        k *= 2
    if TB < LANE:
        gcum_t = jnp.concatenate([gcum, jnp.zeros((LANE - TB, LANE), F32)], axis=0).T
    else:
        gcum_t = gcum.T
    rr = lax.broadcasted_iota(jnp.int32, (TB, TB), 0)
    cc = lax.broadcasted_iota(jnp.int32, (TB, TB), 1)
    sync = [None]

    def mark(x):
        if paced:
            sync[0] = x

    def lru():
        u_lru = useg(U_GATE, 2 * LRU_WIDTH)
        gate = u_lru[:, 0:LRU_WIDTH]
        xc = _conv_block(xp_lru, u_lru[:, LRU_WIDTH:2 * LRU_WIDTH], v256, 0, TB) + v256[4:5, :]
        ri = _mm(xc, wri[...])
        yield
        r = jax.nn.sigmoid(ri[:, :LRU_WIDTH] + v256[5:6, :])
        ig = jax.nn.sigmoid(ri[:, LRU_WIDTH:] + v256[6:7, :])
        log_a = (-LRU_C) * r * v256[7:8, :]
        a = jnp.exp(log_a)
        bt = jnp.sqrt(-jnp.tanh(log_a) * (a * a + 1.0)) * (ig * xc)
        yield
        row8 = lax.broadcasted_iota(jnp.int32, (SUBLANE, LRU_WIDTH), 0)
        h_prev = h_lru[0:1, :]
        for i in range(NG):
            a_g = a[i * SUBLANE:(i + 1) * SUBLANE]
            b_g = bt[i * SUBLANE:(i + 1) * SUBLANE]
            k = 1
            while k < SUBLANE:
                b_g = b_g + a_g * _shift_rows(b_g, k, row8, 0.0)
                a_g = a_g * _shift_rows(a_g, k, row8, 1.0)
                k *= 2
            h_g = _after(b_g, sync[0]) + a_g * h_prev
            h_prev = h_g[SUBLANE - 1:SUBLANE, :]
            hs_scr[pl.ds(i * SUBLANE, SUBLANE), :] = h_g
            if paced or i % 4 == 3:
                yield
        h_lru[0:1, :] = h_prev
        mix_ref[:, 0:LRU_WIDTH] = hs_scr[...] * _gelu(gate)

    def ssd():
        u_ssd = useg(U_SSDZ, SSD_INNER + SSD_CONV_DIM)
        z = u_ssd[:, 0:SSD_INNER]
        xbc = _silu(_conv_block(xp_ssd, u_ssd[:, SSD_INNER:SSD_INNER + SSD_CONV_DIM], v768, 0, TB) + v768[4:5, :])
        yield
        causal = rr >= cc
        lo = lane128 < (LANE // 2)
        row_s = lax.broadcasted_iota(jnp.int32, (LANE, SSD_STATE), 0) < (LANE // 2)
        y_pairs = []
        for p in range(SSD_GROUPS):
            xs_p = xbc[:, p * LANE:(p + 1) * LANE]
            b_p = xbc[:, SSD_INNER + p * SSD_STATE:SSD_INNER + (p + 1) * SSD_STATE]
            c_p = xbc[:, SSD_INNER + SSD_GROUPS * SSD_STATE + p * SSD_STATE:
                      SSD_INNER + SSD_GROUPS * SSD_STATE + (p + 1) * SSD_STATE]
            cb = _mm_nt(c_p, b_p)
            h0, h1 = 2 * p, 2 * p + 1
            xdt = xs_p * jnp.where(lo, sp[:, SM_DT + h0:SM_DT + h0 + 1], sp[:, SM_DT + h1:SM_DT + h1 + 1])
            s_pair = s_ssd[p]
            ys = []
            for h in (h0, h1):
                gcol = gcum[:, SM_DT + h:SM_DT + h + 1]
                grow = gcum_t[SM_DT + h:SM_DT + h + 1, 0:TB]
                seg = jnp.exp(jnp.where(causal, gcol - grow, -jnp.inf))
                y_diag = _mm(cb * seg, xdt)
                y_off = _mm_nt(c_p * jnp.exp(gcol), s_pair)
                ys.append(y_diag + y_off)
                yield
            y_pairs.append(jnp.where(lo, ys[0], ys[1]))
            g0 = gcum[:, SM_DT + h0:SM_DT + h0 + 1]
            g1 = gcum[:, SM_DT + h1:SM_DT + h1 + 1]
            gl0 = g0[TB - 1:TB, :]
            gl1 = g1[TB - 1:TB, :]
            to_end = jnp.where(lo, jnp.exp(gl0 - g0), jnp.exp(gl1 - g1))
            s_ssd[p] = jnp.where(row_s, jnp.exp(gl0), jnp.exp(gl1)) * s_pair + _mm_tn(xdt * to_end, b_p)
            yield
        y = jnp.concatenate(y_pairs, axis=-1) + v256[8:9, :] * xbc[:, 0:SSD_INNER]
        yg = y * _silu(z)
        gs = SSD_INNER // SSD_GROUPS
        for p in range(SSD_GROUPS):
            mix_ref[:, LRU_WIDTH + p * gs:LRU_WIDTH + (p + 1) * gs] = _rms(
                yg[:, p * gs:(p + 1) * gs], v256[9:10, p * gs:(p + 1) * gs])

    def gdn():
        lane = lax.broadcasted_iota(jnp.int32, (TB, GDN_WIDTH), 1)
        hmask = [(lane >= h * GDN_HEAD_DIM) & (lane < (h + 1) * GDN_HEAD_DIM) for h in range(GDN_HEADS)]

        def by_head(vals):
            return jnp.where(hmask[0], vals[0], jnp.where(hmask[1], vals[1], jnp.where(hmask[2], vals[2], vals[3])))

        def head_sum(x):
            return by_head([jnp.sum(jnp.where(m, x, 0.0), axis=-1, keepdims=True) for m in hmask])

        u_gdn = useg(U_QKV, GDN_CONV_DIM + GDN_WIDTH)
        qkv = _silu(_conv_block(xp_gdn, u_gdn[:, 0:GDN_CONV_DIM], v768, 5, TB))
        qf = qkv[:, 0:GDN_WIDTH]
        kf = qkv[:, GDN_WIDTH:2 * GDN_WIDTH]
        vf = qkv[:, 2 * GDN_WIDTH:3 * GDN_WIDTH]
        qf = qf * lax.rsqrt(head_sum(qf * qf) + 1e-6) * (GDN_HEAD_DIM ** -0.5)
        kf = kf * lax.rsqrt(head_sum(kf * kf) + 1e-6)
        beta_f = by_head([sig[:, SM_BETA + h:SM_BETA + h + 1] for h in range(GDN_HEADS)])
        gc_f = by_head([gcum[:, SM_A + h:SM_A + h + 1] for h in range(GDN_HEADS)])
        chunk_shift = int(math.log2(QG))
        same_chunk = lax.shift_right_logical(rr, chunk_shift) == lax.shift_right_logical(cc, chunk_shift)
        incl = same_chunk & (rr >= cc)
        strict = same_chunk & (rr > cc)
        eg = jnp.exp(gc_f)
        kq = _mm_nt(jnp.concatenate([jnp.where(m, kf, 0.0) for m in hmask] + [jnp.where(m, qf, 0.0) for m in hmask],
                                    axis=0), kf)
        rhs = jnp.concatenate([beta_f * vf, beta_f * eg * kf], axis=1)
        yield
        decs, pws, rs = [], [], []
        for h in range(GDN_HEADS):
            gcol = gcum[:, SM_A + h:SM_A + h + 1]
            grow = gcum_t[SM_A + h:SM_A + h + 1, 0:TB]
            dec = jnp.exp(jnp.where(incl, gcol - grow, -jnp.inf))
            low = jnp.where(strict, sig[:, SM_BETA + h:SM_BETA + h + 1] * kq[h * TB:(h + 1) * TB] * dec, 0.0)
            decs.append(dec)
            pws.append(-low)
            rs.append(-low)
        mark(pws[GDN_HEADS - 1])
        yield
        for _ in range(chunk_shift - 1):
            for h in range(GDN_HEADS):
                pws[h] = _mm(pws[h], pws[h])
                mark(pws[h])
                if paced:
                    yield
                rs[h] = rs[h] + pws[h] + _mm(rs[h], pws[h])
                mark(rs[h])
                yield
        sols, qks = [], []
        for h in range(GDN_HEADS):
            sols.append(rhs + _mm(rs[h], rhs))
            qks.append(kq[(GDN_HEADS + h) * TB:(GDN_HEADS + h + 1) * TB] * decs[h])
            yield
        u_all = by_head([s[:, 0:GDN_WIDTH] for s in sols])
        w_all = by_head([s[:, GDN_WIDTH:2 * GDN_WIDTH] for s in sols])
        q_dec = qf * eg
        head_shift = int(math.log2(GDN_HEAD_DIM))
        r2 = lax.shift_right_logical(lax.broadcasted_iota(jnp.int32, (GDN_WIDTH, GDN_WIDTH), 0), head_shift)
        c2 = lax.shift_right_logical(lax.broadcasted_iota(jnp.int32, (GDN_WIDTH, GDN_WIDTH), 1), head_shift)
        blockdiag = r2 == c2
        s_bd = s_gdn[...]
        deltas, o_state = [], []
        for c in range(TB // QG):
            sl = slice(c * QG, (c + 1) * QG)
            ws = _mm(jnp.concatenate([w_all[sl], q_dec[sl]], axis=0), s_bd)
            delta = u_all[sl] - ws[0:QG]
            o_state.append(ws[QG:2 * QG])
            mark(delta)
            yield
            g_c = gc_f[sl]
            g_last = g_c[QG - 1:QG, :]
            s_bd = s_bd * jnp.exp(g_last) + jnp.where(blockdiag, _mm_tn(kf[sl] * jnp.exp(g_last - g_c), delta), 0.0)
            deltas.append(delta)
            yield
        s_gdn[...] = s_bd
        delta_all = jnp.concatenate(deltas, axis=0)
        o = jnp.concatenate(o_state, axis=0) + by_head([_mm(qks[h], delta_all) for h in range(GDN_HEADS)])
        yield
        ms = head_sum(o * o) * (1.0 / GDN_HEAD_DIM)
        o = o * lax.rsqrt(ms + RMS_EPS) * v256[10:11, :]
        zg = u_gdn[:, GDN_CONV_DIM:GDN_CONV_DIM + GDN_WIDTH]
        mix_ref[:, 2 * LRU_WIDTH:2 * LRU_WIDTH + GDN_WIDTH] = o * _silu(zg)

    def s5():
        row_n = lax.broadcasted_iota(jnp.int32, (SUBLANE, S5_N), 0)
        us5 = useg(U_S5, S5_WIDTH)
        bu = _mm(us5, bmat[...])
        yield
        lam_r, lam_i = v1024[0:1, :], v1024[1:2, :]
        pows = []
        pr, pi = lam_r, lam_i
        k = 1
        while k < SUBLANE:
            pows.append((k, jnp.where(row_n >= k, pr, 0.0), jnp.where(row_n >= k, pi, 0.0)))
            pr, pi = pr * pr - pi * pi, 2.0 * (pr * pi)
            k *= 2
        in_r = jnp.where(row_n == 0, lam_r, 0.0)
        in_i = jnp.where(row_n == 0, lam_i, 0.0)
        g_r = x_s5[0]
        g_i = x_s5[1]
        for i in range(NG):
            p_r = _after(pltpu.roll(g_r, 1, 0), sync[0])
            p_i = _after(pltpu.roll(g_i, 1, 0), sync[0])
            g_r = bu[i * SUBLANE:(i + 1) * SUBLANE, 0:S5_N] + (in_r * p_r - in_i * p_i)
            g_i = bu[i * SUBLANE:(i + 1) * SUBLANE, S5_N:2 * S5_N] + (in_r * p_i + in_i * p_r)
            for k, pr, pi in pows:
                s_r = pltpu.roll(g_r, k, 0)
                s_i = pltpu.roll(g_i, k, 0)
                g_r, g_i = g_r + (pr * s_r - pi * s_i), g_i + (pr * s_i + pi * s_r)
            x5_scr[pl.ds(i * SUBLANE, SUBLANE), 0:S5_N] = g_r
            x5_scr[pl.ds(i * SUBLANE, SUBLANE), S5_N:2 * S5_N] = g_i
            yield
        x_s5[0] = g_r
        x_s5[1] = g_i
        y5 = _mm(x5_scr[...], cmat[...])
        y5 = _gelu(y5 + v256[11:12, :] * us5)
        yield
        mix_ref[:, 3 * LRU_WIDTH:3 * LRU_WIDTH + S5_WIDTH] = y5 * jax.nn.sigmoid(_mm(y5, wglu[...]) + v256[12:13, :])

    def finish():
        for xp in (xp_lru, xp_ssd, xp_gdn):
            xp[pl.ds(0, HIST), :] = xp[pl.ds(TB, HIST), :]

        pl.when(t == NT - 1)(_emit)

    def _emit():
        lru_h_o[...] = h_lru[0:1, :]
        lru_b_o[...] = xp_lru[pl.ds(HIST - K1, K1), :]
        ssd_b_o[...] = xp_ssd[pl.ds(HIST - K1, K1), :]
        gdn_b_o[...] = xp_gdn[pl.ds(HIST - K1, K1), :]
        ssd_s_o[...] = s_ssd[...]
        gdn_s_o[...] = s_gdn[...]
        s5r_o[...] = x_s5[0, SUBLANE - 1:SUBLANE, :]
        s5i_o[...] = x_s5[1, SUBLANE - 1:SUBLANE, :]

    return [gdn(), s5(), ssd(), lru()], finish


def _mixer(src, states, ls, pk, l, fused):
    B, T, _ = src.shape
    TB = min(T, MIXER_BLOCK)
    QG = min(TB, 64)
    NT = T // TB
    NB = SHORT_SEQ_BATCH if (B % SHORT_SEQ_BATCH == 0 and TB < MIXER_BLOCK) else 1

    def per_b(shape):
        nd = len(shape)
        return pl.BlockSpec((NB,) + tuple(shape[1:]), lambda b, t: (b,) + (0,) * (nd - 1))

    def per_b_of_layer(shape):
        nd = len(shape)
        return pl.BlockSpec((None, NB) + tuple(shape[2:]), lambda b, t: (ls, b) + (0,) * (nd - 2))

    def per_l(arr):
        nd = arr.ndim
        return pl.BlockSpec((None,) + tuple(arr.shape[1:]), lambda b, t: (l,) + (0,) * (nd - 1))

    proj = (pk["g_mix"], pk["w_in"]) if fused else ()
    params = (pk["v256"], pk["v768"], pk["v128"], pk["v1024"], pk["wri"], pk["bmat"], pk["cmat"], pk["wglu"])
    state_shapes = [s.shape[1:] for s in states]
    out_shape = [jax.ShapeDtypeStruct((B, T, D_MODEL), F32)] + [jax.ShapeDtypeStruct(s, F32) for s in state_shapes]
    outs = pl.pallas_call(
        functools.partial(_mixer_kernel, TB, QG, NT, NB, fused),
        grid=(B // NB, NT),
        in_specs=[pl.BlockSpec((NB, TB, src.shape[2]), lambda b, t: (b, t, 0))] + [per_l(p) for p in proj]
                 + [per_b_of_layer(s.shape) for s in states] + [per_l(p) for p in params],
        out_specs=[pl.BlockSpec((NB, TB, D_MODEL), lambda b, t: (b, t, 0))] + [per_b(s) for s in state_shapes],
        out_shape=out_shape,
        scratch_shapes=[pltpu.VMEM((NB, TB + HIST, LRU_WIDTH), F32),
                        pltpu.VMEM((NB, TB + HIST, SSD_CONV_DIM), F32),
                        pltpu.VMEM((NB, TB + HIST, GDN_CONV_DIM), F32),
                        pltpu.VMEM((NB, SUBLANE, LRU_WIDTH), F32),
                        pltpu.VMEM((NB, SSD_GROUPS, LANE, SSD_STATE), F32),
                        pltpu.VMEM((NB, GDN_WIDTH, GDN_WIDTH), F32),
                        pltpu.VMEM((NB, 2, SUBLANE, S5_N), F32),
                        pltpu.VMEM((NB, TB, LRU_WIDTH), F32),
                        pltpu.VMEM((NB, TB, 2 * S5_N), F32)],
        compiler_params=pltpu.CompilerParams(dimension_semantics=("arbitrary", "arbitrary"),
                                             vmem_limit_bytes=VMEM_LIMIT),
        name="mixer",
    )(src, *proj, *states, *params)
    return outs[0], tuple(outs[1:])


def _attend(q, k_ref, v_ref):
    heads = []
    for h in range(MEM_HEADS):
        sl = slice(h * MEM_HEAD_DIM, (h + 1) * MEM_HEAD_DIM)
        s = _mm_nt(q[:, sl], k_ref[:, sl]) * (MEM_HEAD_DIM ** -0.5)
        e = jnp.exp(s - jnp.max(s, axis=-1, keepdims=True))
        p = e / jnp.sum(e, axis=-1, keepdims=True)
        heads.append(_mm(p, v_ref[:, sl]))
    return jnp.concatenate(heads, axis=-1)


def _attn_kernel(x_ref, mix_ref, wout, g_ref, wq, k_ref, v_ref, wo, o_ref):
    x1 = x_ref[0] + _mm(mix_ref[0], wout[...])
    q = _mm(_rms(x1, g_ref[...]), wq[...])
    o_ref[0] = x1 + _mm(_attend(q, k_ref, v_ref), wo[...])


def _attn_short_kernel(T, NB, x_ref, mix_ref, wout, g_ref, wq, k_ref, v_ref, wo, o_ref, x1_scr, q_scr, a_scr):
    b = pl.program_id(0)

    @pl.when(b == 0)
    def _project_in():
        x1 = x_ref[...] + _mm(mix_ref[...], wout[...])
        x1_scr[...] = x1
        q_scr[...] = _mm(_rms(x1, g_ref[...]), wq[...])

    for bb in range(NB):
        rows = pl.ds(pl.multiple_of((b * NB + bb) * T, T), T)
        a_scr[rows, :] = _attend(q_scr[rows, :], k_ref.at[bb], v_ref.at[bb])

    @pl.when(b == pl.num_programs(0) - 1)
    def _project_out():
        o_ref[...] = x1_scr[...] + _mm(a_scr[...], wo[...])


def _attn(x, mix, mem_k, mem_v, lkv, pk, l):
    B, T, _ = x.shape

    def per_l(arr):
        nd = arr.ndim
        return pl.BlockSpec((None,) + tuple(arr.shape[1:]), lambda *i: (l,) + (0,) * (nd - 1),
                            pipeline_mode=pl.Buffered(1))

    weights = (pk["w_out"], pk["g_cross"], pk["w_cq"])
    if T >= MIXER_BLOCK:
        TB = min(T, ROW_TILE)
        blk = pl.BlockSpec((1, TB, D_MODEL), lambda b, t: (b, t, 0))
        kv = pl.BlockSpec((None, None, MEM_TOKENS, D_MODEL), lambda b, t: (lkv, b, 0, 0))
        return pl.pallas_call(
            _attn_kernel,
            grid=(B, T // TB),
            in_specs=[blk, blk] + [per_l(w) for w in weights] + [kv, kv, per_l(pk["w_co"])],
            out_specs=blk,
            out_shape=jax.ShapeDtypeStruct((B, T, D_MODEL), F32),
            compiler_params=pltpu.CompilerParams(dimension_semantics=("parallel", "parallel"),
                                                 vmem_limit_bytes=VMEM_LIMIT),
            name="attn",
        )(x, mix, *weights, mem_k, mem_v, pk["w_co"])
    R = B * T
    NB = SHORT_SEQ_BATCH // 2 if B % (SHORT_SEQ_BATCH // 2) == 0 else 1
    rows = pl.BlockSpec((R, D_MODEL), lambda b: (0, 0))
    kv = pl.BlockSpec((None, NB, MEM_TOKENS, D_MODEL), lambda b: (lkv, b, 0, 0))
    out = pl.pallas_call(
        functools.partial(_attn_short_kernel, T, NB),
        grid=(B // NB,),
        in_specs=[rows, rows] + [per_l(w) for w in weights] + [kv, kv, per_l(pk["w_co"])],
        out_specs=rows,
        out_shape=jax.ShapeDtypeStruct((R, D_MODEL), F32),
        scratch_shapes=[pltpu.VMEM((R, D_MODEL), F32)] * 3,
        compiler_params=pltpu.CompilerParams(dimension_semantics=("arbitrary",), vmem_limit_bytes=VMEM_LIMIT),
        name="attn_short",
    )(x.reshape(R, D_MODEL), mix.reshape(R, D_MODEL), *weights, mem_k, mem_v, pk["w_co"])
    return out.reshape(B, T, D_MODEL)


def _ffn_kernel(final, x_ref, g_ref, wg, wu, wd, gf_ref, o_ref):
    x = x_ref[...]
    h = _rms(x, g_ref[...]).astype(MXU_DTYPE)
    y = x + _mm(_silu(_mm(h, wg[...])) * _mm(h, wu[...]), wd[...])
    if final:
        y = _rms(y, gf_ref[...])
    o_ref[...] = y


def _ffn(x2d, pk, l, final):
    R = x2d.shape[0]
    TM = min(R, ROW_TILE)

    def per_l(arr):
        nd = arr.ndim
        return pl.BlockSpec((None,) + tuple(arr.shape[1:]), lambda i: (l,) + (0,) * (nd - 1),
                            pipeline_mode=pl.Buffered(1))

    blk = pl.BlockSpec((TM, D_MODEL), lambda i: (i, 0))
    return pl.pallas_call(
        functools.partial(_ffn_kernel, final),
        grid=(R // TM,),
        in_specs=[blk, per_l(pk["g_ffn"]), per_l(pk["w_gate"]), per_l(pk["w_up"]), per_l(pk["w_down"]),
                  pl.BlockSpec((1, D_MODEL), lambda i: (0, 0))],
        out_specs=blk,
        out_shape=jax.ShapeDtypeStruct((R, D_MODEL), F32),
        compiler_params=pltpu.CompilerParams(dimension_semantics=("parallel",), vmem_limit_bytes=VMEM_LIMIT),
        name="ffn",
    )(x2d, pk["g_ffn"], pk["w_gate"], pk["w_up"], pk["w_down"], pk["g_final"])


def _memkv_kernel(m_ref, g_ref, wk, wv, k_ref, v_ref):
    h = _rms(m_ref[0], g_ref[...]).astype(MXU_DTYPE)
    k_ref[0] = _mm(h, wk[...])
    v_ref[0] = _mm(h, wv[...])


def _memory_kv(mem, pk, l):
    B = mem.shape[0]

    def per_l(arr):
        nd = arr.ndim
        return pl.BlockSpec((None,) + tuple(arr.shape[1:]), lambda b: (l,) + (0,) * (nd - 1))

    blk = pl.BlockSpec((1, MEM_TOKENS, D_MODEL), lambda b: (b, 0, 0))
    return pl.pallas_call(
        _memkv_kernel,
        grid=(B,),
        in_specs=[blk, per_l(pk["g_mem"]), per_l(pk["w_ck"]), per_l(pk["w_cv"])],
        out_specs=[blk, blk],
        out_shape=[jax.ShapeDtypeStruct((B, MEM_TOKENS, D_MODEL), F32)] * 2,
        compiler_params=pltpu.CompilerParams(dimension_semantics=("parallel",), vmem_limit_bytes=VMEM_LIMIT),
        name="memory_kv",
    )(mem, pk["g_mem"], pk["w_ck"], pk["w_cv"])


def _block_diag(w):
    L, n, a, b = w.shape
    return jnp.einsum('lnab,nm->lnamb', w, jnp.eye(n, dtype=w.dtype)).reshape(L, n * a, n * b)


def _pack_rows(rows, n_rows):
    L, width = rows[0].shape[0], rows[0].shape[-1]
    rows = [r.reshape(L, -1, width) for r in rows]
    used = sum(r.shape[1] for r in rows)
    return jnp.concatenate(rows + [jnp.zeros((L, n_rows - used, width), F32)], axis=1)


def _prepare(norm_mix_g, w_in, w_out, lru_conv_w, lru_conv_b, lru_w_r, lru_b_r, lru_w_i, lru_b_i, lru_lambda,
             ssd_conv_w, ssd_conv_b, ssd_dt_bias, ssd_a_log, ssd_d, ssd_norm_g,
             gdn_conv_w, gdn_dt_bias, gdn_a_log, gdn_norm_g,
             s5_a_re, s5_a_im, s5_log_dt, s5_b_re, s5_b_im, s5_c_re, s5_c_im, s5_d, s5_w_glu, s5_b_glu,
             norm_mem_g, norm_cross_g, w_cq, w_ck, w_cv, w_co,
             norm_ffn_g, w_ffn_gate, w_ffn_up, w_ffn_down, norm_final_g):
    L = DEPTH
    bf = MXU_DTYPE
    pk = {}
    pk["g_mix"] = norm_mix_g.reshape(L, 1, D_MODEL)
    pk["w_in"] = jnp.concatenate(
        [w_in[:, :, 0:1536], w_in[:, :, 1540:2564], w_in[:, :, 2572:2828], w_in[:, :, 1536:1540],
         w_in[:, :, 2564:2572], jnp.zeros((L, D_MODEL, U_COLS - IN_COLS), F32)], axis=2).astype(bf)
    pk["w_out"] = w_out.astype(bf)
    sp_lam = jax.nn.softplus(-lru_lambda)
    pk["v256"] = _pack_rows(
        [lru_conv_w, lru_conv_b, lru_b_r, lru_b_i, sp_lam, jnp.repeat(ssd_d, SSD_INNER // SSD_HEADS, axis=1),
         ssd_norm_g, jnp.tile(gdn_norm_g, (1, GDN_HEADS)), s5_d, s5_b_glu], 16)
    pk["v768"] = _pack_rows([ssd_conv_w, ssd_conv_b, gdn_conv_w], 16)
    zeros4 = jnp.zeros((L, 4), F32)
    bias = jnp.concatenate([ssd_dt_bias, zeros4, gdn_dt_bias, jnp.zeros((L, LANE - 12), F32)], axis=1)
    avec = jnp.concatenate([-jnp.exp(ssd_a_log.astype(F32)), zeros4, -jnp.exp(gdn_a_log),
                            jnp.zeros((L, LANE - 12), F32)], axis=1)
    pk["v128"] = _pack_rows([bias, avec], 8)
    pk["wri"] = jnp.concatenate([_block_diag(lru_w_r), _block_diag(lru_w_i)], axis=-1).astype(bf)
    a_re = s5_a_re.astype(F32)
    a_im = s5_a_im.astype(F32)
    step = jnp.exp(s5_log_dt.astype(F32))[:, :, None]
    mag = jnp.exp(a_re * step)
    ang = a_im * step
    lb_re, lb_im = mag * jnp.cos(ang), mag * jnp.sin(ang)
    den = a_re * a_re + a_im * a_im
    f_re = ((lb_re - 1.0) * a_re + lb_im * a_im) / den
    f_im = (lb_im * a_re - (lb_re - 1.0) * a_im) / den
    b_re = s5_b_re.astype(F32)
    b_im = s5_b_im.astype(F32)
    bb_re = f_re[..., None] * b_re - f_im[..., None] * b_im
    bb_im = f_re[..., None] * b_im + f_im[..., None] * b_re
    pk["v1024"] = _pack_rows([lb_re.reshape(L, S5_N), lb_im.reshape(L, S5_N)], 8)
    pk["bmat"] = jnp.concatenate([_block_diag(jnp.swapaxes(bb_re, 2, 3)), _block_diag(jnp.swapaxes(bb_im, 2, 3))],
                                 axis=-1).astype(bf)
    pk["cmat"] = jnp.concatenate([_block_diag(jnp.swapaxes(s5_c_re, 2, 3)), -_block_diag(jnp.swapaxes(s5_c_im, 2, 3))],
                                 axis=1).astype(bf)
    pk["wglu"] = s5_w_glu.astype(bf)
    pk["g_mem"] = norm_mem_g.reshape(L, 1, D_MODEL)
    pk["g_cross"] = norm_cross_g.reshape(L, 1, D_MODEL)
    pk["w_cq"], pk["w_ck"], pk["w_cv"], pk["w_co"] = (w.astype(bf) for w in (w_cq, w_ck, w_cv, w_co))
    pk["g_ffn"] = norm_ffn_g.reshape(L, 1, D_MODEL)
    pk["w_gate"], pk["w_up"], pk["w_down"] = (w.astype(bf) for w in (w_ffn_gate, w_ffn_up, w_ffn_down))
    pk["g_final"] = norm_final_g.reshape(1, D_MODEL)
    return pk


def _states_to_kernel(lru_h, lru_buf, ssd, ssd_buf, gdn, gdn_buf, s5_re, s5_im):
    L, B = lru_h.shape[:2]
    eye = jnp.eye(GDN_HEADS, dtype=F32)
    gdn_bd = jnp.einsum('lbhkv,hg->lbhkgv', gdn, eye).reshape(L, B, GDN_WIDTH, GDN_WIDTH)
    return (lru_h.reshape(L, B, 1, LRU_WIDTH), lru_buf, ssd.reshape(L, B, SSD_GROUPS, LANE, SSD_STATE), ssd_buf,
            gdn_bd, gdn_buf, s5_re.reshape(L, B, 1, S5_N), s5_im.reshape(L, B, 1, S5_N))


def _states_from_kernel(per_layer):
    lru_h, lru_buf, ssd, ssd_buf, gdn_bd, gdn_buf, s5_re, s5_im = (
        jnp.stack([st[j] for st in per_layer], axis=0) for j in range(8))
    L, B = lru_h.shape[:2]
    gdn = jnp.einsum('lbhkhv->lbhkv', gdn_bd.reshape(L, B, GDN_HEADS, GDN_HEAD_DIM, GDN_HEADS, GDN_HEAD_DIM))
    return (lru_h.reshape(L, B, LRU_WIDTH), lru_buf, ssd.reshape(L, B, SSD_HEADS, SSD_INNER // SSD_HEADS, SSD_STATE),
            ssd_buf, gdn, gdn_buf, s5_re.reshape(L, B, S5_GROUPS, S5_STATE), s5_im.reshape(L, B, S5_GROUPS, S5_STATE))


def _layer(l, x, mem_k, mem_v, lkv, states, ls, pk, final):
    B, T, _ = x.shape
    if T >= MIXER_BLOCK:
        mix, new_states = _mixer(x, states, ls, pk, l, True)
    else:
        u = _in_proj(x.reshape(B * T, D_MODEL), pk["g_mix"], pk["w_in"], l).reshape(B, T, U_COLS)
        mix, new_states = _mixer(u, states, ls, pk, l, False)
    x = _attn(x, mix, mem_k, mem_v, lkv, pk, l)
    x = _ffn(x.reshape(B * T, D_MODEL), pk, l, final).reshape(B, T, D_MODEL)
    return x, new_states


def kernel(x_prompt, x_sample, mem_prompt, state_lru_h, cache_lru_conv, state_ssd, cache_ssd_conv, state_gdn, cache_gdn_conv, state_s5_re, state_s5_im, cache_mem_k, cache_mem_v, norm_mix_g, w_in, w_out, lru_conv_w, lru_conv_b, lru_w_r, lru_b_r, lru_w_i, lru_b_i, lru_lambda, ssd_conv_w, ssd_conv_b, ssd_dt_bias, ssd_a_log, ssd_d, ssd_norm_g, gdn_conv_w, gdn_dt_bias, gdn_a_log, gdn_norm_g, s5_a_re, s5_a_im, s5_log_dt, s5_b_re, s5_b_im, s5_c_re, s5_c_im, s5_d, s5_w_glu, s5_b_glu, norm_mem_g, norm_cross_g, w_cq, w_ck, w_cv, w_co, norm_ffn_g, w_ffn_gate, w_ffn_up, w_ffn_down, norm_final_g):
    pk = _prepare(norm_mix_g, w_in, w_out, lru_conv_w, lru_conv_b, lru_w_r, lru_b_r, lru_w_i, lru_b_i, lru_lambda,
                  ssd_conv_w, ssd_conv_b, ssd_dt_bias, ssd_a_log, ssd_d, ssd_norm_g,
                  gdn_conv_w, gdn_dt_bias, gdn_a_log, gdn_norm_g,
                  s5_a_re, s5_a_im, s5_log_dt, s5_b_re, s5_b_im, s5_c_re, s5_c_im, s5_d, s5_w_glu, s5_b_glu,
                  norm_mem_g, norm_cross_g, w_cq, w_ck, w_cv, w_co,
                  norm_ffn_g, w_ffn_gate, w_ffn_up, w_ffn_down, norm_final_g)
    bp = x_prompt.shape[0]
    bs = x_sample.shape[0]
    k1 = CONV_WIDTH - 1
    zero_states = _states_to_kernel(
        jnp.zeros((1, bp, LRU_WIDTH), F32), jnp.zeros((1, bp, k1, LRU_WIDTH), F32),
        jnp.zeros((1, bp, SSD_HEADS, SSD_INNER // SSD_HEADS, SSD_STATE), F32), jnp.zeros((1, bp, k1, SSD_CONV_DIM), F32),
        jnp.zeros((1, bp, GDN_HEADS, GDN_HEAD_DIM, GDN_HEAD_DIM), F32), jnp.zeros((1, bp, k1, GDN_CONV_DIM), F32),
        jnp.zeros((1, bp, S5_GROUPS, S5_STATE), F32), jnp.zeros((1, bp, S5_GROUPS, S5_STATE), F32))
    sample_states = _states_to_kernel(state_lru_h, cache_lru_conv, state_ssd, cache_ssd_conv,
                                      state_gdn, cache_gdn_conv, state_s5_re, state_s5_im)
    cache_k = cache_mem_k.reshape(DEPTH, bs, MEM_TOKENS, D_MODEL)
    cache_v = cache_mem_v.reshape(DEPTH, bs, MEM_TOKENS, D_MODEL)
    xp, xs = x_prompt, x_sample
    p_states, s_states, p_mk, p_mv = [], [], [], []
    for l in range(DEPTH):
        final = l == DEPTH - 1
        mk, mv = _memory_kv(mem_prompt, pk, l)
        xp, sp = _layer(l, xp, mk[None], mv[None], 0, zero_states, 0, pk, final)
        xs, ss = _layer(l, xs, cache_k, cache_v, l, sample_states, l, pk, final)
        p_states.append(sp)
        s_states.append(ss)
        p_mk.append(mk.reshape(bp, MEM_TOKENS, MEM_HEADS, MEM_HEAD_DIM))
        p_mv.append(mv.reshape(bp, MEM_TOKENS, MEM_HEADS, MEM_HEAD_DIM))

    return ((xp, xs) + _states_from_kernel(p_states) + (jnp.stack(p_mk, axis=0), jnp.stack(p_mv, axis=0))
            + _states_from_kernel(s_states))
```

```python
import functools
import math

import jax
import jax.numpy as jnp
import numpy as np
from jax import lax
from jax.experimental import pallas as pl
from jax.experimental.pallas import tpu as pltpu

F32 = jnp.float32
MXU_DTYPE = jnp.bfloat16

D_MODEL = 1024
DEPTH = 4
CONV_WIDTH = 4
RMS_EPS = 1e-6
LRU_WIDTH = 256
LRU_BLOCKS = 4
LRU_C = 8.0
SSD_INNER = 256
SSD_HEADS = 4
SSD_GROUPS = 2
SSD_STATE = 128
SSD_CONV_DIM = 768
GDN_WIDTH = 256
GDN_HEAD_DIM = 64
GDN_HEADS = 4
GDN_CONV_DIM = 768
S5_WIDTH = 256
S5_GROUP_CH = 16
S5_GROUPS = 16
S5_STATE = 64
S5_N = S5_GROUPS * S5_STATE
MEM_TOKENS = 256
MEM_HEADS = 4
MEM_HEAD_DIM = 256
FFN_HIDDEN = 2816
IN_COLS = 2828

U_GATE, U_LRUX, U_SSDZ, U_XBC, U_QKV, U_GDNZ, U_S5, U_SMALL = 0, 256, 512, 768, 1536, 2304, 2560, 2816
U_COLS = 2944
SM_DT, SM_BETA, SM_A = 0, 4, 8

LANE = 128
SUBLANE = 8
VMEM_LIMIT = 56 * 1024 * 1024
HIST = SUBLANE
ROW_TILE = 1024
MIXER_BLOCK = 256
SHORT_SEQ_BATCH = 8


def _mm(a, b):
    return jnp.dot(a.astype(MXU_DTYPE), b.astype(MXU_DTYPE), preferred_element_type=F32)


def _mm_nt(a, b):
    return lax.dot_general(a.astype(MXU_DTYPE), b.astype(MXU_DTYPE), (((1,), (1,)), ((), ())),
                           preferred_element_type=F32)


def _mm_tn(a, b):
    return lax.dot_general(a.astype(MXU_DTYPE), b.astype(MXU_DTYPE), (((0,), (0,)), ((), ())),
                           preferred_element_type=F32)


def _rms(x, g):
    ms = jnp.mean(x * x, axis=-1, keepdims=True)
    return x * lax.rsqrt(ms + RMS_EPS) * g


def _silu(x):
    return x * jax.nn.sigmoid(x)


def _gelu(x):
    return jax.nn.gelu(x, approximate=True)


def _shift_rows(x, k, row, fill):
    return jnp.where(row >= k, pltpu.roll(x, k, 0), fill)


def _in_proj_kernel(x_ref, g_ref, w_ref, u_ref):
    h = _rms(x_ref[...], g_ref[...])
    u_ref[...] = _mm(h, w_ref[...])


def _in_proj(x2d, g_all, w_all, l):
    R = x2d.shape[0]
    TM = min(R, MIXER_BLOCK)
    return pl.pallas_call(
        _in_proj_kernel,
        grid=(R // TM,),
        in_specs=[pl.BlockSpec((TM, D_MODEL), lambda i: (i, 0)),
                  pl.BlockSpec((None, 1, D_MODEL), lambda i: (l, 0, 0)),
                  pl.BlockSpec((None, D_MODEL, U_COLS), lambda i: (l, 0, 0))],
        out_specs=pl.BlockSpec((TM, U_COLS), lambda i: (i, 0)),
        out_shape=jax.ShapeDtypeStruct((R, U_COLS), F32),
        compiler_params=pltpu.CompilerParams(dimension_semantics=("parallel",), vmem_limit_bytes=VMEM_LIMIT),
        name="in_proj",
    )(x2d, g_all, w_all)


def _conv_block(xp_ref, x, w_ref, w_row0, TB):
    xp_ref[pl.ds(HIST, TB), :] = x
    y = x * w_ref[w_row0 + CONV_WIDTH - 1:w_row0 + CONV_WIDTH, :]
    for j in range(CONV_WIDTH - 1):
        off = HIST - (CONV_WIDTH - 1) + j
        y = y + xp_ref[pl.ds(off, TB), :] * w_ref[w_row0 + j:w_row0 + j + 1, :]
    return y


def _interleave(streams):
    streams = list(streams)
    while streams:
        for s in list(streams):
            try:
                next(s)
            except StopIteration:
                streams.remove(s)


def _after(x, dep):
    if dep is None:
        return x
    z = dep[0:SUBLANE, 0:LANE] * 0.0
    return x + jnp.tile(z, (x.shape[0] // SUBLANE, x.shape[1] // LANE))


def _interleave_paced(lead, others):
    lead_gen, lead_steps = lead
    done = [0] * len(others)
    for r in range(1, lead_steps + 1):
        next(lead_gen, None)
        for j, (gen, steps) in enumerate(others):
            target = -(-steps * r // lead_steps)
            while done[j] < target:
                next(gen, None)
                done[j] += 1
    for gen in [lead_gen] + [g for g, _ in others]:
        for _ in gen:
            pass


def _mixer_kernel(TB, QG, NT, NB, fused, *refs):
    n_in = (3 if fused else 1) + 8
    n_par = 8
    n_out = 9
    ins, params = refs[:n_in], refs[n_in:n_in + n_par]
    outs = refs[n_in + n_par:n_in + n_par + n_out]
    scratch = refs[n_in + n_par + n_out:]
    streams, finals = [], []
    for bb in range(NB):
        if fused:
            x_ref, g_ref, win_ref = ins[:3]
            hb = _rms(x_ref[bb], g_ref[...]).astype(MXU_DTYPE)
            states_in = ins[3:]

            def useg(start, width, hb=hb, win_ref=win_ref):
                return _mm(hb, win_ref[:, start:start + width])
        else:
            u_ref = ins[0].at[bb]
            states_in = ins[1:]

            def useg(start, width, u_ref=u_ref):
                return u_ref[:, start:start + width]

        seq_streams, finish = _mixer_sequence(TB, QG, NT, NB == 1, useg, *[r.at[bb] for r in states_in], *params,
                                              *[r.at[bb] for r in outs], *[r.at[bb] for r in scratch])
        streams.append(seq_streams)
        finals.append(finish)
    if NB == 1:
        gdn_stream, s5_stream, ssd_stream, lru_stream = streams[0]
        next(gdn_stream)
        next(gdn_stream)
        n_lead = 2 * (int(math.log2(QG)) - 1) * GDN_HEADS + GDN_HEADS + 2 * (TB // QG) + 2
        n_groups = TB // SUBLANE + 3
        _interleave_paced((gdn_stream, n_lead), [(s5_stream, n_groups), (lru_stream, n_groups), (ssd_stream, 8)])
    else:
        _interleave([seq[j] for j in range(4) for seq in streams])
    for finish in finals:
        finish()


def _mixer_sequence(TB, QG, NT, paced, useg,
                  lru_h0, lru_b0, ssd_s0, ssd_b0, gdn_s0, gdn_b0, s5r0, s5i0,
                  v256, v768, v128, v1024, wri, bmat, cmat, wglu,
                  mix_ref, lru_h_o, lru_b_o, ssd_s_o, ssd_b_o, gdn_s_o, gdn_b_o, s5r_o, s5i_o,
                  xp_lru, xp_ssd, xp_gdn, h_lru, s_ssd, s_gdn, x_s5, hs_scr, x5_scr):
    t = pl.program_id(1)
    K1 = CONV_WIDTH - 1
    NG = TB // SUBLANE

    @pl.when(t == 0)
    def _init():
        for xp, b0 in ((xp_lru, lru_b0), (xp_ssd, ssd_b0), (xp_gdn, gdn_b0)):
            xp[pl.ds(0, HIST), :] = jnp.zeros((HIST, xp.shape[1]), F32)
            xp[pl.ds(HIST - K1, K1), :] = b0[...]
        h_lru[...] = jnp.zeros(h_lru.shape, F32)
        h_lru[0:1, :] = lru_h0[...]
        s_ssd[...] = ssd_s0[...]
        s_gdn[...] = gdn_s0[...]
        x_s5[...] = jnp.zeros(x_s5.shape, F32)
        x_s5[0, SUBLANE - 1:SUBLANE, :] = s5r0[...]
        x_s5[1, SUBLANE - 1:SUBLANE, :] = s5i0[...]

    row128 = lax.broadcasted_iota(jnp.int32, (TB, LANE), 0)
    lane128 = lax.broadcasted_iota(jnp.int32, (TB, LANE), 1)
    small = useg(U_SMALL, LANE)
    sp = jax.nn.softplus(small + v128[0:1, :])
    sig = jax.nn.sigmoid(small)
    gcum = sp * v128[1:2, :]
    rk = jnp.where(lane128 < SM_BETA, row128, row128 & (QG - 1))
    k = 1
    while k < TB:
        gcum = gcum + jnp.where(rk >= k, pltpu.roll(gcum, k, 0), 0.0)
        k *= 2
    if TB < LANE:
        gcum_t = jnp.concatenate([gcum, jnp.zeros((LANE - TB, LANE), F32)], axis=0).T
    else:
        gcum_t = gcum.T
    rr = lax.broadcasted_iota(jnp.int32, (TB, TB), 0)
    cc = lax.broadcasted_iota(jnp.int32, (TB, TB), 1)
    sync = [None]

    def mark(x):
        if paced:
            sync[0] = x

    def lru():
        u_lru = useg(U_GATE, 2 * LRU_WIDTH)
        gate = u_lru[:, 0:LRU_WIDTH]
        xc = _conv_block(xp_lru, u_lru[:, LRU_WIDTH:2 * LRU_WIDTH], v256, 0, TB) + v256[4:5, :]
        ri = _mm(xc, wri[...])
        yield
        r = jax.nn.sigmoid(ri[:, :LRU_WIDTH] + v256[5:6, :])
        ig = jax.nn.sigmoid(ri[:, LRU_WIDTH:] + v256[6:7, :])
        log_a = (-LRU_C) * r * v256[7:8, :]
        a = jnp.exp(log_a)
        bt = jnp.sqrt(-jnp.tanh(log_a) * (a * a + 1.0)) * (ig * xc)
        yield
        row8 = lax.broadcasted_iota(jnp.int32, (SUBLANE, LRU_WIDTH), 0)
        h_prev = h_lru[0:1, :]
        for i in range(NG):
            a_g = a[i * SUBLANE:(i + 1) * SUBLANE]
            b_g = bt[i * SUBLANE:(i + 1) * SUBLANE]
            k = 1
            while k < SUBLANE:
                b_g = b_g + a_g * _shift_rows(b_g, k, row8, 0.0)
                a_g = a_g * _shift_rows(a_g, k, row8, 1.0)
                k *= 2
            h_g = _after(b_g, sync[0]) + a_g * h_prev
            h_prev = h_g[SUBLANE - 1:SUBLANE, :]
            hs_scr[pl.ds(i * SUBLANE, SUBLANE), :] = h_g
            if paced or i % 4 == 3:
                yield
        h_lru[0:1, :] = h_prev
        mix_ref[:, 0:LRU_WIDTH] = hs_scr[...] * _gelu(gate)

    def ssd():
        u_ssd = useg(U_SSDZ, SSD_INNER + SSD_CONV_DIM)
        z = u_ssd[:, 0:SSD_INNER]
        xbc = _silu(_conv_block(xp_ssd, u_ssd[:, SSD_INNER:SSD_INNER + SSD_CONV_DIM], v768, 0, TB) + v768[4:5, :])
        yield
        causal = rr >= cc
        lo = lane128 < (LANE // 2)
        row_s = lax.broadcasted_iota(jnp.int32, (LANE, SSD_STATE), 0) < (LANE // 2)
        y_pairs = []
        for p in range(SSD_GROUPS):
            xs_p = xbc[:, p * LANE:(p + 1) * LANE]
            b_p = xbc[:, SSD_INNER + p * SSD_STATE:SSD_INNER + (p + 1) * SSD_STATE]
            c_p = xbc[:, SSD_INNER + SSD_GROUPS * SSD_STATE + p * SSD_STATE:
                      SSD_INNER + SSD_GROUPS * SSD_STATE + (p + 1) * SSD_STATE]
            cb = _mm_nt(c_p, b_p)
            h0, h1 = 2 * p, 2 * p + 1
            xdt = xs_p * jnp.where(lo, sp[:, SM_DT + h0:SM_DT + h0 + 1], sp[:, SM_DT + h1:SM_DT + h1 + 1])
            s_pair = s_ssd[p]
            ys = []
            for h in (h0, h1):
                gcol = gcum[:, SM_DT + h:SM_DT + h + 1]
                grow = gcum_t[SM_DT + h:SM_DT + h + 1, 0:TB]
                seg = jnp.exp(jnp.where(causal, gcol - grow, -jnp.inf))
                y_diag = _mm(cb * seg, xdt)
                y_off = _mm_nt(c_p * jnp.exp(gcol), s_pair)
                ys.append(y_diag + y_off)
                yield
            y_pairs.append(jnp.where(lo, ys[0], ys[1]))
            g0 = gcum[:, SM_DT + h0:SM_DT + h0 + 1]
            g1 = gcum[:, SM_DT + h1:SM_DT + h1 + 1]
            gl0 = g0[TB - 1:TB, :]
            gl1 = g1[TB - 1:TB, :]
            to_end = jnp.where(lo, jnp.exp(gl0 - g0), jnp.exp(gl1 - g1))
            s_ssd[p] = jnp.where(row_s, jnp.exp(gl0), jnp.exp(gl1)) * s_pair + _mm_tn(xdt * to_end, b_p)
            yield
        y = jnp.concatenate(y_pairs, axis=-1) + v256[8:9, :] * xbc[:, 0:SSD_INNER]
        yg = y * _silu(z)
        gs = SSD_INNER // SSD_GROUPS
        for p in range(SSD_GROUPS):
            mix_ref[:, LRU_WIDTH + p * gs:LRU_WIDTH + (p + 1) * gs] = _rms(
                yg[:, p * gs:(p + 1) * gs], v256[9:10, p * gs:(p + 1) * gs])

    def gdn():
        lane = lax.broadcasted_iota(jnp.int32, (TB, GDN_WIDTH), 1)
        hmask = [(lane >= h * GDN_HEAD_DIM) & (lane < (h + 1) * GDN_HEAD_DIM) for h in range(GDN_HEADS)]

        def by_head(vals):
            return jnp.where(hmask[0], vals[0], jnp.where(hmask[1], vals[1], jnp.where(hmask[2], vals[2], vals[3])))

        def head_sum(x):
            return by_head([jnp.sum(jnp.where(m, x, 0.0), axis=-1, keepdims=True) for m in hmask])

        u_gdn = useg(U_QKV, GDN_CONV_DIM + GDN_WIDTH)
        qkv = _silu(_conv_block(xp_gdn, u_gdn[:, 0:GDN_CONV_DIM], v768, 5, TB))
        qf = qkv[:, 0:GDN_WIDTH]
        kf = qkv[:, GDN_WIDTH:2 * GDN_WIDTH]
        vf = qkv[:, 2 * GDN_WIDTH:3 * GDN_WIDTH]
        qf = qf * lax.rsqrt(head_sum(qf * qf) + 1e-6) * (GDN_HEAD_DIM ** -0.5)
        kf = kf * lax.rsqrt(head_sum(kf * kf) + 1e-6)
        beta_f = by_head([sig[:, SM_BETA + h:SM_BETA + h + 1] for h in range(GDN_HEADS)])
        gc_f = by_head([gcum[:, SM_A + h:SM_A + h + 1] for h in range(GDN_HEADS)])
        chunk_shift = int(math.log2(QG))
        same_chunk = lax.shift_right_logical(rr, chunk_shift) == lax.shift_right_logical(cc, chunk_shift)
        incl = same_chunk & (rr >= cc)
        strict = same_chunk & (rr > cc)
        eg = jnp.exp(gc_f)
        kq = _mm_nt(jnp.concatenate([jnp.where(m, kf, 0.0) for m in hmask] + [jnp.where(m, qf, 0.0) for m in hmask],
                                    axis=0), kf)
        rhs = jnp.concatenate([beta_f * vf, beta_f * eg * kf], axis=1)
        yield
        decs, pws, rs = [], [], []
        for h in range(GDN_HEADS):
            gcol = gcum[:, SM_A + h:SM_A + h + 1]
            grow = gcum_t[SM_A + h:SM_A + h + 1, 0:TB]
            dec = jnp.exp(jnp.where(incl, gcol - grow, -jnp.inf))
            low = jnp.where(strict, sig[:, SM_BETA + h:SM_BETA + h + 1] * kq[h * TB:(h + 1) * TB] * dec, 0.0)
            decs.append(dec)
            pws.append(-low)
            rs.append(-low)
        mark(pws[GDN_HEADS - 1])
        yield
        for _ in range(chunk_shift - 1):
            for h in range(GDN_HEADS):
                pws[h] = _mm(pws[h], pws[h])
                mark(pws[h])
                if paced:
                    yield
                rs[h] = rs[h] + pws[h] + _mm(rs[h], pws[h])
                mark(rs[h])
                yield
        sols, qks = [], []
        for h in range(GDN_HEADS):
            sols.append(rhs + _mm(rs[h], rhs))
            qks.append(kq[(GDN_HEADS + h) * TB:(GDN_HEADS + h + 1) * TB] * decs[h])
            yield
        u_all = by_head([s[:, 0:GDN_WIDTH] for s in sols])
        w_all = by_head([s[:, GDN_WIDTH:2 * GDN_WIDTH] for s in sols])
        q_dec = qf * eg
        head_shift = int(math.log2(GDN_HEAD_DIM))
        r2 = lax.shift_right_logical(lax.broadcasted_iota(jnp.int32, (GDN_WIDTH, GDN_WIDTH), 0), head_shift)
        c2 = lax.shift_right_logical(lax.broadcasted_iota(jnp.int32, (GDN_WIDTH, GDN_WIDTH), 1), head_shift)
        blockdiag = r2 == c2
        s_bd = s_gdn[...]
        deltas, o_state = [], []
        for c in range(TB // QG):
            sl = slice(c * QG, (c + 1) * QG)
            ws = _mm(jnp.concatenate([w_all[sl], q_dec[sl]], axis=0), s_bd)
            delta = u_all[sl] - ws[0:QG]
            o_state.append(ws[QG:2 * QG])
            mark(delta)
            yield
            g_c = gc_f[sl]
            g_last = g_c[QG - 1:QG, :]
            s_bd = s_bd * jnp.exp(g_last) + jnp.where(blockdiag, _mm_tn(kf[sl] * jnp.exp(g_last - g_c), delta), 0.0)
            deltas.append(delta)
            yield
        s_gdn[...] = s_bd
        delta_all = jnp.concatenate(deltas, axis=0)
        o = jnp.concatenate(o_state, axis=0) + by_head([_mm(qks[h], delta_all) for h in range(GDN_HEADS)])
        yield
        ms = head_sum(o * o) * (1.0 / GDN_HEAD_DIM)
        o = o * lax.rsqrt(ms + RMS_EPS) * v256[10:11, :]
        zg = u_gdn[:, GDN_CONV_DIM:GDN_CONV_DIM + GDN_WIDTH]
        mix_ref[:, 2 * LRU_WIDTH:2 * LRU_WIDTH + GDN_WIDTH] = o * _silu(zg)

    def s5():
        row_n = lax.broadcasted_iota(jnp.int32, (SUBLANE, S5_N), 0)
        us5 = useg(U_S5, S5_WIDTH)
        bu = _mm(us5, bmat[...])
        yield
        lam_r, lam_i = v1024[0:1, :], v1024[1:2, :]
        pows = []
        pr, pi = lam_r, lam_i
        k = 1
        while k < SUBLANE:
            pows.append((k, jnp.where(row_n >= k, pr, 0.0), jnp.where(row_n >= k, pi, 0.0)))
            pr, pi = pr * pr - pi * pi, 2.0 * (pr * pi)
            k *= 2
        in_r = jnp.where(row_n == 0, lam_r, 0.0)
        in_i = jnp.where(row_n == 0, lam_i, 0.0)
        g_r = x_s5[0]
        g_i = x_s5[1]
        for i in range(NG):
            p_r = _after(pltpu.roll(g_r, 1, 0), sync[0])
            p_i = _after(pltpu.roll(g_i, 1, 0), sync[0])
            g_r = bu[i * SUBLANE:(i + 1) * SUBLANE, 0:S5_N] + (in_r * p_r - in_i * p_i)
            g_i = bu[i * SUBLANE:(i + 1) * SUBLANE, S5_N:2 * S5_N] + (in_r * p_i + in_i * p_r)
            for k, pr, pi in pows:
                s_r = pltpu.roll(g_r, k, 0)
                s_i = pltpu.roll(g_i, k, 0)
                g_r, g_i = g_r + (pr * s_r - pi * s_i), g_i + (pr * s_i + pi * s_r)
            x5_scr[pl.ds(i * SUBLANE, SUBLANE), 0:S5_N] = g_r
            x5_scr[pl.ds(i * SUBLANE, SUBLANE), S5_N:2 * S5_N] = g_i
            yield
        x_s5[0] = g_r
        x_s5[1] = g_i
        y5 = _mm(x5_scr[...], cmat[...])
        y5 = _gelu(y5 + v256[11:12, :] * us5)
        yield
        mix_ref[:, 3 * LRU_WIDTH:3 * LRU_WIDTH + S5_WIDTH] = y5 * jax.nn.sigmoid(_mm(y5, wglu[...]) + v256[12:13, :])

    def finish():
        for xp in (xp_lru, xp_ssd, xp_gdn):
            xp[pl.ds(0, HIST), :] = xp[pl.ds(TB, HIST), :]

        pl.when(t == NT - 1)(_emit)

    def _emit():
        lru_h_o[...] = h_lru[0:1, :]
        lru_b_o[...] = xp_lru[pl.ds(HIST - K1, K1), :]
        ssd_b_o[...] = xp_ssd[pl.ds(HIST - K1, K1), :]
        gdn_b_o[...] = xp_gdn[pl.ds(HIST - K1, K1), :]
        ssd_s_o[...] = s_ssd[...]
        gdn_s_o[...] = s_gdn[...]
        s5r_o[...] = x_s5[0, SUBLANE - 1:SUBLANE, :]
        s5i_o[...] = x_s5[1, SUBLANE - 1:SUBLANE, :]

    return [gdn(), s5(), ssd(), lru()], finish


def _mixer(src, states, ls, pk, l, fused):
    B, T, _ = src.shape
    TB = min(T, MIXER_BLOCK)
    QG = min(TB, 64)
    NT = T // TB
    NB = SHORT_SEQ_BATCH if (B % SHORT_SEQ_BATCH == 0 and TB < MIXER_BLOCK) else 1

    def per_b(shape):
        nd = len(shape)
        return pl.BlockSpec((NB,) + tuple(shape[1:]), lambda b, t: (b,) + (0,) * (nd - 1))

    def per_b_of_layer(shape):
        nd = len(shape)
        return pl.BlockSpec((None, NB) + tuple(shape[2:]), lambda b, t: (ls, b) + (0,) * (nd - 2))

    def per_l(arr):
        nd = arr.ndim
        return pl.BlockSpec((None,) + tuple(arr.shape[1:]), lambda b, t: (l,) + (0,) * (nd - 1))

    proj = (pk["g_mix"], pk["w_in"]) if fused else ()
    params = (pk["v256"], pk["v768"], pk["v128"], pk["v1024"], pk["wri"], pk["bmat"], pk["cmat"], pk["wglu"])
    state_shapes = [s.shape[1:] for s in states]
    out_shape = [jax.ShapeDtypeStruct((B, T, D_MODEL), F32)] + [jax.ShapeDtypeStruct(s, F32) for s in state_shapes]
    outs = pl.pallas_call(
        functools.partial(_mixer_kernel, TB, QG, NT, NB, fused),
        grid=(B // NB, NT),
        in_specs=[pl.BlockSpec((NB, TB, src.shape[2]), lambda b, t: (b, t, 0))] + [per_l(p) for p in proj]
                 + [per_b_of_layer(s.shape) for s in states] + [per_l(p) for p in params],
        out_specs=[pl.BlockSpec((NB, TB, D_MODEL), lambda b, t: (b, t, 0))] + [per_b(s) for s in state_shapes],
        out_shape=out_shape,
        scratch_shapes=[pltpu.VMEM((NB, TB + HIST, LRU_WIDTH), F32),
                        pltpu.VMEM((NB, TB + HIST, SSD_CONV_DIM), F32),
                        pltpu.VMEM((NB, TB + HIST, GDN_CONV_DIM), F32),
                        pltpu.VMEM((NB, SUBLANE, LRU_WIDTH), F32),
                        pltpu.VMEM((NB, SSD_GROUPS, LANE, SSD_STATE), F32),
                        pltpu.VMEM((NB, GDN_WIDTH, GDN_WIDTH), F32),
                        pltpu.VMEM((NB, 2, SUBLANE, S5_N), F32),
                        pltpu.VMEM((NB, TB, LRU_WIDTH), F32),
                        pltpu.VMEM((NB, TB, 2 * S5_N), F32)],
        compiler_params=pltpu.CompilerParams(dimension_semantics=("arbitrary", "arbitrary"),
                                             vmem_limit_bytes=VMEM_LIMIT),
        name="mixer",
    )(src, *proj, *states, *params)
    return outs[0], tuple(outs[1:])


def _attend(q, k_ref, v_ref):
    heads = []
    for h in range(MEM_HEADS):
        sl = slice(h * MEM_HEAD_DIM, (h + 1) * MEM_HEAD_DIM)
        s = _mm_nt(q[:, sl], k_ref[:, sl]) * (MEM_HEAD_DIM ** -0.5)
        e = jnp.exp(s - jnp.max(s, axis=-1, keepdims=True))
        p = e / jnp.sum(e, axis=-1, keepdims=True)
        heads.append(_mm(p, v_ref[:, sl]))
    return jnp.concatenate(heads, axis=-1)


def _attn_kernel(x_ref, mix_ref, wout, g_ref, wq, k_ref, v_ref, wo, o_ref):
    x1 = x_ref[0] + _mm(mix_ref[0], wout[...])
    q = _mm(_rms(x1, g_ref[...]), wq[...])
    o_ref[0] = x1 + _mm(_attend(q, k_ref, v_ref), wo[...])


def _attn_short_kernel(T, NB, x_ref, mix_ref, wout, g_ref, wq, k_ref, v_ref, wo, o_ref, x1_scr, q_scr, a_scr):
    b = pl.program_id(0)

    @pl.when(b == 0)
    def _project_in():
        x1 = x_ref[...] + _mm(mix_ref[...], wout[...])
        x1_scr[...] = x1
        q_scr[...] = _mm(_rms(x1, g_ref[...]), wq[...])

    for bb in range(NB):
        rows = pl.ds(pl.multiple_of((b * NB + bb) * T, T), T)
        a_scr[rows, :] = _attend(q_scr[rows, :], k_ref.at[bb], v_ref.at[bb])

    @pl.when(b == pl.num_programs(0) - 1)
    def _project_out():
        o_ref[...] = x1_scr[...] + _mm(a_scr[...], wo[...])


def _attn(x, mix, mem_k, mem_v, lkv, pk, l):
    B, T, _ = x.shape

    def per_l(arr):
        nd = arr.ndim
        return pl.BlockSpec((None,) + tuple(arr.shape[1:]), lambda *i: (l,) + (0,) * (nd - 1),
                            pipeline_mode=pl.Buffered(1))

    weights = (pk["w_out"], pk["g_cross"], pk["w_cq"])
    if T >= MIXER_BLOCK:
        TB = min(T, ROW_TILE)
        blk = pl.BlockSpec((1, TB, D_MODEL), lambda b, t: (b, t, 0))
        kv = pl.BlockSpec((None, None, MEM_TOKENS, D_MODEL), lambda b, t: (lkv, b, 0, 0))
        return pl.pallas_call(
            _attn_kernel,
            grid=(B, T // TB),
            in_specs=[blk, blk] + [per_l(w) for w in weights] + [kv, kv, per_l(pk["w_co"])],
            out_specs=blk,
            out_shape=jax.ShapeDtypeStruct((B, T, D_MODEL), F32),
            compiler_params=pltpu.CompilerParams(dimension_semantics=("parallel", "parallel"),
                                                 vmem_limit_bytes=VMEM_LIMIT),
            name="attn",
        )(x, mix, *weights, mem_k, mem_v, pk["w_co"])
    R = B * T
    NB = SHORT_SEQ_BATCH // 2 if B % (SHORT_SEQ_BATCH // 2) == 0 else 1
    rows = pl.BlockSpec((R, D_MODEL), lambda b: (0, 0))
    kv = pl.BlockSpec((None, NB, MEM_TOKENS, D_MODEL), lambda b: (lkv, b, 0, 0))
    out = pl.pallas_call(
        functools.partial(_attn_short_kernel, T, NB),
        grid=(B // NB,),
        in_specs=[rows, rows] + [per_l(w) for w in weights] + [kv, kv, per_l(pk["w_co"])],
        out_specs=rows,
        out_shape=jax.ShapeDtypeStruct((R, D_MODEL), F32),
        scratch_shapes=[pltpu.VMEM((R, D_MODEL), F32)] * 3,
        compiler_params=pltpu.CompilerParams(dimension_semantics=("arbitrary",), vmem_limit_bytes=VMEM_LIMIT),
        name="attn_short",
    )(x.reshape(R, D_MODEL), mix.reshape(R, D_MODEL), *weights, mem_k, mem_v, pk["w_co"])
    return out.reshape(B, T, D_MODEL)


def _ffn_kernel(final, x_ref, g_ref, wg, wu, wd, gf_ref, o_ref):
    x = x_ref[...]
    h = _rms(x, g_ref[...]).astype(MXU_DTYPE)
    y = x + _mm(_silu(_mm(h, wg[...])) * _mm(h, wu[...]), wd[...])
    if final:
        y = _rms(y, gf_ref[...])
    o_ref[...] = y


def _ffn(x2d, pk, l, final):
    R = x2d.shape[0]
    TM = min(R, ROW_TILE)

    def per_l(arr):
        nd = arr.ndim
        return pl.BlockSpec((None,) + tuple(arr.shape[1:]), lambda i: (l,) + (0,) * (nd - 1),
                            pipeline_mode=pl.Buffered(1))

    blk = pl.BlockSpec((TM, D_MODEL), lambda i: (i, 0))
    return pl.pallas_call(
        functools.partial(_ffn_kernel, final),
        grid=(R // TM,),
        in_specs=[blk, per_l(pk["g_ffn"]), per_l(pk["w_gate"]), per_l(pk["w_up"]), per_l(pk["w_down"]),
                  pl.BlockSpec((1, D_MODEL), lambda i: (0, 0))],
        out_specs=blk,
        out_shape=jax.ShapeDtypeStruct((R, D_MODEL), F32),
        compiler_params=pltpu.CompilerParams(dimension_semantics=("parallel",), vmem_limit_bytes=VMEM_LIMIT,
                                             allow_input_fusion=[False, False, True, True, True, False]),
        name="ffn",
    )(x2d, pk["g_ffn"], pk["w_gate"], pk["w_up"], pk["w_down"], pk["g_final"])


def _memkv_kernel(m_ref, g_ref, wk, wv, k_ref, v_ref):
    h = _rms(m_ref[0], g_ref[...]).astype(MXU_DTYPE)
    k_ref[0] = _mm(h, wk[...])
    v_ref[0] = _mm(h, wv[...])


def _memory_kv(mem, pk, l):
    B = mem.shape[0]

    def per_l(arr):
        nd = arr.ndim
        return pl.BlockSpec((None,) + tuple(arr.shape[1:]), lambda b: (l,) + (0,) * (nd - 1))

    blk = pl.BlockSpec((1, MEM_TOKENS, D_MODEL), lambda b: (b, 0, 0))
    return pl.pallas_call(
        _memkv_kernel,
        grid=(B,),
        in_specs=[blk, per_l(pk["g_mem"]), per_l(pk["w_ck"]), per_l(pk["w_cv"])],
        out_specs=[blk, blk],
        out_shape=[jax.ShapeDtypeStruct((B, MEM_TOKENS, D_MODEL), F32)] * 2,
        compiler_params=pltpu.CompilerParams(dimension_semantics=("parallel",), vmem_limit_bytes=VMEM_LIMIT),
        name="memory_kv",
    )(mem, pk["g_mem"], pk["w_ck"], pk["w_cv"])


def _block_diag(w):
    L, n, a, b = w.shape
    return jnp.einsum('lnab,nm->lnamb', w, jnp.eye(n, dtype=w.dtype)).reshape(L, n * a, n * b)


def _pack_rows(rows, n_rows):
    L, width = rows[0].shape[0], rows[0].shape[-1]
    rows = [r.reshape(L, -1, width) for r in rows]
    used = sum(r.shape[1] for r in rows)
    return jnp.concatenate(rows + [jnp.zeros((L, n_rows - used, width), F32)], axis=1)


def _prepare(norm_mix_g, w_in, w_out, lru_conv_w, lru_conv_b, lru_w_r, lru_b_r, lru_w_i, lru_b_i, lru_lambda,
             ssd_conv_w, ssd_conv_b, ssd_dt_bias, ssd_a_log, ssd_d, ssd_norm_g,
             gdn_conv_w, gdn_dt_bias, gdn_a_log, gdn_norm_g,
             s5_a_re, s5_a_im, s5_log_dt, s5_b_re, s5_b_im, s5_c_re, s5_c_im, s5_d, s5_w_glu, s5_b_glu,
             norm_mem_g, norm_cross_g, w_cq, w_ck, w_cv, w_co,
             norm_ffn_g, w_ffn_gate, w_ffn_up, w_ffn_down, norm_final_g):
    L = DEPTH
    bf = MXU_DTYPE
    pk = {}
    pk["g_mix"] = norm_mix_g.reshape(L, 1, D_MODEL)
    pk["w_in"] = jnp.concatenate(
        [w_in[:, :, 0:1536], w_in[:, :, 1540:2564], w_in[:, :, 2572:2828], w_in[:, :, 1536:1540],
         w_in[:, :, 2564:2572], jnp.zeros((L, D_MODEL, U_COLS - IN_COLS), F32)], axis=2).astype(bf)
    pk["w_out"] = w_out.astype(bf)
    sp_lam = jax.nn.softplus(-lru_lambda)
    pk["v256"] = _pack_rows(
        [lru_conv_w, lru_conv_b, lru_b_r, lru_b_i, sp_lam, jnp.repeat(ssd_d, SSD_INNER // SSD_HEADS, axis=1),
         ssd_norm_g, jnp.tile(gdn_norm_g, (1, GDN_HEADS)), s5_d, s5_b_glu], 16)
    pk["v768"] = _pack_rows([ssd_conv_w, ssd_conv_b, gdn_conv_w], 16)
    zeros4 = jnp.zeros((L, 4), F32)
    bias = jnp.concatenate([ssd_dt_bias, zeros4, gdn_dt_bias, jnp.zeros((L, LANE - 12), F32)], axis=1)
    avec = jnp.concatenate([-jnp.exp(ssd_a_log.astype(F32)), zeros4, -jnp.exp(gdn_a_log),
                            jnp.zeros((L, LANE - 12), F32)], axis=1)
    pk["v128"] = _pack_rows([bias, avec], 8)
    pk["wri"] = jnp.concatenate([_block_diag(lru_w_r), _block_diag(lru_w_i)], axis=-1).astype(bf)
    a_re = s5_a_re.astype(F32)
    a_im = s5_a_im.astype(F32)
    step = jnp.exp(s5_log_dt.astype(F32))[:, :, None]
    mag = jnp.exp(a_re * step)
    ang = a_im * step
    lb_re, lb_im = mag * jnp.cos(ang), mag * jnp.sin(ang)
    den = a_re * a_re + a_im * a_im
    f_re = ((lb_re - 1.0) * a_re + lb_im * a_im) / den
    f_im = (lb_im * a_re - (lb_re - 1.0) * a_im) / den
    b_re = s5_b_re.astype(F32)
    b_im = s5_b_im.astype(F32)
    bb_re = f_re[..., None] * b_re - f_im[..., None] * b_im
    bb_im = f_re[..., None] * b_im + f_im[..., None] * b_re
    pk["v1024"] = _pack_rows([lb_re.reshape(L, S5_N), lb_im.reshape(L, S5_N)], 8)
    pk["bmat"] = jnp.concatenate([_block_diag(jnp.swapaxes(bb_re, 2, 3)), _block_diag(jnp.swapaxes(bb_im, 2, 3))],
                                 axis=-1).astype(bf)
    pk["cmat"] = jnp.concatenate([_block_diag(jnp.swapaxes(s5_c_re, 2, 3)), -_block_diag(jnp.swapaxes(s5_c_im, 2, 3))],
                                 axis=1).astype(bf)
    pk["wglu"] = s5_w_glu.astype(bf)
    pk["g_mem"] = norm_mem_g.reshape(L, 1, D_MODEL)
    pk["g_cross"] = norm_cross_g.reshape(L, 1, D_MODEL)
    pk["w_cq"], pk["w_ck"], pk["w_cv"], pk["w_co"] = (w.astype(bf) for w in (w_cq, w_ck, w_cv, w_co))
    pk["g_ffn"] = norm_ffn_g.reshape(L, 1, D_MODEL)
    pk["w_gate"], pk["w_up"], pk["w_down"] = (w.astype(bf) for w in (w_ffn_gate, w_ffn_up, w_ffn_down))
    pk["g_final"] = norm_final_g.reshape(1, D_MODEL)
    return pk


def _states_to_kernel(lru_h, lru_buf, ssd, ssd_buf, gdn, gdn_buf, s5_re, s5_im):
    L, B = lru_h.shape[:2]
    eye = jnp.eye(GDN_HEADS, dtype=F32)
    gdn_bd = jnp.einsum('lbhkv,hg->lbhkgv', gdn, eye).reshape(L, B, GDN_WIDTH, GDN_WIDTH)
    return (lru_h.reshape(L, B, 1, LRU_WIDTH), lru_buf, ssd.reshape(L, B, SSD_GROUPS, LANE, SSD_STATE), ssd_buf,
            gdn_bd, gdn_buf, s5_re.reshape(L, B, 1, S5_N), s5_im.reshape(L, B, 1, S5_N))


def _states_from_kernel(per_layer):
    lru_h, lru_buf, ssd, ssd_buf, gdn_bd, gdn_buf, s5_re, s5_im = (
        jnp.stack([st[j] for st in per_layer], axis=0) for j in range(8))
    L, B = lru_h.shape[:2]
    gdn = jnp.einsum('lbhkhv->lbhkv', gdn_bd.reshape(L, B, GDN_HEADS, GDN_HEAD_DIM, GDN_HEADS, GDN_HEAD_DIM))
    return (lru_h.reshape(L, B, LRU_WIDTH), lru_buf, ssd.reshape(L, B, SSD_HEADS, SSD_INNER // SSD_HEADS, SSD_STATE),
            ssd_buf, gdn, gdn_buf, s5_re.reshape(L, B, S5_GROUPS, S5_STATE), s5_im.reshape(L, B, S5_GROUPS, S5_STATE))


def _layer(l, x, mem_k, mem_v, lkv, states, ls, pk, final):
    B, T, _ = x.shape
    if T >= MIXER_BLOCK:
        mix, new_states = _mixer(x, states, ls, pk, l, True)
    else:
        u = _in_proj(x.reshape(B * T, D_MODEL), pk["g_mix"], pk["w_in"], l).reshape(B, T, U_COLS)
        mix, new_states = _mixer(u, states, ls, pk, l, False)
    x = _attn(x, mix, mem_k, mem_v, lkv, pk, l)
    x = _ffn(x.reshape(B * T, D_MODEL), pk, l, final).reshape(B, T, D_MODEL)
    return x, new_states


def kernel(x_prompt, x_sample, mem_prompt, state_lru_h, cache_lru_conv, state_ssd, cache_ssd_conv, state_gdn, cache_gdn_conv, state_s5_re, state_s5_im, cache_mem_k, cache_mem_v, norm_mix_g, w_in, w_out, lru_conv_w, lru_conv_b, lru_w_r, lru_b_r, lru_w_i, lru_b_i, lru_lambda, ssd_conv_w, ssd_conv_b, ssd_dt_bias, ssd_a_log, ssd_d, ssd_norm_g, gdn_conv_w, gdn_dt_bias, gdn_a_log, gdn_norm_g, s5_a_re, s5_a_im, s5_log_dt, s5_b_re, s5_b_im, s5_c_re, s5_c_im, s5_d, s5_w_glu, s5_b_glu, norm_mem_g, norm_cross_g, w_cq, w_ck, w_cv, w_co, norm_ffn_g, w_ffn_gate, w_ffn_up, w_ffn_down, norm_final_g):
    pk = _prepare(norm_mix_g, w_in, w_out, lru_conv_w, lru_conv_b, lru_w_r, lru_b_r, lru_w_i, lru_b_i, lru_lambda,
                  ssd_conv_w, ssd_conv_b, ssd_dt_bias, ssd_a_log, ssd_d, ssd_norm_g,
                  gdn_conv_w, gdn_dt_bias, gdn_a_log, gdn_norm_g,
                  s5_a_re, s5_a_im, s5_log_dt, s5_b_re, s5_b_im, s5_c_re, s5_c_im, s5_d, s5_w_glu, s5_b_glu,
                  norm_mem_g, norm_cross_g, w_cq, w_ck, w_cv, w_co,
                  norm_ffn_g, w_ffn_gate, w_ffn_up, w_ffn_down, norm_final_g)
    bp = x_prompt.shape[0]
    bs = x_sample.shape[0]
    k1 = CONV_WIDTH - 1
    zero_states = _states_to_kernel(
        jnp.zeros((1, bp, LRU_WIDTH), F32), jnp.zeros((1, bp, k1, LRU_WIDTH), F32),
        jnp.zeros((1, bp, SSD_HEADS, SSD_INNER // SSD_HEADS, SSD_STATE), F32), jnp.zeros((1, bp, k1, SSD_CONV_DIM), F32),
        jnp.zeros((1, bp, GDN_HEADS, GDN_HEAD_DIM, GDN_HEAD_DIM), F32), jnp.zeros((1, bp, k1, GDN_CONV_DIM), F32),
        jnp.zeros((1, bp, S5_GROUPS, S5_STATE), F32), jnp.zeros((1, bp, S5_GROUPS, S5_STATE), F32))
    sample_states = _states_to_kernel(state_lru_h, cache_lru_conv, state_ssd, cache_ssd_conv,
                                      state_gdn, cache_gdn_conv, state_s5_re, state_s5_im)
    cache_k = cache_mem_k.reshape(DEPTH, bs, MEM_TOKENS, D_MODEL)
    cache_v = cache_mem_v.reshape(DEPTH, bs, MEM_TOKENS, D_MODEL)
    xp, xs = x_prompt, x_sample
    p_states, s_states, p_mk, p_mv = [], [], [], []
    for l in range(DEPTH):
        final = l == DEPTH - 1
        mk, mv = _memory_kv(mem_prompt, pk, l)
        xp, sp = _layer(l, xp, mk[None], mv[None], 0, zero_states, 0, pk, final)
        xs, ss = _layer(l, xs, cache_k, cache_v, l, sample_states, l, pk, final)
        p_states.append(sp)
        s_states.append(ss)
        p_mk.append(mk.reshape(bp, MEM_TOKENS, MEM_HEADS, MEM_HEAD_DIM))
        p_mv.append(mv.reshape(bp, MEM_TOKENS, MEM_HEADS, MEM_HEAD_DIM))

    return ((xp, xs) + _states_from_kernel(p_states) + (jnp.stack(p_mk, axis=0), jnp.stack(p_mv, axis=0))
            + _states_from_kernel(s_states))
```
